```python
import jax
import jax.numpy as jnp
from jax import lax
import numpy as np

D_MODEL = 1024
BATCH = 2
SEQ = 8192
DEPTH = 2

HEAD_DIM = 64
N_BRANCH = 4
BRANCH_HEADS = 4
BRANCH_WIDTH = BRANCH_HEADS * HEAD_DIM
ROPE_THETA = 10000.0
QBLK = 128
NEG = -1e30
TINY = 1e-30
EPS = 1e-6
ATTN_SCALE = HEAD_DIM ** -0.5
A_LATENT = 128
IDX_HEADS = 8
IDX_DIM = 32
DSA_TOPK = 256
MOBA_BLOCK = 256
MOBA_TOPK = 3
CMP_LEN = 32
CMP_STRIDE = 16
CMP_HIDDEN = 128
SLC_BLOCK = 64
SLC_TOPK = 16
NSA_WINDOW = 512
FORCE_SCORE = 1e9
NSA_KV = 6
SWA_WINDOW = 128
D_KV_HEADS = 2
D_FF = 256 * ((8 * D_MODEL // 3 + 255) // 256)
CONV_W = 3

IN_WIDTHS = (
    BRANCH_WIDTH,
    A_LATENT,
    IDX_HEADS * IDX_DIM,
    IDX_DIM,
    IDX_HEADS,
    BRANCH_WIDTH,
    BRANCH_WIDTH,
    BRANCH_WIDTH,
    BRANCH_WIDTH,
    NSA_KV * HEAD_DIM,
    3 * BRANCH_HEADS,
    BRANCH_WIDTH,
    D_KV_HEADS * HEAD_DIM,
    D_KV_HEADS * HEAD_DIM,
    N_BRANCH * D_MODEL,
)
N_IN = sum(IN_WIDTHS)

kernel_name = 'hybrid_dsa_moba_nsa_swa_convffn'


def rms_norm(x, g):
    xf = x.astype(jnp.float32)
    y = xf * lax.rsqrt(jnp.mean(jnp.square(xf), axis=-1, keepdims=True) + EPS)
    return (y * g.astype(jnp.float32)).astype(x.dtype)


def rope_tables(seq, dim):
    inv = ROPE_THETA ** (-jnp.arange(0, dim, 2, dtype=jnp.float32) / dim)
    ang = jnp.arange(seq, dtype=jnp.float32)[:, None] * inv[None, :]
    return jnp.cos(ang), jnp.sin(ang)


def apply_rope(x, cs):
    cos, sin = cs
    shape = (1, cos.shape[0]) + (1,) * (x.ndim - 3) + (cos.shape[1],)
    c = cos.reshape(shape).astype(x.dtype)
    s = sin.reshape(shape).astype(x.dtype)
    x1, x2 = jnp.split(x, 2, axis=-1)
    return jnp.concatenate([x1 * c - x2 * s, x2 * c + x1 * s], axis=-1)


def masked_softmax(s, ok, sink=None):
    s = jnp.where(ok, s, NEG)
    m = jnp.max(s, axis=-1, keepdims=True)
    if sink is not None:
        m = jnp.maximum(m, sink)
    e = jnp.where(ok, jnp.exp(s - m), 0.0)
    den = jnp.sum(e, axis=-1, keepdims=True)
    if sink is not None:
        den = den + jnp.exp(sink - m)
    return e / jnp.maximum(den, TINY)


def gather_rows(table, idx):
    return jax.vmap(lambda t, i: t[i])(table, idx)


def banded_attention(q, k, v, window, sink=None):
    B, S, KV, G, dh = q.shape
    nb = S // QBLK
    nprev = window // QBLK
    nk = (nprev + 1) * QBLK
    padw = ((0, 0), (nprev * QBLK, 0), (0, 0), (0, 0))
    kp = jnp.pad(k, padw).reshape(B, nb + nprev, QBLK, KV, dh)
    vp = jnp.pad(v, padw).reshape(B, nb + nprev, QBLK, KV, dh)
    kband = jnp.concatenate([kp[:, o:o + nb] for o in range(nprev + 1)], axis=2)
    vband = jnp.concatenate([vp[:, o:o + nb] for o in range(nprev + 1)], axis=2)
    qb = q.reshape(B, nb, QBLK, KV, G, dh)
    s = jnp.einsum('bnqcgd,bnkcd->bncgqk', qb, kband, preferred_element_type=jnp.float32) * ATTN_SCALE
    blk = jnp.arange(nb)[:, None]
    tpos = blk * QBLK + jnp.arange(QBLK)[None, :]
    kpos = (blk - nprev) * QBLK + jnp.arange(nk)[None, :]
    diff = tpos[:, :, None] - kpos[:, None, :]
    ok = (diff >= 0) & (diff < window) & (kpos[:, None, :] >= 0)
    ok = ok[None, :, None, None]
    sk = None if sink is None else sink.astype(jnp.float32)[None, None, :, :, None, None]
    p = masked_softmax(s, ok, sk)
    o = jnp.einsum('bncgqk,bnkcd->bnqcgd', p.astype(v.dtype), vband)
    return o.reshape(B, S, KV, G, dh)


def dsa_attention(q, k, v, iq, ik, iw):
    B, S, H, dh = q.shape
    topk = min(DSA_TOPK, S // 4)
    kpos = jnp.arange(S)
    ikf = ik.astype(jnp.float32)

    def one_block(i):
        t0 = i * QBLK
        qb = lax.dynamic_slice_in_dim(q, t0, QBLK, axis=1)
        iqb = lax.dynamic_slice_in_dim(iq, t0, QBLK, axis=1).astype(jnp.float32)
        iwb = lax.dynamic_slice_in_dim(iw, t0, QBLK, axis=1).astype(jnp.float32)
        tpos = t0 + jnp.arange(QBLK)
        dots = jnp.einsum('bqhd,bkd->bqkh', iqb, ikf)
        score = jnp.einsum('bqkh,bqh->bqk', jax.nn.relu(dots), iwb)
        score = jnp.where(kpos[None, None, :] <= tpos[None, :, None], score, NEG)
        _, sel = lax.top_k(score, topk)
        ks = gather_rows(k, sel)
        vs = gather_rows(v, sel)
        s = jnp.einsum('bqhd,bqkd->bhqk', qb, ks, preferred_element_type=jnp.float32) * ATTN_SCALE
        ok = (sel <= tpos[None, :, None])[:, None]
        p = masked_softmax(s, ok)
        return jnp.einsum('bhqk,bqkd->bqhd', p.astype(v.dtype), vs)

    out = lax.map(one_block, jnp.arange(S // QBLK))
    return out.transpose(1, 0, 2, 3, 4).reshape(B, S, H, dh)


def moba_attention(q, k, v):
    B, S, H, dh = q.shape
    nblk = -(-S // MOBA_BLOCK)
    pad = nblk * MOBA_BLOCK - S
    padw = ((0, 0), (0, pad), (0, 0), (0, 0))
    kp = jnp.pad(k, padw)
    vp = jnp.pad(v, padw)
    nsel = min(MOBA_TOPK, nblk - 1)
    kbh = kp.reshape(B, nblk, MOBA_BLOCK, H, dh).transpose(0, 3, 1, 2, 4)
    vbh = vp.reshape(B, nblk, MOBA_BLOCK, H, dh).transpose(0, 3, 1, 2, 4)
    kmean = jnp.mean(kbh.astype(jnp.float32), axis=3)
    blk_ids = jnp.arange(nblk)

    def one_block(i):
        t0 = i * QBLK
        qb = lax.dynamic_slice_in_dim(q, t0, QBLK, axis=1)
        tpos = t0 + jnp.arange(QBLK)
        own = t0 // MOBA_BLOCK
        ko = lax.dynamic_slice_in_dim(kp, own * MOBA_BLOCK, MOBA_BLOCK, axis=1)
        vo = lax.dynamic_slice_in_dim(vp, own * MOBA_BLOCK, MOBA_BLOCK, axis=1)
        opos = own * MOBA_BLOCK + jnp.arange(MOBA_BLOCK)
        s_own = jnp.einsum('bqhd,bkhd->bhqk', qb, ko, preferred_element_type=jnp.float32) * ATTN_SCALE
        ok_own = jnp.broadcast_to((opos[None, :] <= tpos[:, None])[None, None], s_own.shape)
        if nsel == 0:
            p = masked_softmax(s_own, ok_own)
            return jnp.einsum('bhqk,bkhd->bqhd', p.astype(v.dtype), vo)
        gate = jnp.einsum('bqhd,bhjd->bhqj', qb.astype(jnp.float32), kmean)
        gate = jnp.where(blk_ids < own, gate, NEG)
        _, sel = lax.top_k(gate, nsel)
        ks = jax.vmap(gather_rows)(kbh, sel)
        vs = jax.vmap(gather_rows)(vbh, sel)
        s_sel = jnp.einsum('bqhd,bhqnkd->bhqnk', qb, ks, preferred_element_type=jnp.float32) * ATTN_SCALE
        s_sel = s_sel.reshape(B, H, QBLK, nsel * MOBA_BLOCK)
        ok_sel = jnp.repeat(sel < own, MOBA_BLOCK, axis=-1)
        p = masked_softmax(jnp.concatenate([s_own, s_sel], axis=-1),
                           jnp.concatenate([ok_own, ok_sel], axis=-1)).astype(v.dtype)
        p_own = p[..., :MOBA_BLOCK]
        p_sel = p[..., MOBA_BLOCK:].reshape(B, H, QBLK, nsel, MOBA_BLOCK)
        return (jnp.einsum('bhqk,bkhd->bqhd', p_own, vo)
                + jnp.einsum('bhqnk,bhqnkd->bqhd', p_sel, vs))

    out = lax.map(one_block, jnp.arange(S // QBLK))
    return out.transpose(1, 0, 2, 3, 4).reshape(B, S, H, dh)


def compress_blocks(blocks, pe, w1, w2):
    B, N, L, dh = blocks.shape
    z = (blocks + pe).reshape(B, N, L * dh)
    return jax.nn.gelu(z @ w1) @ w2


def dsa_mixer(q_raw, lat_raw, iq_raw, ik_raw, iw_raw, qk_g, lat_g, kv_up, idx_k_g, cs, cs_idx):
    B, S, _ = q_raw.shape
    q = apply_rope(rms_norm(q_raw.reshape(B, S, BRANCH_HEADS, HEAD_DIM), qk_g[0]), cs)
    kv = rms_norm(lat_raw, lat_g) @ kv_up
    k, v = jnp.split(kv, 2, axis=-1)
    k = apply_rope(rms_norm(k, qk_g[1]), cs)
    iq = apply_rope(iq_raw.reshape(B, S, IDX_HEADS, IDX_DIM), cs_idx)
    ik = apply_rope(rms_norm(ik_raw, idx_k_g), cs_idx)
    iw = iw_raw * (IDX_HEADS ** -0.5 * IDX_DIM ** -0.5)
    return dsa_attention(q, k, v, iq, ik, iw)


def moba_mixer(q_raw, k_raw, v_raw, qk_g, cs):
    B, S, _ = q_raw.shape
    hs = (B, S, BRANCH_HEADS, HEAD_DIM)
    q = apply_rope(rms_norm(q_raw.reshape(hs), qk_g[0]), cs)
    k = apply_rope(rms_norm(k_raw.reshape(hs), qk_g[1]), cs)
    return moba_attention(q, k, v_raw.reshape(hs))


def nsa_mixer(q_raw, kv_raw, g_raw, qk_g, cmp_pe, cmp_w1, cmp_w2, cs):
    B, S, _ = q_raw.shape
    H, dh = BRANCH_HEADS, HEAD_DIM
    q = apply_rope(rms_norm(q_raw.reshape(B, S, H, dh), qk_g[0]), cs)
    kv = kv_raw.reshape(B, S, NSA_KV, dh)
    kc_raw, vc_raw = kv[:, :, 0], kv[:, :, 1]
    k_slc = apply_rope(rms_norm(kv[:, :, 2], qk_g[2]), cs)
    v_slc = kv[:, :, 3]
    k_win = apply_rope(rms_norm(kv[:, :, 4], qk_g[3]), cs)
    v_win = kv[:, :, 5]
    ncmp = (S - CMP_LEN) // CMP_STRIDE + 1
    gidx = np.arange(ncmp)[:, None] * CMP_STRIDE + np.arange(CMP_LEN)[None, :]
    k_cmp = rms_norm(compress_blocks(apply_rope(kc_raw, cs)[:, gidx], cmp_pe[0], cmp_w1[0], cmp_w2[0]), qk_g[1])
    v_cmp = compress_blocks(vc_raw[:, gidx], cmp_pe[1], cmp_w1[1], cmp_w2[1])
    cmp_end = jnp.asarray(np.arange(ncmp) * CMP_STRIDE + CMP_LEN - 1)
    nslc = S // SLC_BLOCK
    ntop = min(SLC_TOPK, nslc)
    c_start = np.arange(ncmp) * CMP_STRIDE
    s_start = np.arange(nslc) * SLC_BLOCK
    cmp_to_slc = jnp.asarray(((c_start[:, None] < s_start[None, :] + SLC_BLOCK)
                              & (c_start[:, None] + CMP_LEN > s_start[None, :])).astype(np.float32))
    ks_blocks = k_slc.reshape(B, nslc, SLC_BLOCK, dh)
    vs_blocks = v_slc.reshape(B, nslc, SLC_BLOCK, dh)
    jj = jnp.arange(nslc)

    def one_block(i):
        t0 = i * QBLK
        qb = lax.dynamic_slice_in_dim(q, t0, QBLK, axis=1)
        tpos = t0 + jnp.arange(QBLK)
        s_c = jnp.einsum('bqhd,bnd->bhqn', qb, k_cmp, preferred_element_type=jnp.float32) * ATTN_SCALE
        p_c = masked_softmax(s_c, (cmp_end[None, :] <= tpos[:, None])[None, None])
        o_c = jnp.einsum('bhqn,bnd->bqhd', p_c.astype(q.dtype), v_cmp)
        imp = jnp.einsum('bhqn,nj->bqj', p_c, cmp_to_slc)
        cur = tpos // SLC_BLOCK
        forced = (jj[None, :] == 0) | (jj[None, :] == cur[:, None]) | (jj[None, :] == cur[:, None] - 1)
        imp = jnp.where(forced[None], FORCE_SCORE, imp)
        imp = jnp.where((jj[None, :] <= cur[:, None])[None], imp, NEG)
        _, sel = lax.top_k(imp, ntop)
        kg = gather_rows(ks_blocks, sel)
        vg = gather_rows(vs_blocks, sel)
        kpos = sel[..., None] * SLC_BLOCK + jnp.arange(SLC_BLOCK)
        ok = (kpos <= tpos[None, :, None, None]).reshape(B, 1, QBLK, ntop * SLC_BLOCK)
        s_s = jnp.einsum('bqhd,bqnkd->bhqnk', qb, kg, preferred_element_type=jnp.float32) * ATTN_SCALE
        p_s = masked_softmax(s_s.reshape(B, H, QBLK, ntop * SLC_BLOCK), ok)
        o_s = jnp.einsum('bhqnk,bqnkd->bqhd', p_s.reshape(B, H, QBLK, ntop, SLC_BLOCK).astype(q.dtype), vg)
        return o_c, o_s

    o_c, o_s = lax.map(one_block, jnp.arange(S // QBLK))
    o_c = o_c.transpose(1, 0, 2, 3, 4).reshape(B, S, H, dh)
    o_s = o_s.transpose(1, 0, 2, 3, 4).reshape(B, S, H, dh)
    o_w = banded_attention(q.reshape(B, S, 1, H, dh), k_win[:, :, None], v_win[:, :, None], NSA_WINDOW)
    o_w = o_w.reshape(B, S, H, dh)
    g = jax.nn.sigmoid(g_raw.reshape(B, S, H, 3).astype(jnp.float32)).astype(q.dtype)
    return g[..., 0:1] * o_c + g[..., 1:2] * o_s + g[..., 2:3] * o_w


def swa_mixer(q_raw, k_raw, v_raw, qk_g, sink, cs):
    B, S, _ = q_raw.shape
    G = BRANCH_HEADS // D_KV_HEADS
    q = apply_rope(rms_norm(q_raw.reshape(B, S, BRANCH_HEADS, HEAD_DIM), qk_g[0]), cs)
    q = q.reshape(B, S, D_KV_HEADS, G, HEAD_DIM)
    k = apply_rope(rms_norm(k_raw.reshape(B, S, D_KV_HEADS, HEAD_DIM), qk_g[1]), cs)
    v = v_raw.reshape(B, S, D_KV_HEADS, HEAD_DIM)
    o = banded_attention(q, k, v, SWA_WINDOW, sink.reshape(D_KV_HEADS, G))
    return o.reshape(B, S, BRANCH_HEADS, HEAD_DIM)


def conv_ffn(h, w_up, conv_w, conv_b, w_down):
    u = h @ w_up
    c = u.shape[-1]
    u = lax.conv_general_dilated(u, conv_w[:, None, :], window_strides=(1,), padding=((CONV_W - 1, 0),),
                                 dimension_numbers=('NWC', 'WIO', 'NWC'), feature_group_count=c) + conv_b
    gate, val = jnp.split(u, 2, axis=-1)
    return (jax.nn.silu(gate) * val) @ w_down


def setup_inputs(seed: int = 0) -> dict:
    key = jax.random.key(seed)
    ks = jax.random.split(key, 21)
    f32 = jnp.float32

    def nrm(k, shape, fan_in):
        return jax.random.normal(k, shape, f32) * (fan_in ** -0.5)

    def gain(k, shape):
        return 1.0 + 0.05 * jax.random.normal(k, shape, f32)

    return {
        'x': jax.random.normal(ks[0], (BATCH, SEQ, D_MODEL), f32),
        'norm1_g': gain(ks[1], (DEPTH, D_MODEL)),
        'w_in': nrm(ks[2], (DEPTH, D_MODEL, N_IN), D_MODEL),
        'a_qk_g': gain(ks[3], (DEPTH, 2, HEAD_DIM)),
        'a_lat_g': gain(ks[4], (DEPTH, A_LATENT)),
        'a_kv_up': nrm(ks[5], (DEPTH, A_LATENT, 2 * HEAD_DIM), A_LATENT),
        'a_idx_k_g': gain(ks[6], (DEPTH, IDX_DIM)),
        'b_qk_g': gain(ks[7], (DEPTH, 2, HEAD_DIM)),
        'c_qk_g': gain(ks[8], (DEPTH, 4, HEAD_DIM)),
        'c_cmp_pe': 0.1 * jax.random.normal(ks[9], (DEPTH, 2, CMP_LEN, HEAD_DIM), f32),
        'c_cmp_w1': nrm(ks[10], (DEPTH, 2, CMP_LEN * HEAD_DIM, CMP_HIDDEN), CMP_LEN * HEAD_DIM),
        'c_cmp_w2': nrm(ks[11], (DEPTH, 2, CMP_HIDDEN, HEAD_DIM), CMP_HIDDEN),
        'd_qk_g': gain(ks[12], (DEPTH, 2, HEAD_DIM)),
        'd_sink': 0.5 * jax.random.normal(ks[13], (DEPTH, BRANCH_HEADS), f32),
        'w_branch': nrm(ks[14], (DEPTH, N_BRANCH, BRANCH_WIDTH, D_MODEL), BRANCH_WIDTH),
        'w_out': nrm(ks[15], (DEPTH, D_MODEL, D_MODEL), D_MODEL),
        'norm2_g': gain(ks[16], (DEPTH, D_MODEL)),
        'w_up': nrm(ks[17], (DEPTH, D_MODEL, 2 * D_FF), D_MODEL),
        'conv_w': nrm(ks[18], (DEPTH, CONV_W, 2 * D_FF), CONV_W),
        'conv_b': 0.01 * jax.random.normal(ks[19], (DEPTH, 2 * D_FF), f32),
        'w_down': nrm(ks[20], (DEPTH, D_FF, D_MODEL), D_FF),
    }


def reference(x, norm1_g, w_in, a_qk_g, a_lat_g, a_kv_up, a_idx_k_g, b_qk_g, c_qk_g, c_cmp_pe,
              c_cmp_w1, c_cmp_w2, d_qk_g, d_sink, w_branch, w_out, norm2_g, w_up, conv_w, conv_b, w_down):
    B, S, _ = x.shape
    cs = rope_tables(S, HEAD_DIM)
    cs_idx = rope_tables(S, IDX_DIM)
    split_at = np.cumsum(IN_WIDTHS)[:-1].tolist()
    for l in range(DEPTH):
        h = rms_norm(x, norm1_g[l])
        proj = h @ w_in[l]
        (a_q, a_lat, a_iq, a_ik, a_iw, b_q, b_k, b_v, c_q, c_kv, c_g,
         d_q, d_k, d_v, g_br) = jnp.split(proj, split_at, axis=-1)
        ya = dsa_mixer(a_q, a_lat, a_iq, a_ik, a_iw, a_qk_g[l], a_lat_g[l], a_kv_up[l], a_idx_k_g[l], cs, cs_idx)
        yb = moba_mixer(b_q, b_k, b_v, b_qk_g[l], cs)
        yc = nsa_mixer(c_q, c_kv, c_g, c_qk_g[l], c_cmp_pe[l], c_cmp_w1[l], c_cmp_w2[l], cs)
        yd = swa_mixer(d_q, d_k, d_v, d_qk_g[l], d_sink[l], cs)
        ys = jnp.stack([ya, yb, yc, yd], axis=2).reshape(B, S, N_BRANCH, BRANCH_WIDTH)
        br = jnp.einsum('bsnc,ncd->bsnd', ys, w_branch[l])
        gates = jax.nn.sigmoid(g_br.reshape(B, S, N_BRANCH, D_MODEL))
        x = x + jnp.einsum('bsnd,bsnd->bsd', gates, br) @ w_out[l]
        x = x + conv_ffn(rms_norm(x, norm2_g[l]), w_up[l], conv_w[l], conv_b[l], w_down[l])
    return x
```

```python
import functools
import math

import numpy as np
import jax
import jax.numpy as jnp
from jax import lax
from jax.experimental import pallas as pl
from jax.experimental.pallas import tpu as pltpu

F32 = jnp.float32
BF16 = jnp.bfloat16
HIGHEST = lax.Precision.HIGHEST

LANES = 128
SUBLANES = 8
VMEM_LIMIT = 48 * 1024 * 1024

D_MODEL = 1024
HEAD_DIM = 64
N_BRANCH = 4
BRANCH_HEADS = 4
BRANCH_WIDTH = BRANCH_HEADS * HEAD_DIM
ROPE_THETA = 10000.0
QBLK = 128
NEG = -1e30
TINY = 1e-30
EPS = 1e-6
ATTN_SCALE = HEAD_DIM ** -0.5
A_LATENT = 128
IDX_HEADS = 8
IDX_DIM = 32
DSA_TOPK = 256
MOBA_BLOCK = 256
MOBA_TOPK = 3
CMP_LEN = 32
CMP_STRIDE = 16
CMP_HIDDEN = 128
SLC_BLOCK = 64
SLC_TOPK = 16
NSA_WINDOW = 512
FORCE_SCORE = 1e9
NSA_KV = 6
SWA_WINDOW = 128
D_KV_HEADS = 2
D_FF = 256 * ((8 * D_MODEL // 3 + 255) // 256)
CONV_W = 3
IW_SCALE = IDX_HEADS ** -0.5 * IDX_DIM ** -0.5
INT_MIN = -2 ** 31

_O_AIW_END = 680
_O_CG_END = 2100
_O_GBR = 2612
C_AQ, C_ALAT, C_AIQ, C_AIKW = 0, 256, 384, 640
C_BQ, C_BK, C_BV = 768, 1024, 1280
C_CQ, C_CKV, C_CG = 1536, 1792, 2176
C_DQ, C_DK, C_DV = 2304, 2560, 2688
W_MIX = 2816
C_GBR = 3072
W_TOT = C_GBR + N_BRANCH * D_MODEL

KC = 512


def _cparams(sem):
    return pltpu.CompilerParams(dimension_semantics=sem, vmem_limit_bytes=VMEM_LIMIT)


def _inproj_kernel(x_ref, g_ref, w_ref, o_ref, h_ref):
    @pl.when(pl.program_id(1) == 0)
    def _():
        x = x_ref[...]
        ms = jnp.mean(x * x, axis=-1, keepdims=True)
        h_ref[...] = (x * lax.rsqrt(ms + EPS) * g_ref[...]).astype(h_ref.dtype)

    o_ref[...] = jnp.dot(h_ref[...], w_ref[...], preferred_element_type=F32)


def _inproj(x2, g, w):
    n = x2.shape[0]
    tm, tn = 512, 1024
    return pl.pallas_call(
        _inproj_kernel,
        grid=(n // tm, W_TOT // tn),
        in_specs=[pl.BlockSpec((tm, D_MODEL), lambda i, j: (i, 0)),
                  pl.BlockSpec((1, D_MODEL), lambda i, j: (0, 0)),
                  pl.BlockSpec((D_MODEL, tn), lambda i, j: (0, j))],
        out_specs=pl.BlockSpec((tm, tn), lambda i, j: (i, j)),
        out_shape=jax.ShapeDtypeStruct((n, W_TOT), F32),
        scratch_shapes=[pltpu.VMEM((tm, D_MODEL), BF16)],
        compiler_params=_cparams(("parallel", "arbitrary")),
        name="inproj",
    )(x2, g, w)


def _rope(x, cosw, sinw, half):
    w = x.shape[1]
    lane = lax.broadcasted_iota(jnp.int32, x.shape, 1)
    first = (lane % (2 * half)) < half
    rot = jnp.where(first, pltpu.roll(x, w - half, 1), pltpu.roll(x, half, 1))
    return x * cosw + rot * sinw


def _gnorm(x, avg, gain):
    ms = jnp.dot(x * x, avg, precision=HIGHEST, preferred_element_type=F32)
    return x * lax.rsqrt(ms + EPS) * gain


def _prep_kernel(p_ref, g_ref, g2_ref, kvup_ref, c64_ref, s64_ref, c32_ref, s32_ref,
                 a64_ref, a32_ref, o_ref, km_ref):
    c64, s64 = c64_ref[...], s64_ref[...]
    c32, s32 = c32_ref[...], s32_ref[...]
    a64, a32 = a64_ref[...], a32_ref[...]
    c64h, s64h, a64h = c64[:, :LANES], s64[:, :LANES], a64[:LANES, :LANES]
    tm = p_ref.shape[0]
    lane1 = lax.broadcasted_iota(jnp.int32, (tm, LANES), 1)
    lo64 = lane1 < HEAD_DIM

    def seg(c, w):
        return p_ref[:, c:c + w], g_ref[:, c:c + w]

    def normrope256(c):
        x, g = seg(c, 256)
        return _rope(_gnorm(x, a64, g), c64, s64, HEAD_DIM // 2)

    o_ref[:, C_AQ:C_AQ + 256] = normrope256(C_AQ)
    x, g = seg(C_ALAT, A_LATENT)
    ms = jnp.mean(x * x, axis=-1, keepdims=True)
    latn = x * lax.rsqrt(ms + EPS) * g
    kv = jnp.dot(latn, kvup_ref[...], precision=HIGHEST, preferred_element_type=F32)
    kr = _rope(_gnorm(kv, a64h, g2_ref[...]), c64h, s64h, HEAD_DIM // 2)
    o_ref[:, C_ALAT:C_ALAT + LANES] = jnp.where(lo64, kr, kv)
    x, _ = seg(C_AIQ, 256)
    o_ref[:, C_AIQ:C_AIQ + 256] = _rope(x, c32, s32, IDX_DIM // 2)
    x, g = seg(C_AIKW, LANES)
    xr = _rope(_gnorm(x, a32, g), c32[:, :LANES], s32[:, :LANES], IDX_DIM // 2)
    o_ref[:, C_AIKW:C_AIKW + LANES] = jnp.where(lane1 < IDX_DIM, xr, x * IW_SCALE)
    o_ref[:, C_BQ:C_BQ + 256] = normrope256(C_BQ)
    kb = normrope256(C_BK)
    o_ref[:, C_BK:C_BK + 256] = kb
    km_ref[0] = jnp.mean(kb, axis=0, keepdims=True)
    o_ref[:, C_BV:C_BV + 256] = p_ref[:, C_BV:C_BV + 256]
    o_ref[:, C_CQ:C_CQ + 256] = normrope256(C_CQ)
    for r in range(3):
        c = C_CKV + r * LANES
        x, g = seg(c, LANES)
        xn = x if r == 0 else _gnorm(x, a64h, g)
        o_ref[:, c:c + LANES] = jnp.where(lo64, _rope(xn, c64h, s64h, HEAD_DIM // 2), x)
    x, _ = seg(C_CG, LANES)
    o_ref[:, C_CG:C_CG + LANES] = 1.0 / (1.0 + jnp.exp(-x))
    o_ref[:, C_DQ:C_DQ + 256] = normrope256(C_DQ)
    x, g = seg(C_DK, LANES)
    o_ref[:, C_DK:C_DK + LANES] = _rope(_gnorm(x, a64h, g), c64h, s64h, HEAD_DIM // 2)
    o_ref[:, C_DV:C_DV + LANES] = p_ref[:, C_DV:C_DV + LANES]


def _prep(proj, gains, g2, kvup, tabs, seq):
    n = proj.shape[0]
    tm = MOBA_BLOCK
    nt = seq // tm
    c64, s64, c32, s32, a64, a32 = tabs
    tab_spec = pl.BlockSpec((tm, 256), lambda i: (i % nt, 0))
    full = lambda a: pl.BlockSpec(a.shape, lambda i: (0,) * a.ndim)
    return pl.pallas_call(
        _prep_kernel,
        grid=(n // tm,),
        in_specs=[pl.BlockSpec((tm, W_MIX), lambda i: (i, 0)), full(gains), full(g2), full(kvup),
                  tab_spec, tab_spec, tab_spec, tab_spec, full(a64), full(a32)],
        out_specs=[pl.BlockSpec((tm, W_MIX), lambda i: (i, 0)),
                   pl.BlockSpec((1, 1, 256), lambda i: (i, 0, 0))],
        out_shape=[jax.ShapeDtypeStruct((n, W_MIX), F32),
                   jax.ShapeDtypeStruct((n // tm, 1, 256), F32)],
        compiler_params=_cparams(("parallel",)),
        name="prep",
    )(proj, gains, g2, kvup, c64, s64, c32, s32, a64, a32)


def _gelu_tanh(x):
    return 0.5 * x * (1.0 + jnp.tanh(math.sqrt(2.0 / math.pi) * (x + 0.044715 * (x * x * x))))


def _cmp_kernel(hk_ref, hv_ref, w1_ref, pe_ref, w2_ref, g_ref, o_ref):
    nrow = hk_ref.shape[1]
    half = CMP_STRIDE * HEAD_DIM
    out = jnp.zeros((nrow, LANES), F32)
    for r, h_ref in enumerate((hk_ref, hv_ref)):
        hb = h_ref[0]
        u = jnp.dot(hb, w1_ref[r, :half, :], precision=HIGHEST, preferred_element_type=F32)
        v = jnp.dot(hb, w1_ref[r, half:, :], precision=HIGHEST, preferred_element_type=F32)
        bias = jnp.dot(pe_ref[r], w1_ref[r], precision=HIGHEST, preferred_element_type=F32)
        pre = u + pltpu.roll(v, nrow - 1, 0) + bias
        out = out + jnp.dot(_gelu_tanh(pre), w2_ref[r], precision=HIGHEST, preferred_element_type=F32)
    lane = lax.broadcasted_iota(jnp.int32, out.shape, 1)
    lo = lane < HEAD_DIM
    ms = jnp.sum(jnp.where(lo, out * out, 0.0), axis=-1, keepdims=True) * (1.0 / HEAD_DIM)
    o_ref[0] = jnp.where(lo, out * lax.rsqrt(ms + EPS) * g_ref[...], out)


def _compress(hk, hv, w1, pe, w2p, gk):
    b, nrow, _ = hk.shape
    full = lambda a: pl.BlockSpec(a.shape, lambda i: (0,) * a.ndim)
    hspec = pl.BlockSpec((1, nrow, CMP_STRIDE * HEAD_DIM), lambda i: (i, 0, 0))
    return pl.pallas_call(
        _cmp_kernel,
        grid=(b,),
        in_specs=[hspec, hspec, full(w1), full(pe), full(w2p), full(gk)],
        out_specs=pl.BlockSpec((1, nrow, LANES), lambda i: (i, 0, 0)),
        out_shape=jax.ShapeDtypeStruct((b, nrow, LANES), F32),
        compiler_params=_cparams(("parallel",)),
        name="nsa_compress",
    )(hk, hv, w1, pe, w2p, gk)


def _tile_rows(x, n):
    return jnp.concatenate([x] * n, axis=0)


def _online_step(carry, s, ok, v):
    m, l, acc = carry
    s = jnp.where(ok, s, NEG)
    m_new = jnp.maximum(m, jnp.max(s, axis=-1, keepdims=True))
    e = jnp.where(ok, jnp.exp(s - m_new), 0.0)
    alpha = jnp.exp(m - m_new)
    l = alpha * l + jnp.sum(e, axis=-1, keepdims=True)
    acc = alpha * acc + jnp.dot(e.astype(BF16), v, preferred_element_type=F32)
    return m_new, l, acc


def _online_init(rows):
    return (jnp.full((rows, 1), NEG, F32), jnp.zeros((rows, 1), F32), jnp.zeros((rows, LANES), F32))


def _top_lowest_index(work, jj, k):
    picks = jnp.zeros(work.shape, F32)
    big = jnp.int32(work.shape[1])
    for _ in range(k):
        mx = jnp.max(work, axis=-1, keepdims=True)
        idx = jnp.min(jnp.where(work == mx, jj, big), axis=-1, keepdims=True)
        pick = jj == idx
        picks = jnp.where(pick, 1.0, picks)
        work = jnp.where(pick, -jnp.inf, work)
    return picks


def _dsa_kernel(q_ref, iq_ref, iw_ref, ikt_ref, kt_ref, kv_ref, o_ref, sc_ref, *, seq, topk):
    i = pl.program_id(1)
    t0 = i * QBLK
    nck = (t0 + QBLK + KC - 1) // KC
    tpos = t0 + lax.broadcasted_iota(jnp.int32, (QBLK, 1), 0)
    lane = lax.broadcasted_iota(jnp.int32, (QBLK, KC), 1)
    iq = iq_ref[0].reshape(IDX_HEADS * QBLK, LANES)
    iw = iw_ref[0]
    wcols = [iw[:, IDX_DIM + h:IDX_DIM + h + 1] for h in range(IDX_HEADS)]

    def score_chunk(c, carry):
        d = jnp.dot(iq, ikt_ref[0, c], preferred_element_type=F32)
        acc = jnp.zeros((QBLK, KC), F32)
        for h in range(IDX_HEADS):
            acc = acc + jnp.maximum(d[h * QBLK:(h + 1) * QBLK], 0.0) * wcols[h]
        acc = jnp.where(c * KC + lane <= tpos, acc, NEG)
        bits = lax.bitcast_convert_type(acc, jnp.int32)
        sc_ref[c] = bits ^ ((bits >> 31) & 0x7FFFFFFF)
        return carry

    lax.fori_loop(0, nck, score_chunk, 0)

    def count(pred):
        def body(c, acc):
            m = pred(sc_ref[c], c * KC + lane).astype(jnp.int32)
            part = m[:, 0:LANES]
            for j in range(1, KC // LANES):
                part = part + m[:, j * LANES:(j + 1) * LANES]
            return acc + part
        acc = lax.fori_loop(0, nck, body, jnp.zeros((QBLK, LANES), jnp.int32))
        return jnp.sum(acc, axis=-1, keepdims=True)

    thr = jnp.zeros((QBLK, 1), jnp.int32)
    thr = jnp.where(count(lambda k, idx: k >= thr) >= topk, thr, jnp.full_like(thr, INT_MIN))

    def bit_step(b, thr):
        cand = thr | lax.shift_left(jnp.int32(1), 30 - b)
        return jnp.where(count(lambda k, idx: k >= cand) >= topk, cand, thr)

    thr = lax.fori_loop(0, 31, bit_step, thr)
    need = topk - count(lambda k, idx: k > thr)
    n_eq = count(lambda k, idx: k == thr)
    nbits = max(1, int(math.ceil(math.log2(seq))))

    def tie_search(_):
        def jstep(b, jb):
            cand = jb | lax.shift_left(jnp.int32(1), nbits - 1 - b)
            c = count(lambda k, idx: jnp.where(k == thr, idx, seq + cand) < cand)
            return jnp.where(c < need, cand, jb)
        return lax.fori_loop(0, nbits, jstep, jnp.zeros((QBLK, 1), jnp.int32))

    jbound = lax.cond(jnp.max(n_eq - need) > 0, tie_search,
                      lambda _: jnp.full((QBLK, 1), seq, jnp.int32), 0)

    q = q_ref[0].reshape(BRANCH_HEADS * QBLK, LANES)

    def attend(c, carry):
        k = sc_ref[c]
        idx = c * KC + lane
        tie_ok = jnp.where(k == thr, idx, seq + jbound) <= jbound
        sel = jnp.where(k > thr, idx, jnp.where(tie_ok, idx, seq + tpos)) <= tpos
        ok = _tile_rows(jnp.where(sel, 1.0, 0.0), BRANCH_HEADS) > 0.5
        s = jnp.dot(q, kt_ref[0, c], preferred_element_type=F32)
        return _online_step(carry, s, ok, kv_ref[0, c])

    m, l, acc = lax.fori_loop(0, nck, attend, _online_init(BRANCH_HEADS * QBLK))
    o_ref[0] = (acc / jnp.maximum(l, TINY)).reshape(BRANCH_HEADS, QBLK, LANES)


def _dsa(q, iq, ikw, ikt, kt, kv, seq):
    b = q.shape[0]
    nq = seq // QBLK
    nc = seq // KC
    topk = min(DSA_TOPK, seq // 4)
    return pl.pallas_call(
        functools.partial(_dsa_kernel, seq=seq, topk=topk),
        grid=(b, nq),
        in_specs=[pl.BlockSpec((1, BRANCH_HEADS, QBLK, LANES), lambda bi, i: (bi, 0, i, 0)),
                  pl.BlockSpec((1, IDX_HEADS, QBLK, LANES), lambda bi, i: (bi, 0, i, 0)),
                  pl.BlockSpec((1, QBLK, LANES), lambda bi, i: (bi, i, 0)),
                  pl.BlockSpec((1, nc, LANES, KC), lambda bi, i: (bi, 0, 0, 0)),
                  pl.BlockSpec((1, nc, LANES, KC), lambda bi, i: (bi, 0, 0, 0)),
                  pl.BlockSpec((1, nc, KC, LANES), lambda bi, i: (bi, 0, 0, 0))],
        out_specs=pl.BlockSpec((1, BRANCH_HEADS, QBLK, LANES), lambda bi, i: (bi, 0, i, 0)),
        out_shape=jax.ShapeDtypeStruct((b, BRANCH_HEADS, seq, LANES), F32),
        scratch_shapes=[pltpu.VMEM((nc, QBLK, KC), jnp.int32)],
        compiler_params=_cparams(("parallel", "arbitrary")),
        name="dsa",
    )(q, iq, ikw, ikt, kt, kv)


def _moba_kernel(q_ref, kmt_ref, kt_ref, kv_ref, o_ref):
    own = pl.program_id(2)
    rows = MOBA_BLOCK
    qf = q_ref[0, 0]
    jj = lax.broadcasted_iota(jnp.int32, (rows, LANES), 1)
    gate = jnp.dot(qf, kmt_ref[0, 0], precision=HIGHEST, preferred_element_type=F32)
    gate = jnp.where(jj < own, gate, NEG)
    picks = _top_lowest_index(gate, jj, MOBA_TOPK)
    picks = jnp.where(jj < own, picks, 0.0).astype(BF16)
    qb = qf.astype(BF16)
    blk_row = lax.broadcasted_iota(jnp.int32, (LANES, MOBA_BLOCK), 0)

    def past(j, carry):
        expand = jnp.where(blk_row == j, 1.0, 0.0).astype(BF16)
        ok = jnp.dot(picks, expand, preferred_element_type=F32) > 0.5
        s = jnp.dot(qb, kt_ref[0, 0, j], preferred_element_type=F32)
        return _online_step(carry, s, ok, kv_ref[0, 0, j])

    carry = lax.fori_loop(0, own, past, _online_init(rows))
    r = lax.broadcasted_iota(jnp.int32, (rows, MOBA_BLOCK), 0)
    cidx = lax.broadcasted_iota(jnp.int32, (rows, MOBA_BLOCK), 1)
    s = jnp.dot(qb, kt_ref[0, 0, own], preferred_element_type=F32)
    m, l, acc = _online_step(carry, s, cidx <= r, kv_ref[0, 0, own])
    o_ref[0, 0] = acc / jnp.maximum(l, TINY)


def _moba(q, kmt, kt, kv, seq):
    b = q.shape[0]
    nb = seq // MOBA_BLOCK
    return pl.pallas_call(
        _moba_kernel,
        grid=(b, BRANCH_HEADS, nb),
        in_specs=[pl.BlockSpec((1, 1, MOBA_BLOCK, LANES), lambda bi, h, i: (bi, h, i, 0)),
                  pl.BlockSpec((1, 1, LANES, LANES), lambda bi, h, i: (bi, h, 0, 0)),
                  pl.BlockSpec((1, 1, nb, LANES, MOBA_BLOCK), lambda bi, h, i: (bi, h, 0, 0, 0)),
                  pl.BlockSpec((1, 1, nb, MOBA_BLOCK, LANES), lambda bi, h, i: (bi, h, 0, 0, 0))],
        out_specs=pl.BlockSpec((1, 1, MOBA_BLOCK, LANES), lambda bi, h, i: (bi, h, i, 0)),
        out_shape=jax.ShapeDtypeStruct((b, BRANCH_HEADS, seq, LANES), F32),
        compiler_params=_cparams(("parallel", "parallel", "arbitrary")),
        name="moba",
    )(q, kmt, kt, kv)


def _nsa_kernel(q_ref, g_ref, cmpt_ref, cmp_ref, c2s_ref, kst_ref, kvs_ref, kwt_ref, kvw_ref,
                o_ref, *, seq):
    i = pl.program_id(1)
    t0 = i * QBLK
    rows = BRANCH_HEADS * QBLK
    qf = q_ref[0].reshape(rows, LANES)
    qb = qf.astype(BF16)
    tpos = t0 + lax.broadcasted_iota(jnp.int32, (QBLK, 1), 0)
    tpos4 = _tile_rows(tpos, BRANCH_HEADS)

    ncp = cmpt_ref.shape[2]
    s = jnp.dot(qf, cmpt_ref[0], precision=HIGHEST, preferred_element_type=F32)
    cend = lax.broadcasted_iota(jnp.int32, (rows, ncp), 1) * CMP_STRIDE + (CMP_LEN - 1)
    ok = cend <= tpos4
    s = jnp.where(ok, s, NEG)
    e = jnp.where(ok, jnp.exp(s - jnp.max(s, axis=-1, keepdims=True)), 0.0)
    p = e / jnp.maximum(jnp.sum(e, axis=-1, keepdims=True), TINY)
    o_c = jnp.dot(p.astype(BF16), cmp_ref[0].astype(BF16), preferred_element_type=F32)
    psum = p[0:QBLK]
    for h in range(1, BRANCH_HEADS):
        psum = psum + p[h * QBLK:(h + 1) * QBLK]
    imp = jnp.dot(psum, c2s_ref[...], precision=HIGHEST, preferred_element_type=F32)

    jj = lax.broadcasted_iota(jnp.int32, (QBLK, LANES), 1)
    cur = tpos // SLC_BLOCK
    forced = (jj == 0) | (jj == cur) | (jj == cur - 1)
    imp = jnp.where(forced, FORCE_SCORE, imp)
    imp = jnp.where(jj <= cur, imp, NEG)
    imp = jnp.where(jj < seq // SLC_BLOCK, imp, -jnp.inf)
    ntop = min(SLC_TOPK, seq // SLC_BLOCK)
    picks = _top_lowest_index(imp, jj, ntop)
    picks = jnp.where(jj <= cur, picks, 0.0).astype(BF16)

    nck = (t0 + QBLK + KC - 1) // KC
    blk_row = lax.broadcasted_iota(jnp.int32, (LANES, KC), 0)
    blk_col = lax.broadcasted_iota(jnp.int32, (LANES, KC), 1) // SLC_BLOCK
    lane = lax.broadcasted_iota(jnp.int32, (rows, KC), 1)

    def slc(c, carry):
        expand = jnp.where(blk_row == blk_col + c * (KC // SLC_BLOCK), 1.0, 0.0).astype(BF16)
        hit = _tile_rows(jnp.dot(picks, expand, preferred_element_type=F32), BRANCH_HEADS)
        ok = jnp.where(hit > 0.5, c * KC + lane, seq + tpos4) <= tpos4
        sc = jnp.dot(qb, kst_ref[0, c], preferred_element_type=F32)
        return _online_step(carry, sc, ok, kvs_ref[0, c])

    m, l, acc = lax.fori_loop(0, nck, slc, _online_init(rows))
    o_s = acc / jnp.maximum(l, TINY)

    nwin = NSA_WINDOW // QBLK + 1
    cb = jnp.maximum(i - (nwin - 1), 0)
    sw = jnp.concatenate([jnp.dot(qb, kwt_ref[0, cb + r], preferred_element_type=F32)
                          for r in range(nwin)], axis=1)
    kpos = cb * QBLK + lax.broadcasted_iota(jnp.int32, (rows, nwin * QBLK), 1)
    diff = tpos4 - kpos
    ok = jnp.where(diff >= 0, diff, NSA_WINDOW) < NSA_WINDOW
    sw = jnp.where(ok, sw, NEG)
    e = jnp.where(ok, jnp.exp(sw - jnp.max(sw, axis=-1, keepdims=True)), 0.0)
    pw = (e / jnp.maximum(jnp.sum(e, axis=-1, keepdims=True), TINY)).astype(BF16)
    o_w = jnp.zeros((rows, LANES), F32)
    for r in range(nwin):
        o_w = o_w + jnp.dot(pw[:, r * QBLK:(r + 1) * QBLK], kvw_ref[0, cb + r],
                            preferred_element_type=F32)

    g = g_ref[0]
    for h in range(BRANCH_HEADS):
        sl = slice(h * QBLK, (h + 1) * QBLK)
        o_ref[0, h] = (g[:, 3 * h:3 * h + 1] * o_c[sl] + g[:, 3 * h + 1:3 * h + 2] * o_s[sl]
                       + g[:, 3 * h + 2:3 * h + 3] * o_w[sl])


def _nsa(q, g, cmpt, cmp, c2s, kst, kvs, kwt, kvw, seq):
    b = q.shape[0]
    nq = seq // QBLK
    nc = seq // KC
    ncp = cmp.shape[1]
    return pl.pallas_call(
        functools.partial(_nsa_kernel, seq=seq),
        grid=(b, nq),
        in_specs=[pl.BlockSpec((1, BRANCH_HEADS, QBLK, LANES), lambda bi, i: (bi, 0, i, 0)),
                  pl.BlockSpec((1, QBLK, LANES), lambda bi, i: (bi, i, 0)),
                  pl.BlockSpec((1, LANES, ncp), lambda bi, i: (bi, 0, 0)),
                  pl.BlockSpec((1, ncp, LANES), lambda bi, i: (bi, 0, 0)),
                  pl.BlockSpec((ncp, LANES), lambda bi, i: (0, 0)),
                  pl.BlockSpec((1, nc, LANES, KC), lambda bi, i: (bi, 0, 0, 0)),
                  pl.BlockSpec((1, nc, KC, LANES), lambda bi, i: (bi, 0, 0, 0)),
                  pl.BlockSpec((1, nq, LANES, QBLK), lambda bi, i: (bi, 0, 0, 0)),
                  pl.BlockSpec((1, nq, QBLK, LANES), lambda bi, i: (bi, 0, 0, 0))],
        out_specs=pl.BlockSpec((1, BRANCH_HEADS, QBLK, LANES), lambda bi, i: (bi, 0, i, 0)),
        out_shape=jax.ShapeDtypeStruct((b, BRANCH_HEADS, seq, LANES), F32),
        compiler_params=_cparams(("parallel", "arbitrary")),
        name="nsa",
    )(q, g, cmpt, cmp, c2s, kst, kvs, kwt, kvw)


def _swa_kernel(sink_ref, q_ref, kt_ref, kv_ref, o_ref):
    i = pl.program_id(1)
    t0 = i * QBLK
    grp = BRANCH_HEADS // D_KV_HEADS
    rows = grp * QBLK
    cb = jnp.maximum(i - 1, 0)
    tpos = _tile_rows(t0 + lax.broadcasted_iota(jnp.int32, (QBLK, 1), 0), grp)
    kpos = cb * QBLK + lax.broadcasted_iota(jnp.int32, (rows, 2 * QBLK), 1)
    diff = tpos - kpos
    ok = jnp.where(diff >= 0, diff, SWA_WINDOW) < SWA_WINDOW
    for c in range(D_KV_HEADS):
        qb = q_ref[0, c * grp:(c + 1) * grp].reshape(rows, LANES).astype(BF16)
        s = jnp.concatenate([jnp.dot(qb, kt_ref[0, c, cb + r], preferred_element_type=F32)
                             for r in range(2)], axis=1)
        s = jnp.where(ok, s, NEG)
        sink = jnp.concatenate([jnp.full((QBLK, 1), sink_ref[c * grp + gi], F32) for gi in range(grp)],
                               axis=0)
        m = jnp.maximum(jnp.max(s, axis=-1, keepdims=True), sink)
        e = jnp.where(ok, jnp.exp(s - m), 0.0)
        den = jnp.sum(e, axis=-1, keepdims=True) + jnp.exp(sink - m)
        p = (e / jnp.maximum(den, TINY)).astype(BF16)
        o = jnp.zeros((rows, LANES), F32)
        for r in range(2):
            o = o + jnp.dot(p[:, r * QBLK:(r + 1) * QBLK], kv_ref[0, c, cb + r],
                            preferred_element_type=F32)
        o_ref[0, c * grp:(c + 1) * grp] = o.reshape(grp, QBLK, LANES)


def _swa(sink, q, kt, kv, seq):
    b = q.shape[0]
    nq = seq // QBLK
    return pl.pallas_call(
        _swa_kernel,
        grid=(b, nq),
        in_specs=[pl.BlockSpec(memory_space=pltpu.SMEM),
                  pl.BlockSpec((1, BRANCH_HEADS, QBLK, LANES), lambda bi, i: (bi, 0, i, 0)),
                  pl.BlockSpec((1, D_KV_HEADS, nq, LANES, QBLK), lambda bi, i: (bi, 0, 0, 0, 0)),
                  pl.BlockSpec((1, D_KV_HEADS, nq, QBLK, LANES), lambda bi, i: (bi, 0, 0, 0, 0))],
        out_specs=pl.BlockSpec((1, BRANCH_HEADS, QBLK, LANES), lambda bi, i: (bi, 0, i, 0)),
        out_shape=jax.ShapeDtypeStruct((b, BRANCH_HEADS, seq, LANES), F32),
        compiler_params=_cparams(("parallel", "arbitrary")),
        name="swa",
    )(sink, q, kt, kv)


def _merge_kernel(x_ref, y_ref, g0_ref, g1_ref, g2_ref, g3_ref, wb_ref, wo_ref, o_ref):
    merged = jnp.zeros(x_ref.shape, F32)
    for n, g_ref in enumerate((g0_ref, g1_ref, g2_ref, g3_ref)):
        br = jnp.dot(y_ref[:, n * BRANCH_WIDTH:(n + 1) * BRANCH_WIDTH], wb_ref[n],
                     preferred_element_type=F32)
        gate = 1.0 / (1.0 + jnp.exp(-g_ref[...]))
        merged = merged + gate * br
    o_ref[...] = x_ref[...] + jnp.dot(merged.astype(BF16), wo_ref[...], preferred_element_type=F32)


def _merge(x2, ys, proj, wb, wo):
    n = x2.shape[0]
    tm = 256
    gb = C_GBR // D_MODEL
    gate_specs = [pl.BlockSpec((tm, D_MODEL), functools.partial(lambda i, k: (i, gb + k), k=k))
                  for k in range(N_BRANCH)]
    return pl.pallas_call(
        _merge_kernel,
        grid=(n // tm,),
        in_specs=[pl.BlockSpec((tm, D_MODEL), lambda i: (i, 0)),
                  pl.BlockSpec((tm, N_BRANCH * BRANCH_WIDTH), lambda i: (i, 0))] + gate_specs +
                 [pl.BlockSpec(wb.shape, lambda i: (0, 0, 0)),
                  pl.BlockSpec(wo.shape, lambda i: (0, 0))],
        out_specs=pl.BlockSpec((tm, D_MODEL), lambda i: (i, 0)),
        out_shape=jax.ShapeDtypeStruct((n, D_MODEL), F32),
        compiler_params=_cparams(("parallel",)),
        name="merge",
    )(x2, ys, proj, proj, proj, proj, wb, wo)


FF_CHUNK = 256
FF_HALO = 2 * SUBLANES


def _ffn_kernel(x_ref, xp_ref, g_ref, wg_ref, wv_ref, cw_ref, cb_ref, wd_ref, o_ref, h_ref, acc_ref,
                *, tiles_per_seq):
    i = pl.program_id(0)
    f = pl.program_id(1)
    tm = x_ref.shape[0]

    @pl.when(f == 0)
    def _():
        def norm(x):
            ms = jnp.mean(x * x, axis=-1, keepdims=True)
            return (x * lax.rsqrt(ms + EPS) * g_ref[...]).astype(BF16)
        h_ref[FF_HALO:, :] = norm(x_ref[...])
        h_ref[:FF_HALO, :] = norm(xp_ref[...])
        acc_ref[...] = jnp.zeros(acc_ref.shape, F32)

    h = h_ref[...]
    row = lax.broadcasted_iota(jnp.int32, (tm, 1), 0)
    first_tile = (i % tiles_per_seq) == 0

    def conv(w_ref, half):
        u = jnp.dot(h, w_ref[...], preferred_element_type=F32)
        cw = cw_ref[half, 0]
        out = u[FF_HALO:] * cw[2:3] + cb_ref[half, 0]
        for d in (1, 2):
            prev = pltpu.roll(u, d, 0)[FF_HALO:]
            prev = jnp.where(jnp.logical_and(first_tile, row < d), 0.0, prev)
            out = out + prev * cw[2 - d:3 - d]
        return out

    gate = conv(wg_ref, 0)
    val = conv(wv_ref, 1)
    act = gate / (1.0 + jnp.exp(-gate)) * val
    acc_ref[...] += jnp.dot(act.astype(BF16), wd_ref[...], preferred_element_type=F32)

    @pl.when(f == pl.num_programs(1) - 1)
    def _():
        o_ref[...] = x_ref[...] + acc_ref[...]


def _ffn(x2, g, wup, cw, cb, wd, seq):
    n = x2.shape[0]
    tm = 512
    nf = D_FF // FF_CHUNK
    hb = tm // FF_HALO
    return pl.pallas_call(
        functools.partial(_ffn_kernel, tiles_per_seq=seq // tm),
        grid=(n // tm, nf),
        in_specs=[pl.BlockSpec((tm, D_MODEL), lambda i, f: (i, 0)),
                  pl.BlockSpec((FF_HALO, D_MODEL), lambda i, f: (jnp.maximum(i * hb - 1, 0), 0)),
                  pl.BlockSpec((1, D_MODEL), lambda i, f: (0, 0)),
                  pl.BlockSpec((D_MODEL, FF_CHUNK), lambda i, f: (0, f)),
                  pl.BlockSpec((D_MODEL, FF_CHUNK), lambda i, f: (0, nf + f)),
                  pl.BlockSpec((2, 1, CONV_W, FF_CHUNK), lambda i, f: (0, f, 0, 0)),
                  pl.BlockSpec((2, 1, 1, FF_CHUNK), lambda i, f: (0, f, 0, 0)),
                  pl.BlockSpec((FF_CHUNK, D_MODEL), lambda i, f: (f, 0))],
        out_specs=pl.BlockSpec((tm, D_MODEL), lambda i, f: (i, 0)),
        out_shape=jax.ShapeDtypeStruct((n, D_MODEL), F32),
        scratch_shapes=[pltpu.VMEM((tm + FF_HALO, D_MODEL), BF16), pltpu.VMEM((tm, D_MODEL), F32)],
        compiler_params=_cparams(("parallel", "arbitrary")),
        name="convffn",
    )(x2, x2, g, wup, wup, cw, cb, wd)


def _heads_pad(a, b, s, nh):
    a = a.reshape(b, s, nh, HEAD_DIM).transpose(0, 2, 1, 3)
    return jnp.pad(a, ((0, 0), (0, 0), (0, 0), (0, LANES - HEAD_DIM)))


def _chunk_t(a, c):
    lead = a.shape[:-2]
    s = a.shape[-2]
    a = a.reshape(lead + (s // c, c, LANES))
    return jnp.swapaxes(a, -1, -2)


def _chunk(a, c):
    lead = a.shape[:-2]
    s = a.shape[-2]
    return a.reshape(lead + (s // c, c, LANES))


def _rope_tables(seq):
    def tab(dim, reps):
        inv = ROPE_THETA ** (-jnp.arange(0, dim, 2, dtype=F32) / dim)
        ang = jnp.arange(seq, dtype=F32)[:, None] * inv[None, :]
        cos, sin = jnp.cos(ang), jnp.sin(ang)
        return (jnp.tile(jnp.concatenate([cos, cos], axis=1), (1, reps)),
                jnp.tile(jnp.concatenate([-sin, sin], axis=1), (1, reps)))
    c64, s64 = tab(HEAD_DIM, 256 // HEAD_DIM)
    c32, s32 = tab(IDX_DIM, 256 // IDX_DIM)
    a64 = np.kron(np.eye(256 // HEAD_DIM), np.full((HEAD_DIM, HEAD_DIM), 1.0 / HEAD_DIM)).astype(np.float32)
    a32 = np.kron(np.eye(LANES // IDX_DIM), np.full((IDX_DIM, IDX_DIM), 1.0 / IDX_DIM)).astype(np.float32)
    return c64, s64, c32, s32, jnp.asarray(a64), jnp.asarray(a32)


def _cmp_to_slc(seq, ncp):
    c_start = np.arange(ncp) * CMP_STRIDE
    s_start = np.arange(LANES) * SLC_BLOCK
    m = ((c_start[:, None] < s_start[None, :] + SLC_BLOCK) & (c_start[:, None] + CMP_LEN > s_start[None, :]))
    real = (np.arange(ncp) < (seq - CMP_LEN) // CMP_STRIDE + 1)[:, None] & (np.arange(LANES) < seq // SLC_BLOCK)[None, :]
    return jnp.asarray((m & real).astype(np.float32))


def _pad_w_in(w):
    z = lambda k: jnp.zeros((w.shape[0], k), w.dtype)
    return jnp.concatenate([w[:, :_O_AIW_END], z(C_BQ - _O_AIW_END), w[:, _O_AIW_END:_O_CG_END],
                            z(C_DQ - C_CG - 3 * BRANCH_HEADS), w[:, _O_CG_END:_O_GBR], z(C_GBR - W_MIX),
                            w[:, _O_GBR:]], axis=1)


def _gain_row(a_qk_g, a_lat_g, a_idx_k_g, b_qk_g, c_qk_g, d_qk_g):
    one = lambda k: jnp.ones((k,), F32)
    zero = lambda k: jnp.zeros((k,), F32)
    t4 = lambda g: jnp.tile(g, BRANCH_HEADS)
    row = jnp.concatenate([
        t4(a_qk_g[0]) * ATTN_SCALE, a_lat_g, one(256), a_idx_k_g, zero(LANES - IDX_DIM),
        t4(b_qk_g[0]) * ATTN_SCALE, t4(b_qk_g[1]), one(256),
        t4(c_qk_g[0]) * ATTN_SCALE, one(LANES), c_qk_g[2], one(HEAD_DIM), c_qk_g[3], one(HEAD_DIM), one(LANES),
        t4(d_qk_g[0]) * ATTN_SCALE, jnp.tile(d_qk_g[1], D_KV_HEADS), one(LANES)])
    return row[None, :]


def _mixers(proj, b, s, l, tabs, c2s, a_qk_g, a_lat_g, a_kv_up, a_idx_k_g, b_qk_g, c_qk_g, c_cmp_pe,
            c_cmp_w1, c_cmp_w2, d_qk_g, d_sink):
    gains = _gain_row(a_qk_g[l], a_lat_g[l], a_idx_k_g[l], b_qk_g[l], c_qk_g[l], d_qk_g[l])
    g2 = jnp.concatenate([a_qk_g[l][1], jnp.ones((HEAD_DIM,), F32)])[None, :]
    pp, kmean = _prep(proj, gains, g2, a_kv_up[l], tabs, s)
    col = lambda c, w: pp[:, c:c + w]
    b3 = lambda a: a.reshape(b, s, a.shape[-1])

    def unheads(o):
        return o[..., HEAD_DIM:].transpose(0, 2, 1, 3).reshape(b * s, BRANCH_WIDTH)

    qa = _heads_pad(col(C_AQ, 256), b, s, BRANCH_HEADS).astype(BF16)
    iq = col(C_AIQ, 256).reshape(b, s, IDX_HEADS, IDX_DIM).transpose(0, 2, 1, 3)
    iq = jnp.pad(iq, ((0, 0), (0, 0), (0, 0), (0, LANES - IDX_DIM))).astype(BF16)
    ikw = b3(col(C_AIKW, LANES))
    ik = jnp.where(jnp.arange(LANES) < IDX_DIM, ikw, 0.0).astype(BF16)
    kva = b3(col(C_ALAT, LANES)).astype(BF16)
    ya = unheads(_dsa(qa, iq, ikw, _chunk_t(ik, KC), _chunk_t(kva, KC), _chunk(kva, KC), s))
    qbf = _heads_pad(col(C_BQ, 256), b, s, BRANCH_HEADS)
    kb = col(C_BK, 256).reshape(b, s, BRANCH_HEADS, HEAD_DIM)
    vb = col(C_BV, 256).reshape(b, s, BRANCH_HEADS, HEAD_DIM)
    kvb = jnp.concatenate([kb, vb], axis=-1).transpose(0, 2, 1, 3).astype(BF16)
    nb = s // MOBA_BLOCK
    km = kmean.reshape(b, nb, BRANCH_HEADS, HEAD_DIM).transpose(0, 2, 3, 1)
    km = jnp.pad(km, ((0, 0), (0, 0), (0, LANES - HEAD_DIM), (0, LANES - nb)))
    yb = unheads(_moba(qbf, km, _chunk_t(kvb, MOBA_BLOCK), _chunk(kvb, MOBA_BLOCK), s))
    qc = _heads_pad(col(C_CQ, 256), b, s, BRANCH_HEADS)
    kvc = b3(col(C_CKV, LANES))
    nrow = s // CMP_STRIDE
    hk = kvc[..., :HEAD_DIM].reshape(b, nrow, CMP_STRIDE * HEAD_DIM)
    hv = kvc[..., HEAD_DIM:].reshape(b, nrow, CMP_STRIDE * HEAD_DIM)
    w2 = c_cmp_w2[l]
    w2p = jnp.stack([jnp.pad(w2[0], ((0, 0), (0, HEAD_DIM))), jnp.pad(w2[1], ((0, 0), (HEAD_DIM, 0)))])
    pe = c_cmp_pe[l].reshape(2, 1, CMP_LEN * HEAD_DIM)
    gk = jnp.concatenate([c_qk_g[l][1], jnp.ones((HEAD_DIM,), F32)])[None, :]
    cmp = _compress(hk, hv, c_cmp_w1[l], pe, w2p, gk)
    ncp = c2s.shape[0]
    cmp = jnp.pad(cmp, ((0, 0), (0, ncp - nrow), (0, 0)))
    cmpt = jnp.where(jnp.arange(LANES)[:, None] < HEAD_DIM, jnp.swapaxes(cmp, 1, 2), 0.0)
    kvs = b3(col(C_CKV + LANES, LANES)).astype(BF16)
    kvw = b3(col(C_CKV + 2 * LANES, LANES)).astype(BF16)
    gc = b3(col(C_CG, LANES))
    yc = unheads(_nsa(qc, gc, cmpt, cmp, c2s, _chunk_t(kvs, KC), _chunk(kvs, KC),
                      _chunk_t(kvw, QBLK), _chunk(kvw, QBLK), s))
    qd = _heads_pad(col(C_DQ, 256), b, s, BRANCH_HEADS)
    kd = col(C_DK, LANES).reshape(b, s, D_KV_HEADS, HEAD_DIM)
    vd = col(C_DV, LANES).reshape(b, s, D_KV_HEADS, HEAD_DIM)
    kvd = jnp.concatenate([kd, vd], axis=-1).transpose(0, 2, 1, 3).astype(BF16)
    yd = unheads(_swa(d_sink[l], qd, _chunk_t(kvd, QBLK), _chunk(kvd, QBLK), s))
    return jnp.concatenate([ya, yb, yc, yd], axis=1).astype(BF16)


def kernel(x, norm1_g, w_in, a_qk_g, a_lat_g, a_kv_up, a_idx_k_g, b_qk_g, c_qk_g, c_cmp_pe, c_cmp_w1,
           c_cmp_w2, d_qk_g, d_sink, w_branch, w_out, norm2_g, w_up, conv_w, conv_b, w_down):
    b, s, _ = x.shape
    depth = w_in.shape[0]
    assert s % KC == 0 and s >= NSA_WINDOW + QBLK and KC >= min(DSA_TOPK, s // 4)
    tabs = _rope_tables(s)
    ncp = max(LANES, s // CMP_STRIDE)
    c2s = _cmp_to_slc(s, ncp)
    x2 = x.reshape(b * s, D_MODEL)
    nf = D_FF // FF_CHUNK
    for l in range(depth):
        proj = _inproj(x2, norm1_g[l][None, :], _pad_w_in(w_in[l]).astype(BF16))
        ys = _mixers(proj, b, s, l, tabs, c2s, a_qk_g, a_lat_g, a_kv_up, a_idx_k_g, b_qk_g, c_qk_g,
                     c_cmp_pe, c_cmp_w1, c_cmp_w2, d_qk_g, d_sink)
        x2 = _merge(x2, ys, proj, w_branch[l].astype(BF16), w_out[l].astype(BF16))
        cw = conv_w[l].reshape(CONV_W, 2, nf, FF_CHUNK).transpose(1, 2, 0, 3)
        cb = conv_b[l].reshape(2, nf, 1, FF_CHUNK)
        x2 = _ffn(x2, norm2_g[l][None, :], w_up[l].astype(BF16), cw, cb, w_down[l].astype(BF16), s)
    return x2.reshape(b, s, D_MODEL)
```

```python
import functools
import math

import numpy as np
import jax
import jax.numpy as jnp
from jax import lax
from jax.experimental import pallas as pl
from jax.experimental.pallas import tpu as pltpu

F32 = jnp.float32
BF16 = jnp.bfloat16
HIGHEST = lax.Precision.HIGHEST

LANES = 128
SUBLANES = 8
VMEM_LIMIT = 48 * 1024 * 1024

D_MODEL = 1024
HEAD_DIM = 64
N_BRANCH = 4
BRANCH_HEADS = 4
BRANCH_WIDTH = BRANCH_HEADS * HEAD_DIM
ROPE_THETA = 10000.0
QBLK = 128
NEG = -1e30
TINY = 1e-30
EPS = 1e-6
ATTN_SCALE = HEAD_DIM ** -0.5
A_LATENT = 128
IDX_HEADS = 8
IDX_DIM = 32
DSA_TOPK = 256
MOBA_BLOCK = 256
MOBA_TOPK = 3
CMP_LEN = 32
CMP_STRIDE = 16
CMP_HIDDEN = 128
SLC_BLOCK = 64
SLC_TOPK = 16
NSA_WINDOW = 512
FORCE_SCORE = 1e9
NSA_KV = 6
SWA_WINDOW = 128
D_KV_HEADS = 2
D_FF = 256 * ((8 * D_MODEL // 3 + 255) // 256)
CONV_W = 3
IW_SCALE = IDX_HEADS ** -0.5 * IDX_DIM ** -0.5
INT_MIN = -2 ** 31

_O_AIW_END = 680
_O_CG_END = 2100
_O_GBR = 2612
C_AQ, C_ALAT, C_AIQ, C_AIKW = 0, 256, 384, 640
C_BQ, C_BK, C_BV = 768, 1024, 1280
C_CQ, C_CKV, C_CG = 1536, 1792, 2176
C_DQ, C_DK, C_DV = 2304, 2560, 2688
W_MIX = 2816
C_GBR = 3072
W_TOT = C_GBR + N_BRANCH * D_MODEL

KC = 512


def _cparams(sem):
    return pltpu.CompilerParams(dimension_semantics=sem, vmem_limit_bytes=VMEM_LIMIT)


def _inproj_kernel(x_ref, g_ref, w_ref, o_ref, h_ref):
    @pl.when(pl.program_id(1) == 0)
    def _():
        x = x_ref[...]
        ms = jnp.mean(x * x, axis=-1, keepdims=True)
        h_ref[...] = (x * lax.rsqrt(ms + EPS) * g_ref[...]).astype(h_ref.dtype)

    o_ref[...] = jnp.dot(h_ref[...], w_ref[...], preferred_element_type=F32)


def _inproj(x2, g, w):
    n = x2.shape[0]
    tm, tn = 512, 1024
    return pl.pallas_call(
        _inproj_kernel,
        grid=(n // tm, W_TOT // tn),
        in_specs=[pl.BlockSpec((tm, D_MODEL), lambda i, j: (i, 0)),
                  pl.BlockSpec((1, D_MODEL), lambda i, j: (0, 0)),
                  pl.BlockSpec((D_MODEL, tn), lambda i, j: (0, j))],
        out_specs=pl.BlockSpec((tm, tn), lambda i, j: (i, j)),
        out_shape=jax.ShapeDtypeStruct((n, W_TOT), F32),
        scratch_shapes=[pltpu.VMEM((tm, D_MODEL), BF16)],
        compiler_params=_cparams(("parallel", "arbitrary")),
        name="inproj",
    )(x2, g, w)


def _rope(x, cosw, sinw, half):
    w = x.shape[1]
    lane = lax.broadcasted_iota(jnp.int32, x.shape, 1)
    first = (lane % (2 * half)) < half
    rot = jnp.where(first, pltpu.roll(x, w - half, 1), pltpu.roll(x, half, 1))
    return x * cosw + rot * sinw


def _gnorm(x, avg, gain):
    ms = jnp.dot(x * x, avg, precision=HIGHEST, preferred_element_type=F32)
    return x * lax.rsqrt(ms + EPS) * gain


def _prep_kernel(p_ref, g_ref, g2_ref, kvup_ref, c64_ref, s64_ref, c32_ref, s32_ref,
                 a64_ref, a32_ref, o_ref, km_ref):
    c64, s64 = c64_ref[...], s64_ref[...]
    c32, s32 = c32_ref[...], s32_ref[...]
    a64, a32 = a64_ref[...], a32_ref[...]
    c64h, s64h, a64h = c64[:, :LANES], s64[:, :LANES], a64[:LANES, :LANES]
    tm = p_ref.shape[0]
    lane1 = lax.broadcasted_iota(jnp.int32, (tm, LANES), 1)
    lo64 = lane1 < HEAD_DIM

    def seg(c, w):
        return p_ref[:, c:c + w], g_ref[:, c:c + w]

    def normrope256(c):
        x, g = seg(c, 256)
        return _rope(_gnorm(x, a64, g), c64, s64, HEAD_DIM // 2)

    o_ref[:, C_AQ:C_AQ + 256] = normrope256(C_AQ)
    x, g = seg(C_ALAT, A_LATENT)
    ms = jnp.mean(x * x, axis=-1, keepdims=True)
    latn = x * lax.rsqrt(ms + EPS) * g
    kv = jnp.dot(latn, kvup_ref[...], precision=HIGHEST, preferred_element_type=F32)
    kr = _rope(_gnorm(kv, a64h, g2_ref[...]), c64h, s64h, HEAD_DIM // 2)
    o_ref[:, C_ALAT:C_ALAT + LANES] = jnp.where(lo64, kr, kv)
    x, _ = seg(C_AIQ, 256)
    o_ref[:, C_AIQ:C_AIQ + 256] = _rope(x, c32, s32, IDX_DIM // 2)
    x, g = seg(C_AIKW, LANES)
    xr = _rope(_gnorm(x, a32, g), c32[:, :LANES], s32[:, :LANES], IDX_DIM // 2)
    o_ref[:, C_AIKW:C_AIKW + LANES] = jnp.where(lane1 < IDX_DIM, xr, x * IW_SCALE)
    o_ref[:, C_BQ:C_BQ + 256] = normrope256(C_BQ)
    kb = normrope256(C_BK)
    o_ref[:, C_BK:C_BK + 256] = kb
    km_ref[0] = jnp.mean(kb, axis=0, keepdims=True)
    o_ref[:, C_BV:C_BV + 256] = p_ref[:, C_BV:C_BV + 256]
    o_ref[:, C_CQ:C_CQ + 256] = normrope256(C_CQ)
    for r in range(3):
        c = C_CKV + r * LANES
        x, g = seg(c, LANES)
        xn = x if r == 0 else _gnorm(x, a64h, g)
        o_ref[:, c:c + LANES] = jnp.where(lo64, _rope(xn, c64h, s64h, HEAD_DIM // 2), x)
    x, _ = seg(C_CG, LANES)
    o_ref[:, C_CG:C_CG + LANES] = 1.0 / (1.0 + jnp.exp(-x))
    o_ref[:, C_DQ:C_DQ + 256] = normrope256(C_DQ)
    x, g = seg(C_DK, LANES)
    o_ref[:, C_DK:C_DK + LANES] = _rope(_gnorm(x, a64h, g), c64h, s64h, HEAD_DIM // 2)
    o_ref[:, C_DV:C_DV + LANES] = p_ref[:, C_DV:C_DV + LANES]


def _prep(proj, gains, g2, kvup, tabs, seq):
    n = proj.shape[0]
    tm = MOBA_BLOCK
    nt = seq // tm
    c64, s64, c32, s32, a64, a32 = tabs
    tab_spec = pl.BlockSpec((tm, 256), lambda i: (i % nt, 0))
    full = lambda a: pl.BlockSpec(a.shape, lambda i: (0,) * a.ndim)
    return pl.pallas_call(
        _prep_kernel,
        grid=(n // tm,),
        in_specs=[pl.BlockSpec((tm, W_MIX), lambda i: (i, 0)), full(gains), full(g2), full(kvup),
                  tab_spec, tab_spec, tab_spec, tab_spec, full(a64), full(a32)],
        out_specs=[pl.BlockSpec((tm, W_MIX), lambda i: (i, 0)),
                   pl.BlockSpec((1, 1, 256), lambda i: (i, 0, 0))],
        out_shape=[jax.ShapeDtypeStruct((n, W_MIX), F32),
                   jax.ShapeDtypeStruct((n // tm, 1, 256), F32)],
        compiler_params=_cparams(("parallel",)),
        name="prep",
    )(proj, gains, g2, kvup, c64, s64, c32, s32, a64, a32)


def _gelu_tanh(x):
    return 0.5 * x * (1.0 + jnp.tanh(math.sqrt(2.0 / math.pi) * (x + 0.044715 * (x * x * x))))


def _cmp_kernel(hk_ref, hv_ref, w1_ref, pe_ref, w2_ref, g_ref, o_ref):
    nrow = hk_ref.shape[1]
    half = CMP_STRIDE * HEAD_DIM
    out = jnp.zeros((nrow, LANES), F32)
    for r, h_ref in enumerate((hk_ref, hv_ref)):
        hb = h_ref[0]
        u = jnp.dot(hb, w1_ref[r, :half, :], precision=HIGHEST, preferred_element_type=F32)
        v = jnp.dot(hb, w1_ref[r, half:, :], precision=HIGHEST, preferred_element_type=F32)
        bias = jnp.dot(pe_ref[r], w1_ref[r], precision=HIGHEST, preferred_element_type=F32)
        pre = u + pltpu.roll(v, nrow - 1, 0) + bias
        out = out + jnp.dot(_gelu_tanh(pre), w2_ref[r], precision=HIGHEST, preferred_element_type=F32)
    lane = lax.broadcasted_iota(jnp.int32, out.shape, 1)
    lo = lane < HEAD_DIM
    ms = jnp.sum(jnp.where(lo, out * out, 0.0), axis=-1, keepdims=True) * (1.0 / HEAD_DIM)
    o_ref[0] = jnp.where(lo, out * lax.rsqrt(ms + EPS) * g_ref[...], out)


def _compress(hk, hv, w1, pe, w2p, gk):
    b, nrow, _ = hk.shape
    full = lambda a: pl.BlockSpec(a.shape, lambda i: (0,) * a.ndim)
    hspec = pl.BlockSpec((1, nrow, CMP_STRIDE * HEAD_DIM), lambda i: (i, 0, 0))
    return pl.pallas_call(
        _cmp_kernel,
        grid=(b,),
        in_specs=[hspec, hspec, full(w1), full(pe), full(w2p), full(gk)],
        out_specs=pl.BlockSpec((1, nrow, LANES), lambda i: (i, 0, 0)),
        out_shape=jax.ShapeDtypeStruct((b, nrow, LANES), F32),
        compiler_params=_cparams(("parallel",)),
        name="nsa_compress",
    )(hk, hv, w1, pe, w2p, gk)


def _tile_rows(x, n):
    return jnp.concatenate([x] * n, axis=0)


def _online_step(carry, s, v):
    m, l, acc = carry
    m_new = jnp.maximum(m, jnp.max(s, axis=-1, keepdims=True))
    e = jnp.exp(s - m_new)
    alpha = jnp.exp(m - m_new)
    l = alpha * l + jnp.sum(e, axis=-1, keepdims=True)
    acc = alpha * acc + jnp.dot(e.astype(BF16), v, preferred_element_type=F32)
    return m_new, l, acc


def _online_init(rows):
    return (jnp.full((rows, 1), NEG, F32), jnp.zeros((rows, 1), F32), jnp.zeros((rows, LANES), F32))


def _top_lowest_index(work, jj, k):
    picks = jnp.zeros(work.shape, F32)
    for _ in range(k):
        idx = jnp.argmax(work, axis=-1, keepdims=True).astype(jnp.int32)
        pick = jj == idx
        picks = jnp.where(pick, 1.0, picks)
        work = jnp.where(pick, -jnp.inf, work)
    return picks


def _split_bf16(a):
    hi = a.astype(BF16)
    return hi, (a - hi.astype(F32)).astype(BF16)


def _dsa_kernel(q_ref, iq_ref, iw_ref, ikt_ref, kt_ref, kv_ref, o_ref, sc_ref, *, seq, topk):
    i = pl.program_id(1)
    t0 = i * QBLK
    nck = (t0 + QBLK + KC - 1) // KC
    tpos = t0 + lax.broadcasted_iota(jnp.int32, (QBLK, 1), 0)
    lane = lax.broadcasted_iota(jnp.int32, (QBLK, KC), 1)
    iq = iq_ref[0].reshape(IDX_HEADS * QBLK, LANES)
    iw = iw_ref[0]
    wcols = [iw[:, IDX_DIM + h:IDX_DIM + h + 1] for h in range(IDX_HEADS)]

    def score_chunk(c, carry):
        d = jnp.dot(iq, ikt_ref[0, c], preferred_element_type=F32)
        acc = jnp.zeros((QBLK, KC), F32)
        for h in range(IDX_HEADS):
            acc = acc + jnp.maximum(d[h * QBLK:(h + 1) * QBLK], 0.0) * wcols[h]
        acc = jnp.where(c * KC + lane <= tpos, acc, NEG)
        bits = lax.bitcast_convert_type(acc, jnp.int32)
        sc_ref[c] = bits ^ ((bits >> 31) & 0x7FFFFFFF)
        return carry

    lax.fori_loop(0, nck, score_chunk, 0)

    def count(pred):
        def body(c, acc):
            m = pred(sc_ref[c], c * KC + lane).astype(jnp.int32)
            part = m[:, 0:LANES]
            for j in range(1, KC // LANES):
                part = part + m[:, j * LANES:(j + 1) * LANES]
            return acc + part
        acc = lax.fori_loop(0, nck, body, jnp.zeros((QBLK, LANES), jnp.int32))
        return jnp.sum(acc, axis=-1, keepdims=True)

    zero = jnp.zeros((QBLK, 1), jnp.int32)
    cnt0 = count(lambda k, idx: k >= zero)
    thr = jnp.where(cnt0 >= topk, zero, jnp.full_like(zero, INT_MIN))
    cnt = jnp.where(cnt0 >= topk, cnt0, nck * KC)

    def bit_cond(st):
        b, thr, cnt = st
        return jnp.logical_and(b < 31, jnp.max(cnt) > topk)

    def bit_step(st):
        b, thr, cnt = st
        cand = thr | lax.shift_left(jnp.int32(1), 30 - b)
        c = count(lambda k, idx: k >= cand)
        take = c >= topk
        return b + 1, jnp.where(take, cand, thr), jnp.where(take, c, cnt)

    _, thr, _ = lax.while_loop(bit_cond, bit_step, (jnp.int32(0), thr, cnt))
    need = topk - count(lambda k, idx: k > thr)
    n_eq = count(lambda k, idx: k == thr)
    nbits = max(1, int(math.ceil(math.log2(seq))))

    def tie_search(_):
        def jstep(b, jb):
            cand = jb | lax.shift_left(jnp.int32(1), nbits - 1 - b)
            c = count(lambda k, idx: jnp.where(k == thr, idx, seq + cand) < cand)
            return jnp.where(c < need, cand, jb)
        return lax.fori_loop(0, nbits, jstep, jnp.zeros((QBLK, 1), jnp.int32))

    jbound = lax.cond(jnp.max(n_eq - need) > 0, tie_search,
                      lambda _: jnp.full((QBLK, 1), seq, jnp.int32), 0)

    q = q_ref[0].reshape(BRANCH_HEADS * QBLK, LANES)

    def attend(c, carry):
        k = sc_ref[c]
        idx = c * KC + lane
        tie_ok = jnp.where(k == thr, idx, seq + jbound) <= jbound
        sel = jnp.where(k > thr, idx, jnp.where(tie_ok, idx, seq + tpos)) <= tpos
        bias = jnp.where(sel, 0.0, -jnp.inf)
        s = jnp.dot(q, kt_ref[0, c], preferred_element_type=F32)
        s = (s.reshape(BRANCH_HEADS, QBLK, KC) + bias[None]).reshape(BRANCH_HEADS * QBLK, KC)
        return _online_step(carry, s, kv_ref[0, c])

    m, l, acc = lax.fori_loop(0, nck, attend, _online_init(BRANCH_HEADS * QBLK))
    o_ref[0] = (acc / jnp.maximum(l, TINY)).reshape(BRANCH_HEADS, QBLK, LANES)


def _dsa(q, iq, ikw, ikt, kt, kv, seq):
    b = q.shape[0]
    nq = seq // QBLK
    nc = seq // KC
    topk = min(DSA_TOPK, seq // 4)
    return pl.pallas_call(
        functools.partial(_dsa_kernel, seq=seq, topk=topk),
        grid=(b, nq),
        in_specs=[pl.BlockSpec((1, BRANCH_HEADS, QBLK, LANES), lambda bi, i: (bi, 0, i, 0)),
                  pl.BlockSpec((1, IDX_HEADS, QBLK, LANES), lambda bi, i: (bi, 0, i, 0)),
                  pl.BlockSpec((1, QBLK, LANES), lambda bi, i: (bi, i, 0)),
                  pl.BlockSpec((1, nc, LANES, KC), lambda bi, i: (bi, 0, 0, 0)),
                  pl.BlockSpec((1, nc, LANES, KC), lambda bi, i: (bi, 0, 0, 0)),
                  pl.BlockSpec((1, nc, KC, LANES), lambda bi, i: (bi, 0, 0, 0))],
        out_specs=pl.BlockSpec((1, BRANCH_HEADS, QBLK, LANES), lambda bi, i: (bi, 0, i, 0)),
        out_shape=jax.ShapeDtypeStruct((b, BRANCH_HEADS, seq, LANES), F32),
        scratch_shapes=[pltpu.VMEM((nc, QBLK, KC), jnp.int32)],
        compiler_params=_cparams(("parallel", "arbitrary")),
        name="dsa",
    )(q, iq, ikw, ikt, kt, kv)


def _moba_kernel(q_ref, kmt_ref, kt_ref, kv_ref, o_ref, *, seq):
    own = pl.program_id(2)
    rows = MOBA_BLOCK
    qf = q_ref[0, 0]
    jj = lax.broadcasted_iota(jnp.int32, (rows, LANES), 1)
    gate = jnp.dot(qf, kmt_ref[0, 0], precision=HIGHEST, preferred_element_type=F32)
    gate = jnp.where(jj < own, gate, NEG)
    picks = _top_lowest_index(gate, jj, MOBA_TOPK)
    picks = jnp.where(jj < own, picks, 0.0).astype(BF16)
    qb = qf.astype(BF16)
    kcm = kt_ref.shape[4]
    per = kcm // MOBA_BLOCK
    blk_row = lax.broadcasted_iota(jnp.int32, (LANES, kcm), 0)
    blk_col = lax.broadcasted_iota(jnp.int32, (LANES, kcm), 1) // MOBA_BLOCK
    col = lax.broadcasted_iota(jnp.int32, (rows, kcm), 1)
    own0 = own * MOBA_BLOCK
    tq = own0 + lax.broadcasted_iota(jnp.int32, (rows, 1), 0)

    def group(g, carry):
        expand = jnp.where(blk_row == blk_col + g * per, 1.0, 0.0).astype(BF16)
        hit = jnp.dot(picks, expand, preferred_element_type=F32) > 0.5
        kpos = g * kcm + col
        ok = jnp.where(hit, kpos, jnp.where(kpos >= own0, kpos, seq + tq)) <= tq
        s = jnp.dot(qb, kt_ref[0, 0, g], preferred_element_type=F32)
        return _online_step(carry, jnp.where(ok, s, -jnp.inf), kv_ref[0, 0, g])

    m, l, acc = lax.fori_loop(0, own // per + 1, group, _online_init(rows))
    o_ref[0, 0] = acc / jnp.maximum(l, TINY)


MOBA_KC = 4 * MOBA_BLOCK


def _moba(q, kmt, kt, kv, seq):
    b = q.shape[0]
    nb = seq // MOBA_BLOCK
    kcm = min(MOBA_KC, seq)
    ng = seq // kcm
    return pl.pallas_call(
        functools.partial(_moba_kernel, seq=seq),
        grid=(b, BRANCH_HEADS, nb),
        in_specs=[pl.BlockSpec((1, 1, MOBA_BLOCK, LANES), lambda bi, h, i: (bi, h, i, 0)),
                  pl.BlockSpec((1, 1, LANES, LANES), lambda bi, h, i: (bi, h, 0, 0)),
                  pl.BlockSpec((1, 1, ng, LANES, kcm), lambda bi, h, i: (bi, h, 0, 0, 0)),
                  pl.BlockSpec((1, 1, ng, kcm, LANES), lambda bi, h, i: (bi, h, 0, 0, 0))],
        out_specs=pl.BlockSpec((1, 1, MOBA_BLOCK, LANES), lambda bi, h, i: (bi, h, i, 0)),
        out_shape=jax.ShapeDtypeStruct((b, BRANCH_HEADS, seq, LANES), F32),
        compiler_params=_cparams(("parallel", "parallel", "arbitrary")),
        name="moba",
    )(q, kmt, kt, kv)


def _nsa_kernel(q_ref, g_ref, cmpt_ref, cmp_ref, c2s_ref, kst_ref, kvs_ref, kwt_ref, kvw_ref,
                o_ref, *, seq):
    i = pl.program_id(1)
    t0 = i * QBLK
    rows = BRANCH_HEADS * QBLK
    qf = q_ref[0].reshape(rows, LANES)
    qb = qf.astype(BF16)
    tpos = t0 + lax.broadcasted_iota(jnp.int32, (QBLK, 1), 0)
    tpos4 = _tile_rows(tpos, BRANCH_HEADS)

    ncp = cmpt_ref.shape[2]
    q_hi, q_lo = _split_bf16(qf)
    c_hi, c_lo = _split_bf16(cmpt_ref[0])
    s = (jnp.dot(q_hi, c_hi, preferred_element_type=F32) + jnp.dot(q_hi, c_lo, preferred_element_type=F32)
         + jnp.dot(q_lo, c_hi, preferred_element_type=F32))
    cend = lax.broadcasted_iota(jnp.int32, (rows, ncp), 1) * CMP_STRIDE + (CMP_LEN - 1)
    ok = cend <= tpos4
    s = jnp.where(ok, s, NEG)
    e = jnp.where(ok, jnp.exp(s - jnp.max(s, axis=-1, keepdims=True)), 0.0)
    p = e / jnp.maximum(jnp.sum(e, axis=-1, keepdims=True), TINY)
    o_c = jnp.dot(p.astype(BF16), cmp_ref[0].astype(BF16), preferred_element_type=F32)
    psum = p[0:QBLK]
    for h in range(1, BRANCH_HEADS):
        psum = psum + p[h * QBLK:(h + 1) * QBLK]
    p_hi, p_lo = _split_bf16(psum)
    c2s = c2s_ref[...].astype(BF16)
    imp = (jnp.dot(p_hi, c2s, preferred_element_type=F32)
           + jnp.dot(p_lo, c2s, preferred_element_type=F32))

    jj = lax.broadcasted_iota(jnp.int32, (QBLK, LANES), 1)
    cur = tpos // SLC_BLOCK
    forced = (jj == 0) | (jj == cur) | (jj == cur - 1)
    imp = jnp.where(forced, FORCE_SCORE, imp)
    imp = jnp.where(jj <= cur, imp, NEG)
    imp = jnp.where(jj < seq // SLC_BLOCK, imp, -jnp.inf)
    ntop = min(SLC_TOPK, seq // SLC_BLOCK)
    picks = _top_lowest_index(imp, jj, ntop)
    picks = jnp.where(jj <= cur, picks, 0.0).astype(BF16)

    nck = (t0 + QBLK + KC - 1) // KC
    blk_row = lax.broadcasted_iota(jnp.int32, (LANES, KC), 0)
    blk_col = lax.broadcasted_iota(jnp.int32, (LANES, KC), 1) // SLC_BLOCK
    lane = lax.broadcasted_iota(jnp.int32, (QBLK, KC), 1)

    def slc(c, carry):
        expand = jnp.where(blk_row == blk_col + c * (KC // SLC_BLOCK), 1.0, 0.0).astype(BF16)
        hit = jnp.dot(picks, expand, preferred_element_type=F32) > 0.5
        ok = jnp.where(hit, c * KC + lane, seq + tpos) <= tpos
        bias = jnp.where(ok, 0.0, -jnp.inf)
        sc = jnp.dot(qb, kst_ref[0, c], preferred_element_type=F32)
        sc = (sc.reshape(BRANCH_HEADS, QBLK, KC) + bias[None]).reshape(rows, KC)
        return _online_step(carry, sc, kvs_ref[0, c])

    m, l, acc = lax.fori_loop(0, nck, slc, _online_init(rows))
    o_s = acc / jnp.maximum(l, TINY)

    nwin = NSA_WINDOW // QBLK + 1
    cb = jnp.maximum(i - (nwin - 1), 0)
    sw = jnp.concatenate([jnp.dot(qb, kwt_ref[0, cb + r], preferred_element_type=F32)
                          for r in range(nwin)], axis=1)
    kpos = cb * QBLK + lax.broadcasted_iota(jnp.int32, (rows, nwin * QBLK), 1)
    diff = tpos4 - kpos
    ok = jnp.where(diff >= 0, diff, NSA_WINDOW) < NSA_WINDOW
    sw = jnp.where(ok, sw, NEG)
    e = jnp.where(ok, jnp.exp(sw - jnp.max(sw, axis=-1, keepdims=True)), 0.0)
    pw = (e / jnp.maximum(jnp.sum(e, axis=-1, keepdims=True), TINY)).astype(BF16)
    o_w = jnp.zeros((rows, LANES), F32)
    for r in range(nwin):
        o_w = o_w + jnp.dot(pw[:, r * QBLK:(r + 1) * QBLK], kvw_ref[0, cb + r],
                            preferred_element_type=F32)

    g = g_ref[0]
    for h in range(BRANCH_HEADS):
        sl = slice(h * QBLK, (h + 1) * QBLK)
        o_ref[0, h] = (g[:, 3 * h:3 * h + 1] * o_c[sl] + g[:, 3 * h + 1:3 * h + 2] * o_s[sl]
                       + g[:, 3 * h + 2:3 * h + 3] * o_w[sl])


def _nsa(q, g, cmpt, cmp, c2s, kst, kvs, kwt, kvw, seq):
    b = q.shape[0]
    nq = seq // QBLK
    nc = seq // KC
    ncp = cmp.shape[1]
    return pl.pallas_call(
        functools.partial(_nsa_kernel, seq=seq),
        grid=(b, nq),
        in_specs=[pl.BlockSpec((1, BRANCH_HEADS, QBLK, LANES), lambda bi, i: (bi, 0, i, 0)),
                  pl.BlockSpec((1, QBLK, LANES), lambda bi, i: (bi, i, 0)),
                  pl.BlockSpec((1, LANES, ncp), lambda bi, i: (bi, 0, 0)),
                  pl.BlockSpec((1, ncp, LANES), lambda bi, i: (bi, 0, 0)),
                  pl.BlockSpec((ncp, LANES), lambda bi, i: (0, 0)),
                  pl.BlockSpec((1, nc, LANES, KC), lambda bi, i: (bi, 0, 0, 0)),
                  pl.BlockSpec((1, nc, KC, LANES), lambda bi, i: (bi, 0, 0, 0)),
                  pl.BlockSpec((1, nq, LANES, QBLK), lambda bi, i: (bi, 0, 0, 0)),
                  pl.BlockSpec((1, nq, QBLK, LANES), lambda bi, i: (bi, 0, 0, 0))],
        out_specs=pl.BlockSpec((1, BRANCH_HEADS, QBLK, LANES), lambda bi, i: (bi, 0, i, 0)),
        out_shape=jax.ShapeDtypeStruct((b, BRANCH_HEADS, seq, LANES), F32),
        compiler_params=_cparams(("parallel", "arbitrary")),
        name="nsa",
    )(q, g, cmpt, cmp, c2s, kst, kvs, kwt, kvw)


def _swa_kernel(sink_ref, q_ref, kt_ref, kv_ref, o_ref):
    i = pl.program_id(1)
    t0 = i * QBLK
    grp = BRANCH_HEADS // D_KV_HEADS
    rows = grp * QBLK
    cb = jnp.maximum(i - 1, 0)
    tpos = _tile_rows(t0 + lax.broadcasted_iota(jnp.int32, (QBLK, 1), 0), grp)
    kpos = cb * QBLK + lax.broadcasted_iota(jnp.int32, (rows, 2 * QBLK), 1)
    diff = tpos - kpos
    ok = jnp.where(diff >= 0, diff, SWA_WINDOW) < SWA_WINDOW
    for c in range(D_KV_HEADS):
        qb = q_ref[0, c * grp:(c + 1) * grp].reshape(rows, LANES).astype(BF16)
        s = jnp.concatenate([jnp.dot(qb, kt_ref[0, c, cb + r], preferred_element_type=F32)
                             for r in range(2)], axis=1)
        s = jnp.where(ok, s, NEG)
        sink = jnp.concatenate([jnp.full((QBLK, 1), sink_ref[c * grp + gi], F32) for gi in range(grp)],
                               axis=0)
        m = jnp.maximum(jnp.max(s, axis=-1, keepdims=True), sink)
        e = jnp.where(ok, jnp.exp(s - m), 0.0)
        den = jnp.sum(e, axis=-1, keepdims=True) + jnp.exp(sink - m)
        p = (e / jnp.maximum(den, TINY)).astype(BF16)
        o = jnp.zeros((rows, LANES), F32)
        for r in range(2):
            o = o + jnp.dot(p[:, r * QBLK:(r + 1) * QBLK], kv_ref[0, c, cb + r],
                            preferred_element_type=F32)
        o_ref[0, c * grp:(c + 1) * grp] = o.reshape(grp, QBLK, LANES)


def _swa(sink, q, kt, kv, seq):
    b = q.shape[0]
    nq = seq // QBLK
    return pl.pallas_call(
        _swa_kernel,
        grid=(b, nq),
        in_specs=[pl.BlockSpec(memory_space=pltpu.SMEM),
                  pl.BlockSpec((1, BRANCH_HEADS, QBLK, LANES), lambda bi, i: (bi, 0, i, 0)),
                  pl.BlockSpec((1, D_KV_HEADS, nq, LANES, QBLK), lambda bi, i: (bi, 0, 0, 0, 0)),
                  pl.BlockSpec((1, D_KV_HEADS, nq, QBLK, LANES), lambda bi, i: (bi, 0, 0, 0, 0))],
        out_specs=pl.BlockSpec((1, BRANCH_HEADS, QBLK, LANES), lambda bi, i: (bi, 0, i, 0)),
        out_shape=jax.ShapeDtypeStruct((b, BRANCH_HEADS, seq, LANES), F32),
        compiler_params=_cparams(("parallel", "arbitrary")),
        name="swa",
    )(sink, q, kt, kv)


def _merge_kernel(x_ref, y_ref, g0_ref, g1_ref, g2_ref, g3_ref, wb_ref, wo_ref, o_ref):
    merged = jnp.zeros(x_ref.shape, F32)
    for n, g_ref in enumerate((g0_ref, g1_ref, g2_ref, g3_ref)):
        br = jnp.dot(y_ref[:, n * BRANCH_WIDTH:(n + 1) * BRANCH_WIDTH], wb_ref[n],
                     preferred_element_type=F32)
        gate = 1.0 / (1.0 + jnp.exp(-g_ref[...]))
        merged = merged + gate * br
    o_ref[...] = x_ref[...] + jnp.dot(merged.astype(BF16), wo_ref[...], preferred_element_type=F32)


def _merge(x2, ys, proj, wb, wo):
    n = x2.shape[0]
    tm = 256
    gb = C_GBR // D_MODEL
    gate_specs = [pl.BlockSpec((tm, D_MODEL), functools.partial(lambda i, k: (i, gb + k), k=k))
                  for k in range(N_BRANCH)]
    return pl.pallas_call(
        _merge_kernel,
        grid=(n // tm,),
        in_specs=[pl.BlockSpec((tm, D_MODEL), lambda i: (i, 0)),
                  pl.BlockSpec((tm, N_BRANCH * BRANCH_WIDTH), lambda i: (i, 0))] + gate_specs +
                 [pl.BlockSpec(wb.shape, lambda i: (0, 0, 0)),
                  pl.BlockSpec(wo.shape, lambda i: (0, 0))],
        out_specs=pl.BlockSpec((tm, D_MODEL), lambda i: (i, 0)),
        out_shape=jax.ShapeDtypeStruct((n, D_MODEL), F32),
        compiler_params=_cparams(("parallel",)),
        name="merge",
    )(x2, ys, proj, proj, proj, proj, wb, wo)


FF_CHUNK = 256
FF_HALO = 2 * SUBLANES


def _ffn_kernel(x_ref, xp_ref, g_ref, wg_ref, wv_ref, cw_ref, cb_ref, wd_ref, o_ref, h_ref, acc_ref,
                *, tiles_per_seq):
    i = pl.program_id(0)
    f = pl.program_id(1)
    tm = x_ref.shape[0]

    @pl.when(f == 0)
    def _():
        def norm(x):
            ms = jnp.mean(x * x, axis=-1, keepdims=True)
            return (x * lax.rsqrt(ms + EPS) * g_ref[...]).astype(BF16)
        h_ref[FF_HALO:, :] = norm(x_ref[...])
        h_ref[:FF_HALO, :] = norm(xp_ref[...])
        acc_ref[...] = jnp.zeros(acc_ref.shape, F32)

    h = h_ref[...]
    row = lax.broadcasted_iota(jnp.int32, (tm, 1), 0)
    first_tile = (i % tiles_per_seq) == 0

    def conv(w_ref, half):
        u = jnp.dot(h, w_ref[...], preferred_element_type=F32)
        cw = cw_ref[half, 0]
        out = u[FF_HALO:] * cw[2:3] + cb_ref[half, 0]
        for d in (1, 2):
            prev = pltpu.roll(u, d, 0)[FF_HALO:]
            prev = jnp.where(jnp.logical_and(first_tile, row < d), 0.0, prev)
            out = out + prev * cw[2 - d:3 - d]
        return out

    gate = conv(wg_ref, 0)
    val = conv(wv_ref, 1)
    act = gate / (1.0 + jnp.exp(-gate)) * val
    acc_ref[...] += jnp.dot(act.astype(BF16), wd_ref[...], preferred_element_type=F32)

    @pl.when(f == pl.num_programs(1) - 1)
    def _():
        o_ref[...] = x_ref[...] + acc_ref[...]


def _ffn(x2, g, wup, cw, cb, wd, seq):
    n = x2.shape[0]
    tm = 512
    nf = D_FF // FF_CHUNK
    hb = tm // FF_HALO
    return pl.pallas_call(
        functools.partial(_ffn_kernel, tiles_per_seq=seq // tm),
        grid=(n // tm, nf),
        in_specs=[pl.BlockSpec((tm, D_MODEL), lambda i, f: (i, 0)),
                  pl.BlockSpec((FF_HALO, D_MODEL), lambda i, f: (jnp.maximum(i * hb - 1, 0), 0)),
                  pl.BlockSpec((1, D_MODEL), lambda i, f: (0, 0)),
                  pl.BlockSpec((D_MODEL, FF_CHUNK), lambda i, f: (0, f)),
                  pl.BlockSpec((D_MODEL, FF_CHUNK), lambda i, f: (0, nf + f)),
                  pl.BlockSpec((2, 1, CONV_W, FF_CHUNK), lambda i, f: (0, f, 0, 0)),
                  pl.BlockSpec((2, 1, 1, FF_CHUNK), lambda i, f: (0, f, 0, 0)),
                  pl.BlockSpec((FF_CHUNK, D_MODEL), lambda i, f: (f, 0))],
        out_specs=pl.BlockSpec((tm, D_MODEL), lambda i, f: (i, 0)),
        out_shape=jax.ShapeDtypeStruct((n, D_MODEL), F32),
        scratch_shapes=[pltpu.VMEM((tm + FF_HALO, D_MODEL), BF16), pltpu.VMEM((tm, D_MODEL), F32)],
        compiler_params=_cparams(("parallel", "arbitrary")),
        name="convffn",
    )(x2, x2, g, wup, wup, cw, cb, wd)


def _heads_pad(a, b, s, nh):
    a = a.reshape(b, s, nh, HEAD_DIM).transpose(0, 2, 1, 3)
    return jnp.pad(a, ((0, 0), (0, 0), (0, 0), (0, LANES - HEAD_DIM)))


def _chunk_t(a, c):
    lead = a.shape[:-2]
    s = a.shape[-2]
    a = a.reshape(lead + (s // c, c, LANES))
    return jnp.swapaxes(a, -1, -2)


def _chunk(a, c):
    lead = a.shape[:-2]
    s = a.shape[-2]
    return a.reshape(lead + (s // c, c, LANES))


def _rope_tables(seq):
    def tab(dim, reps):
        inv = ROPE_THETA ** (-jnp.arange(0, dim, 2, dtype=F32) / dim)
        ang = jnp.arange(seq, dtype=F32)[:, None] * inv[None, :]
        cos, sin = jnp.cos(ang), jnp.sin(ang)
        return (jnp.tile(jnp.concatenate([cos, cos], axis=1), (1, reps)),
                jnp.tile(jnp.concatenate([-sin, sin], axis=1), (1, reps)))
    c64, s64 = tab(HEAD_DIM, 256 // HEAD_DIM)
    c32, s32 = tab(IDX_DIM, 256 // IDX_DIM)
    a64 = np.kron(np.eye(256 // HEAD_DIM), np.full((HEAD_DIM, HEAD_DIM), 1.0 / HEAD_DIM)).astype(np.float32)
    a32 = np.kron(np.eye(LANES // IDX_DIM), np.full((IDX_DIM, IDX_DIM), 1.0 / IDX_DIM)).astype(np.float32)
    return c64, s64, c32, s32, jnp.asarray(a64), jnp.asarray(a32)


def _cmp_to_slc(seq, ncp):
    c_start = np.arange(ncp) * CMP_STRIDE
    s_start = np.arange(LANES) * SLC_BLOCK
    m = ((c_start[:, None] < s_start[None, :] + SLC_BLOCK) & (c_start[:, None] + CMP_LEN > s_start[None, :]))
    real = (np.arange(ncp) < (seq - CMP_LEN) // CMP_STRIDE + 1)[:, None] & (np.arange(LANES) < seq // SLC_BLOCK)[None, :]
    return jnp.asarray((m & real).astype(np.float32))


def _pad_w_in(w):
    z = lambda k: jnp.zeros((w.shape[0], k), w.dtype)
    return jnp.concatenate([w[:, :_O_AIW_END], z(C_BQ - _O_AIW_END), w[:, _O_AIW_END:_O_CG_END],
                            z(C_DQ - C_CG - 3 * BRANCH_HEADS), w[:, _O_CG_END:_O_GBR], z(C_GBR - W_MIX),
                            w[:, _O_GBR:]], axis=1)


def _gain_row(a_qk_g, a_lat_g, a_idx_k_g, b_qk_g, c_qk_g, d_qk_g):
    one = lambda k: jnp.ones((k,), F32)
    zero = lambda k: jnp.zeros((k,), F32)
    t4 = lambda g: jnp.tile(g, BRANCH_HEADS)
    row = jnp.concatenate([
        t4(a_qk_g[0]) * ATTN_SCALE, a_lat_g, one(256), a_idx_k_g, zero(LANES - IDX_DIM),
        t4(b_qk_g[0]) * ATTN_SCALE, t4(b_qk_g[1]), one(256),
        t4(c_qk_g[0]) * ATTN_SCALE, one(LANES), c_qk_g[2], one(HEAD_DIM), c_qk_g[3], one(HEAD_DIM), one(LANES),
        t4(d_qk_g[0]) * ATTN_SCALE, jnp.tile(d_qk_g[1], D_KV_HEADS), one(LANES)])
    return row[None, :]


def _mixers(proj, b, s, l, tabs, c2s, a_qk_g, a_lat_g, a_kv_up, a_idx_k_g, b_qk_g, c_qk_g, c_cmp_pe,
            c_cmp_w1, c_cmp_w2, d_qk_g, d_sink):
    gains = _gain_row(a_qk_g[l], a_lat_g[l], a_idx_k_g[l], b_qk_g[l], c_qk_g[l], d_qk_g[l])
    g2 = jnp.concatenate([a_qk_g[l][1], jnp.ones((HEAD_DIM,), F32)])[None, :]
    pp, kmean = _prep(proj, gains, g2, a_kv_up[l], tabs, s)
    col = lambda c, w: pp[:, c:c + w]
    b3 = lambda a: a.reshape(b, s, a.shape[-1])

    def unheads(o):
        return o[..., HEAD_DIM:].transpose(0, 2, 1, 3).reshape(b * s, BRANCH_WIDTH)

    qa = _heads_pad(col(C_AQ, 256), b, s, BRANCH_HEADS).astype(BF16)
    iq = col(C_AIQ, 256).reshape(b, s, IDX_HEADS, IDX_DIM).transpose(0, 2, 1, 3)
    iq = jnp.pad(iq, ((0, 0), (0, 0), (0, 0), (0, LANES - IDX_DIM))).astype(BF16)
    ikw = b3(col(C_AIKW, LANES))
    ik = jnp.where(jnp.arange(LANES) < IDX_DIM, ikw, 0.0).astype(BF16)
    kva = b3(col(C_ALAT, LANES)).astype(BF16)
    ya = unheads(_dsa(qa, iq, ikw, _chunk_t(ik, KC), _chunk_t(kva, KC), _chunk(kva, KC), s))
    qbf = _heads_pad(col(C_BQ, 256), b, s, BRANCH_HEADS)
    kb = col(C_BK, 256).reshape(b, s, BRANCH_HEADS, HEAD_DIM)
    vb = col(C_BV, 256).reshape(b, s, BRANCH_HEADS, HEAD_DIM)
    kvb = jnp.concatenate([kb, vb], axis=-1).transpose(0, 2, 1, 3).astype(BF16)
    nb = s // MOBA_BLOCK
    km = kmean.reshape(b, nb, BRANCH_HEADS, HEAD_DIM).transpose(0, 2, 3, 1)
    km = jnp.pad(km, ((0, 0), (0, 0), (0, LANES - HEAD_DIM), (0, LANES - nb)))
    kcm = min(MOBA_KC, s)
    yb = unheads(_moba(qbf, km, _chunk_t(kvb, kcm), _chunk(kvb, kcm), s))
    qc = _heads_pad(col(C_CQ, 256), b, s, BRANCH_HEADS)
    kvc = b3(col(C_CKV, LANES))
    nrow = s // CMP_STRIDE
    hk = kvc[..., :HEAD_DIM].reshape(b, nrow, CMP_STRIDE * HEAD_DIM)
    hv = kvc[..., HEAD_DIM:].reshape(b, nrow, CMP_STRIDE * HEAD_DIM)
    w2 = c_cmp_w2[l]
    w2p = jnp.stack([jnp.pad(w2[0], ((0, 0), (0, HEAD_DIM))), jnp.pad(w2[1], ((0, 0), (HEAD_DIM, 0)))])
    pe = c_cmp_pe[l].reshape(2, 1, CMP_LEN * HEAD_DIM)
    gk = jnp.concatenate([c_qk_g[l][1], jnp.ones((HEAD_DIM,), F32)])[None, :]
    cmp = _compress(hk, hv, c_cmp_w1[l], pe, w2p, gk)
    ncp = c2s.shape[0]
    cmp = jnp.pad(cmp, ((0, 0), (0, ncp - nrow), (0, 0)))
    cmpt = jnp.where(jnp.arange(LANES)[:, None] < HEAD_DIM, jnp.swapaxes(cmp, 1, 2), 0.0)
    kvs = b3(col(C_CKV + LANES, LANES)).astype(BF16)
    kvw = b3(col(C_CKV + 2 * LANES, LANES)).astype(BF16)
    gc = b3(col(C_CG, LANES))
    yc = unheads(_nsa(qc, gc, cmpt, cmp, c2s, _chunk_t(kvs, KC), _chunk(kvs, KC),
                      _chunk_t(kvw, QBLK), _chunk(kvw, QBLK), s))
    qd = _heads_pad(col(C_DQ, 256), b, s, BRANCH_HEADS)
    kd = col(C_DK, LANES).reshape(b, s, D_KV_HEADS, HEAD_DIM)
    vd = col(C_DV, LANES).reshape(b, s, D_KV_HEADS, HEAD_DIM)
    kvd = jnp.concatenate([kd, vd], axis=-1).transpose(0, 2, 1, 3).astype(BF16)
    yd = unheads(_swa(d_sink[l], qd, _chunk_t(kvd, QBLK), _chunk(kvd, QBLK), s))
    return jnp.concatenate([ya, yb, yc, yd], axis=1).astype(BF16)


def kernel(x, norm1_g, w_in, a_qk_g, a_lat_g, a_kv_up, a_idx_k_g, b_qk_g, c_qk_g, c_cmp_pe, c_cmp_w1,
           c_cmp_w2, d_qk_g, d_sink, w_branch, w_out, norm2_g, w_up, conv_w, conv_b, w_down):
    b, s, _ = x.shape
    depth = w_in.shape[0]
    assert s % KC == 0 and s >= NSA_WINDOW + QBLK and KC >= min(DSA_TOPK, s // 4)
    tabs = _rope_tables(s)
    ncp = max(LANES, s // CMP_STRIDE)
    c2s = _cmp_to_slc(s, ncp)
    x2 = x.reshape(b * s, D_MODEL)
    nf = D_FF // FF_CHUNK
    for l in range(depth):
        proj = _inproj(x2, norm1_g[l][None, :], _pad_w_in(w_in[l]).astype(BF16))
        ys = _mixers(proj, b, s, l, tabs, c2s, a_qk_g, a_lat_g, a_kv_up, a_idx_k_g, b_qk_g, c_qk_g,
                     c_cmp_pe, c_cmp_w1, c_cmp_w2, d_qk_g, d_sink)
        x2 = _merge(x2, ys, proj, w_branch[l].astype(BF16), w_out[l].astype(BF16))
        cw = conv_w[l].reshape(CONV_W, 2, nf, FF_CHUNK).transpose(1, 2, 0, 3)
        cb = conv_b[l].reshape(2, nf, 1, FF_CHUNK)
        x2 = _ffn(x2, norm2_g[l][None, :], w_up[l].astype(BF16), cw, cb, w_down[l].astype(BF16), s)
    return x2.reshape(b, s, D_MODEL)
```

```python
import functools
import math

import numpy as np
import jax
import jax.numpy as jnp
from jax import lax
from jax.experimental import pallas as pl
from jax.experimental.pallas import tpu as pltpu

F32 = jnp.float32
BF16 = jnp.bfloat16
HIGHEST = lax.Precision.HIGHEST

LANES = 128
SUBLANES = 8
VMEM_LIMIT = 48 * 1024 * 1024

D_MODEL = 1024
HEAD_DIM = 64
N_BRANCH = 4
BRANCH_HEADS = 4
BRANCH_WIDTH = BRANCH_HEADS * HEAD_DIM
ROPE_THETA = 10000.0
QBLK = 128
NEG = -1e30
TINY = 1e-30
EPS = 1e-6
ATTN_SCALE = HEAD_DIM ** -0.5
A_LATENT = 128
IDX_HEADS = 8
IDX_DIM = 32
DSA_TOPK = 256
MOBA_BLOCK = 256
MOBA_TOPK = 3
CMP_LEN = 32
CMP_STRIDE = 16
CMP_HIDDEN = 128
SLC_BLOCK = 64
SLC_TOPK = 16
NSA_WINDOW = 512
FORCE_SCORE = 1e9
NSA_KV = 6
SWA_WINDOW = 128
D_KV_HEADS = 2
D_FF = 256 * ((8 * D_MODEL // 3 + 255) // 256)
CONV_W = 3
IW_SCALE = IDX_HEADS ** -0.5 * IDX_DIM ** -0.5
INT_MIN = -2 ** 31
_NEG_KEY = int(np.array(NEG, np.float32).view(np.int32)) ^ 0x7FFFFFFF

_O_AIW_END = 680
_O_CG_END = 2100
_O_GBR = 2612
C_AQ, C_ALAT, C_AIQ, C_AIKW = 0, 256, 384, 640
C_BQ, C_BK, C_BV = 768, 1024, 1280
C_CQ, C_CKV, C_CG = 1536, 1792, 2176
C_DQ, C_DK, C_DV = 2304, 2560, 2688
W_MIX = 2816
C_GBR = 3072
W_TOT = C_GBR + N_BRANCH * D_MODEL

KC = 512


def _cparams(sem):
    return pltpu.CompilerParams(dimension_semantics=sem, vmem_limit_bytes=VMEM_LIMIT)


def _inproj_kernel(x_ref, g_ref, w_ref, o_ref, h_ref):
    @pl.when(pl.program_id(1) == 0)
    def _():
        x = x_ref[...]
        ms = jnp.mean(x * x, axis=-1, keepdims=True)
        h_ref[...] = (x * lax.rsqrt(ms + EPS) * g_ref[...]).astype(h_ref.dtype)

    o_ref[...] = jnp.dot(h_ref[...], w_ref[...], preferred_element_type=F32)


def _inproj(x2, g, w):
    n = x2.shape[0]
    tm, tn = 512, 1024
    return pl.pallas_call(
        _inproj_kernel,
        grid=(n // tm, W_TOT // tn),
        in_specs=[pl.BlockSpec((tm, D_MODEL), lambda i, j: (i, 0)),
                  pl.BlockSpec((1, D_MODEL), lambda i, j: (0, 0)),
                  pl.BlockSpec((D_MODEL, tn), lambda i, j: (0, j))],
        out_specs=pl.BlockSpec((tm, tn), lambda i, j: (i, j)),
        out_shape=jax.ShapeDtypeStruct((n, W_TOT), F32),
        scratch_shapes=[pltpu.VMEM((tm, D_MODEL), BF16)],
        compiler_params=_cparams(("parallel", "arbitrary")),
        name="inproj",
    )(x2, g, w)


def _rope(x, cosw, sinw, half):
    w = x.shape[1]
    lane = lax.broadcasted_iota(jnp.int32, x.shape, 1)
    first = (lane % (2 * half)) < half
    rot = jnp.where(first, pltpu.roll(x, w - half, 1), pltpu.roll(x, half, 1))
    return x * cosw + rot * sinw


def _gnorm(x, avg, gain):
    ms = jnp.dot(x * x, avg, precision=HIGHEST, preferred_element_type=F32)
    return x * lax.rsqrt(ms + EPS) * gain


def _prep_kernel(p_ref, g_ref, g2_ref, kvup_ref, c64_ref, s64_ref, c32_ref, s32_ref,
                 a64_ref, a32_ref, o_ref, km_ref):
    c64, s64 = c64_ref[...], s64_ref[...]
    c32, s32 = c32_ref[...], s32_ref[...]
    a64, a32 = a64_ref[...], a32_ref[...]
    c64h, s64h, a64h = c64[:, :LANES], s64[:, :LANES], a64[:LANES, :LANES]
    tm = p_ref.shape[0]
    lane1 = lax.broadcasted_iota(jnp.int32, (tm, LANES), 1)
    lo64 = lane1 < HEAD_DIM

    def seg(c, w):
        return p_ref[:, c:c + w], g_ref[:, c:c + w]

    def normrope256(c):
        x, g = seg(c, 256)
        return _rope(_gnorm(x, a64, g), c64, s64, HEAD_DIM // 2)

    o_ref[:, C_AQ:C_AQ + 256] = normrope256(C_AQ)
    x, g = seg(C_ALAT, A_LATENT)
    ms = jnp.mean(x * x, axis=-1, keepdims=True)
    latn = x * lax.rsqrt(ms + EPS) * g
    kv = jnp.dot(latn, kvup_ref[...], precision=HIGHEST, preferred_element_type=F32)
    kr = _rope(_gnorm(kv, a64h, g2_ref[...]), c64h, s64h, HEAD_DIM // 2)
    o_ref[:, C_ALAT:C_ALAT + LANES] = jnp.where(lo64, kr, kv)
    x, _ = seg(C_AIQ, 256)
    o_ref[:, C_AIQ:C_AIQ + 256] = _rope(x, c32, s32, IDX_DIM // 2)
    x, g = seg(C_AIKW, LANES)
    xr = _rope(_gnorm(x, a32, g), c32[:, :LANES], s32[:, :LANES], IDX_DIM // 2)
    o_ref[:, C_AIKW:C_AIKW + LANES] = jnp.where(lane1 < IDX_DIM, xr, x * IW_SCALE)
    o_ref[:, C_BQ:C_BQ + 256] = normrope256(C_BQ)
    kb = normrope256(C_BK)
    o_ref[:, C_BK:C_BK + 256] = kb
    km_ref[0] = jnp.mean(kb, axis=0, keepdims=True)
    o_ref[:, C_BV:C_BV + 256] = p_ref[:, C_BV:C_BV + 256]
    o_ref[:, C_CQ:C_CQ + 256] = normrope256(C_CQ)
    for r in range(3):
        c = C_CKV + r * LANES
        x, g = seg(c, LANES)
        xn = x if r == 0 else _gnorm(x, a64h, g)
        o_ref[:, c:c + LANES] = jnp.where(lo64, _rope(xn, c64h, s64h, HEAD_DIM // 2), x)
    x, _ = seg(C_CG, LANES)
    o_ref[:, C_CG:C_CG + LANES] = 1.0 / (1.0 + jnp.exp(-x))
    o_ref[:, C_DQ:C_DQ + 256] = normrope256(C_DQ)
    x, g = seg(C_DK, LANES)
    o_ref[:, C_DK:C_DK + LANES] = _rope(_gnorm(x, a64h, g), c64h, s64h, HEAD_DIM // 2)
    o_ref[:, C_DV:C_DV + LANES] = p_ref[:, C_DV:C_DV + LANES]


def _prep(proj, gains, g2, kvup, tabs, seq):
    n = proj.shape[0]
    tm = MOBA_BLOCK
    nt = seq // tm
    c64, s64, c32, s32, a64, a32 = tabs
    tab_spec = pl.BlockSpec((tm, 256), lambda i: (i % nt, 0))
    full = lambda a: pl.BlockSpec(a.shape, lambda i: (0,) * a.ndim)
    return pl.pallas_call(
        _prep_kernel,
        grid=(n // tm,),
        in_specs=[pl.BlockSpec((tm, W_MIX), lambda i: (i, 0)), full(gains), full(g2), full(kvup),
                  tab_spec, tab_spec, tab_spec, tab_spec, full(a64), full(a32)],
        out_specs=[pl.BlockSpec((tm, W_MIX), lambda i: (i, 0)),
                   pl.BlockSpec((1, 1, 256), lambda i: (i, 0, 0))],
        out_shape=[jax.ShapeDtypeStruct((n, W_MIX), F32),
                   jax.ShapeDtypeStruct((n // tm, 1, 256), F32)],
        compiler_params=_cparams(("parallel",)),
        name="prep",
    )(proj, gains, g2, kvup, c64, s64, c32, s32, a64, a32)


def _gelu_tanh(x):
    return 0.5 * x * (1.0 + jnp.tanh(math.sqrt(2.0 / math.pi) * (x + 0.044715 * (x * x * x))))


def _cmp_kernel(hk_ref, hv_ref, w1_ref, pe_ref, w2_ref, g_ref, o_ref):
    nrow = hk_ref.shape[1]
    half = CMP_STRIDE * HEAD_DIM
    out = jnp.zeros((nrow, LANES), F32)
    for r, h_ref in enumerate((hk_ref, hv_ref)):
        hb = h_ref[0]
        u = jnp.dot(hb, w1_ref[r, :half, :], precision=HIGHEST, preferred_element_type=F32)
        v = jnp.dot(hb, w1_ref[r, half:, :], precision=HIGHEST, preferred_element_type=F32)
        bias = jnp.dot(pe_ref[r], w1_ref[r], precision=HIGHEST, preferred_element_type=F32)
        pre = u + pltpu.roll(v, nrow - 1, 0) + bias
        out = out + jnp.dot(_gelu_tanh(pre), w2_ref[r], precision=HIGHEST, preferred_element_type=F32)
    lane = lax.broadcasted_iota(jnp.int32, out.shape, 1)
    lo = lane < HEAD_DIM
    ms = jnp.sum(jnp.where(lo, out * out, 0.0), axis=-1, keepdims=True) * (1.0 / HEAD_DIM)
    o_ref[0] = jnp.where(lo, out * lax.rsqrt(ms + EPS) * g_ref[...], out)


def _compress(hk, hv, w1, pe, w2p, gk):
    b, nrow, _ = hk.shape
    full = lambda a: pl.BlockSpec(a.shape, lambda i: (0,) * a.ndim)
    hspec = pl.BlockSpec((1, nrow, CMP_STRIDE * HEAD_DIM), lambda i: (i, 0, 0))
    return pl.pallas_call(
        _cmp_kernel,
        grid=(b,),
        in_specs=[hspec, hspec, full(w1), full(pe), full(w2p), full(gk)],
        out_specs=pl.BlockSpec((1, nrow, LANES), lambda i: (i, 0, 0)),
        out_shape=jax.ShapeDtypeStruct((b, nrow, LANES), F32),
        compiler_params=_cparams(("parallel",)),
        name="nsa_compress",
    )(hk, hv, w1, pe, w2p, gk)


def _tile_rows(x, n):
    return jnp.concatenate([x] * n, axis=0)


def _online_step(carry, s, v):
    m, l, acc = carry
    m_new = jnp.maximum(m, jnp.max(s, axis=-1, keepdims=True))
    e = jnp.exp(s - m_new)
    alpha = jnp.exp(m - m_new)
    l = alpha * l + jnp.sum(e, axis=-1, keepdims=True)
    acc = alpha * acc + jnp.dot(e.astype(BF16), v, preferred_element_type=F32)
    return m_new, l, acc


def _online_init(rows):
    return (jnp.full((rows, 1), NEG, F32), jnp.zeros((rows, 1), F32), jnp.zeros((rows, LANES), F32))


def _top_lowest_index(work, jj, k):
    picks = jnp.zeros(work.shape, F32)
    jf = jj.astype(F32)
    big = float(work.shape[1])
    for _ in range(k):
        mx = jnp.max(work, axis=-1, keepdims=True)
        idx = jnp.min(jnp.where(work == mx, jf, big), axis=-1, keepdims=True)
        pick = jf == idx
        picks = jnp.where(pick, 1.0, picks)
        work = jnp.where(pick, -jnp.inf, work)
    return picks


def _split_bf16(a):
    hi = a.astype(BF16)
    return hi, (a - hi.astype(F32)).astype(BF16)


def _dsa_kernel(q_ref, iq_ref, iw_ref, ikt_ref, kt_ref, kv_ref, o_ref, sc_ref, w_ref, *, seq, topk):
    i = pl.program_id(1)
    rows = q_ref.shape[2]
    nt = KC // LANES
    t0 = i * rows
    nck = (t0 + rows + KC - 1) // KC
    tpos = t0 + lax.broadcasted_iota(jnp.int32, (rows, LANES), 0)
    lane = lax.broadcasted_iota(jnp.int32, (rows, LANES), 1)
    iq = iq_ref[0].reshape(IDX_HEADS * rows, LANES)
    iw = iw_ref[0]
    for h in range(IDX_HEADS):
        w_ref[h] = jnp.broadcast_to(iw[:, IDX_DIM + h:IDX_DIM + h + 1], (rows, LANES))
    ones = jnp.ones((LANES, LANES), BF16)
    ftop = float(topk)

    def score_chunk(c, carry):
        d = jnp.dot(iq, ikt_ref[0, c], preferred_element_type=F32)
        for j in range(nt):
            acc = jnp.zeros((rows, LANES), F32)
            for h in range(IDX_HEADS):
                acc = acc + jnp.maximum(d[h * rows:(h + 1) * rows, j * LANES:(j + 1) * LANES], 0.0) * w_ref[h]
            acc = jnp.where(c * KC + j * LANES + lane <= tpos, acc, NEG)
            bits = lax.bitcast_convert_type(acc, jnp.int32)
            sc_ref[c, :, j * LANES:(j + 1) * LANES] = bits ^ ((bits >> 31) & 0x7FFFFFFF)
        return carry

    lax.fori_loop(0, nck, score_chunk, 0)

    @pl.when(nck % 2 == 1)
    def _():
        sc_ref[nck] = jnp.full((rows, KC), _NEG_KEY, jnp.int32)

    def count(pred):
        def body(p, acc):
            for u in range(2):
                c = 2 * p + u
                for j in range(nt):
                    hit = pred(sc_ref[c, :, j * LANES:(j + 1) * LANES], c * KC + j * LANES + lane)
                    acc = acc + jnp.where(hit, 1.0, 0.0)
            return acc
        acc = lax.fori_loop(0, (nck + 1) // 2, body, jnp.zeros((rows, LANES), F32))
        return jnp.dot(acc.astype(BF16), ones, preferred_element_type=F32)

    zero = jnp.zeros((rows, LANES), jnp.int32)
    thr = jnp.where(count(lambda k, idx: k >= zero) >= ftop, zero, jnp.full_like(zero, INT_MIN))

    def bit_step(b, thr):
        cand = thr | lax.shift_left(jnp.int32(1), 30 - b)
        return jnp.where(count(lambda k, idx: k >= cand) >= ftop, cand, thr)

    thr = lax.fori_loop(0, 31, bit_step, thr)
    need = ftop - count(lambda k, idx: k > thr)
    n_eq = count(lambda k, idx: k == thr)
    nbits = max(1, int(math.ceil(math.log2(seq))))

    def tie_search(_):
        def jstep(b, jb):
            cand = jb | lax.shift_left(jnp.int32(1), nbits - 1 - b)
            c = count(lambda k, idx: jnp.where(k == thr, idx, seq + cand) < cand)
            return jnp.where(c < need, cand, jb)
        return lax.fori_loop(0, nbits, jstep, zero)

    jbound = lax.cond(jnp.max(n_eq - need) > 0.0, tie_search, lambda _: jnp.full_like(zero, seq), 0)

    q = q_ref[0].reshape(BRANCH_HEADS * rows, LANES)

    def attend(c, carry):
        tiles = []
        for j in range(nt):
            k = sc_ref[c, :, j * LANES:(j + 1) * LANES]
            idx = c * KC + j * LANES + lane
            tie_ok = jnp.where(k == thr, idx, seq + jbound) <= jbound
            sel = jnp.where(k > thr, idx, jnp.where(tie_ok, idx, seq + tpos)) <= tpos
            tiles.append(jnp.where(sel, 0.0, -jnp.inf))
        bias = jnp.concatenate(tiles, axis=1)
        s = jnp.dot(q, kt_ref[0, c], preferred_element_type=F32)
        s = (s.reshape(BRANCH_HEADS, rows, KC) + bias[None]).reshape(BRANCH_HEADS * rows, KC)
        return _online_step(carry, s, kv_ref[0, c])

    m, l, acc = lax.fori_loop(0, nck, attend, _online_init(BRANCH_HEADS * rows))
    o_ref[0] = (acc / jnp.maximum(l, TINY)).reshape(BRANCH_HEADS, rows, LANES)


DSA_QB = 128


def _dsa(q, iq, ikw, ikt, kt, kv, seq):
    b = q.shape[0]
    qb = DSA_QB
    nq = seq // qb
    nc = seq // KC
    topk = min(DSA_TOPK, seq // 4)
    return pl.pallas_call(
        functools.partial(_dsa_kernel, seq=seq, topk=topk),
        grid=(b, nq),
        in_specs=[pl.BlockSpec((1, BRANCH_HEADS, qb, LANES), lambda bi, i: (bi, 0, i, 0)),
                  pl.BlockSpec((1, IDX_HEADS, qb, LANES), lambda bi, i: (bi, 0, i, 0)),
                  pl.BlockSpec((1, qb, LANES), lambda bi, i: (bi, i, 0)),
                  pl.BlockSpec((1, nc, LANES, KC), lambda bi, i: (bi, 0, 0, 0)),
                  pl.BlockSpec((1, nc, LANES, KC), lambda bi, i: (bi, 0, 0, 0)),
                  pl.BlockSpec((1, nc, KC, LANES), lambda bi, i: (bi, 0, 0, 0))],
        out_specs=pl.BlockSpec((1, BRANCH_HEADS, qb, LANES), lambda bi, i: (bi, 0, i, 0)),
        out_shape=jax.ShapeDtypeStruct((b, BRANCH_HEADS, seq, LANES), F32),
        scratch_shapes=[pltpu.VMEM((nc, qb, KC), jnp.int32), pltpu.VMEM((IDX_HEADS, qb, LANES), F32)],
        compiler_params=_cparams(("parallel", "arbitrary")),
        name="dsa",
    )(q, iq, ikw, ikt, kt, kv)


def _moba_kernel(q_ref, kmt_ref, kt_ref, kv_ref, o_ref, *, seq):
    own = pl.program_id(2)
    rows = MOBA_BLOCK
    qf = q_ref[0, 0]
    jj = lax.broadcasted_iota(jnp.int32, (rows, LANES), 1)
    gate = jnp.dot(qf, kmt_ref[0, 0], precision=HIGHEST, preferred_element_type=F32)
    gate = jnp.where(jj < own, gate, NEG)
    picks = _top_lowest_index(gate, jj, MOBA_TOPK)
    picks = jnp.where(jj < own, picks, 0.0).astype(BF16)
    qb = qf.astype(BF16)
    kcm = kt_ref.shape[4]
    per = kcm // MOBA_BLOCK
    blk_row = lax.broadcasted_iota(jnp.int32, (LANES, kcm), 0)
    blk_col = lax.broadcasted_iota(jnp.int32, (LANES, kcm), 1) // MOBA_BLOCK
    col = lax.broadcasted_iota(jnp.int32, (rows, kcm), 1)
    own0 = own * MOBA_BLOCK
    tq = own0 + lax.broadcasted_iota(jnp.int32, (rows, 1), 0)

    def group(g, carry):
        expand = jnp.where(blk_row == blk_col + g * per, 1.0, 0.0).astype(BF16)
        hit = jnp.dot(picks, expand, preferred_element_type=F32) > 0.5
        kpos = g * kcm + col
        ok = jnp.where(hit, kpos, jnp.where(kpos >= own0, kpos, seq + tq)) <= tq
        s = jnp.dot(qb, kt_ref[0, 0, g], preferred_element_type=F32)
        return _online_step(carry, jnp.where(ok, s, -jnp.inf), kv_ref[0, 0, g])

    m, l, acc = lax.fori_loop(0, own // per + 1, group, _online_init(rows))
    o_ref[0, 0] = acc / jnp.maximum(l, TINY)


MOBA_KC = 4 * MOBA_BLOCK


def _moba(q, kmt, kt, kv, seq):
    b = q.shape[0]
    nb = seq // MOBA_BLOCK
    kcm = min(MOBA_KC, seq)
    ng = seq // kcm
    return pl.pallas_call(
        functools.partial(_moba_kernel, seq=seq),
        grid=(b, BRANCH_HEADS, nb),
        in_specs=[pl.BlockSpec((1, 1, MOBA_BLOCK, LANES), lambda bi, h, i: (bi, h, i, 0)),
                  pl.BlockSpec((1, 1, LANES, LANES), lambda bi, h, i: (bi, h, 0, 0)),
                  pl.BlockSpec((1, 1, ng, LANES, kcm), lambda bi, h, i: (bi, h, 0, 0, 0)),
                  pl.BlockSpec((1, 1, ng, kcm, LANES), lambda bi, h, i: (bi, h, 0, 0, 0))],
        out_specs=pl.BlockSpec((1, 1, MOBA_BLOCK, LANES), lambda bi, h, i: (bi, h, i, 0)),
        out_shape=jax.ShapeDtypeStruct((b, BRANCH_HEADS, seq, LANES), F32),
        compiler_params=_cparams(("parallel", "parallel", "arbitrary")),
        name="moba",
    )(q, kmt, kt, kv)


def _nsa_kernel(q_ref, g_ref, cmpt_ref, cmp_ref, c2s_ref, kst_ref, kvs_ref, kwt_ref, kvw_ref,
                o_ref, *, seq):
    i = pl.program_id(1)
    qr = q_ref.shape[2]
    t0 = i * qr
    rows = BRANCH_HEADS * qr
    qf = q_ref[0].reshape(rows, LANES)
    qb = qf.astype(BF16)
    tpos = t0 + lax.broadcasted_iota(jnp.int32, (qr, 1), 0)
    tpos4 = _tile_rows(tpos, BRANCH_HEADS)

    ncp = cmpt_ref.shape[2]
    q_hi, q_lo = _split_bf16(qf)
    c_hi, c_lo = _split_bf16(cmpt_ref[0])
    s = (jnp.dot(q_hi, c_hi, preferred_element_type=F32) + jnp.dot(q_hi, c_lo, preferred_element_type=F32)
         + jnp.dot(q_lo, c_hi, preferred_element_type=F32))
    cend = lax.broadcasted_iota(jnp.int32, (rows, ncp), 1) * CMP_STRIDE + (CMP_LEN - 1)
    ok = cend <= tpos4
    s = jnp.where(ok, s, NEG)
    e = jnp.where(ok, jnp.exp(s - jnp.max(s, axis=-1, keepdims=True)), 0.0)
    p = e / jnp.maximum(jnp.sum(e, axis=-1, keepdims=True), TINY)
    o_c = jnp.dot(p.astype(BF16), cmp_ref[0].astype(BF16), preferred_element_type=F32)
    psum = p[0:qr]
    for h in range(1, BRANCH_HEADS):
        psum = psum + p[h * qr:(h + 1) * qr]
    p_hi, p_lo = _split_bf16(psum)
    c2s = c2s_ref[...].astype(BF16)
    imp = (jnp.dot(p_hi, c2s, preferred_element_type=F32)
           + jnp.dot(p_lo, c2s, preferred_element_type=F32))

    jj = lax.broadcasted_iota(jnp.int32, (qr, LANES), 1)
    cur = tpos // SLC_BLOCK
    forced = (jj == 0) | (jj == cur) | (jj == cur - 1)
    imp = jnp.where(forced, FORCE_SCORE, imp)
    imp = jnp.where(jj <= cur, imp, NEG)
    imp = jnp.where(jj < seq // SLC_BLOCK, imp, -jnp.inf)
    ntop = min(SLC_TOPK, seq // SLC_BLOCK)
    picks = _top_lowest_index(imp, jj, ntop)
    picks = jnp.where(jj <= cur, picks, 0.0).astype(BF16)

    nck = (t0 + qr + KC - 1) // KC
    blk_row = lax.broadcasted_iota(jnp.int32, (LANES, KC), 0)
    blk_col = lax.broadcasted_iota(jnp.int32, (LANES, KC), 1) // SLC_BLOCK
    lane = lax.broadcasted_iota(jnp.int32, (qr, KC), 1)

    def slc(c, carry):
        expand = jnp.where(blk_row == blk_col + c * (KC // SLC_BLOCK), 1.0, 0.0).astype(BF16)
        hit = jnp.dot(picks, expand, preferred_element_type=F32) > 0.5
        ok = jnp.where(hit, c * KC + lane, seq + tpos) <= tpos
        bias = jnp.where(ok, 0.0, -jnp.inf)
        sc = jnp.dot(qb, kst_ref[0, c], preferred_element_type=F32)
        sc = (sc.reshape(BRANCH_HEADS, qr, KC) + bias[None]).reshape(rows, KC)
        return _online_step(carry, sc, kvs_ref[0, c])

    m, l, acc = lax.fori_loop(0, nck, slc, _online_init(rows))
    o_s = acc / jnp.maximum(l, TINY)

    nwin = (NSA_WINDOW + qr) // QBLK
    cb = jnp.maximum(i * (qr // QBLK) - NSA_WINDOW // QBLK, 0)
    sw = jnp.concatenate([jnp.dot(qb, kwt_ref[0, cb + r], preferred_element_type=F32)
                          for r in range(nwin)], axis=1)
    kpos = cb * QBLK + lax.broadcasted_iota(jnp.int32, (rows, nwin * QBLK), 1)
    diff = tpos4 - kpos
    ok = jnp.where(diff >= 0, diff, NSA_WINDOW) < NSA_WINDOW
    sw = jnp.where(ok, sw, NEG)
    e = jnp.where(ok, jnp.exp(sw - jnp.max(sw, axis=-1, keepdims=True)), 0.0)
    pw = (e / jnp.maximum(jnp.sum(e, axis=-1, keepdims=True), TINY)).astype(BF16)
    o_w = jnp.zeros((rows, LANES), F32)
    for r in range(nwin):
        o_w = o_w + jnp.dot(pw[:, r * QBLK:(r + 1) * QBLK], kvw_ref[0, cb + r],
                            preferred_element_type=F32)

    g = g_ref[0]
    for h in range(BRANCH_HEADS):
        sl = slice(h * qr, (h + 1) * qr)
        o_ref[0, h] = (g[:, 3 * h:3 * h + 1] * o_c[sl] + g[:, 3 * h + 1:3 * h + 2] * o_s[sl]
                       + g[:, 3 * h + 2:3 * h + 3] * o_w[sl])


NSA_QB = 256


def _nsa(q, g, cmpt, cmp, c2s, kst, kvs, kwt, kvw, seq):
    b = q.shape[0]
    qb = NSA_QB
    nq = seq // QBLK
    nc = seq // KC
    ncp = cmp.shape[1]
    return pl.pallas_call(
        functools.partial(_nsa_kernel, seq=seq),
        grid=(b, seq // qb),
        in_specs=[pl.BlockSpec((1, BRANCH_HEADS, qb, LANES), lambda bi, i: (bi, 0, i, 0)),
                  pl.BlockSpec((1, qb, LANES), lambda bi, i: (bi, i, 0)),
                  pl.BlockSpec((1, LANES, ncp), lambda bi, i: (bi, 0, 0)),
                  pl.BlockSpec((1, ncp, LANES), lambda bi, i: (bi, 0, 0)),
                  pl.BlockSpec((ncp, LANES), lambda bi, i: (0, 0)),
                  pl.BlockSpec((1, nc, LANES, KC), lambda bi, i: (bi, 0, 0, 0)),
                  pl.BlockSpec((1, nc, KC, LANES), lambda bi, i: (bi, 0, 0, 0)),
                  pl.BlockSpec((1, nq, LANES, QBLK), lambda bi, i: (bi, 0, 0, 0)),
                  pl.BlockSpec((1, nq, QBLK, LANES), lambda bi, i: (bi, 0, 0, 0))],
        out_specs=pl.BlockSpec((1, BRANCH_HEADS, qb, LANES), lambda bi, i: (bi, 0, i, 0)),
        out_shape=jax.ShapeDtypeStruct((b, BRANCH_HEADS, seq, LANES), F32),
        compiler_params=_cparams(("parallel", "arbitrary")),
        name="nsa",
    )(q, g, cmpt, cmp, c2s, kst, kvs, kwt, kvw)


def _swa_kernel(sink_ref, q_ref, kt_ref, kv_ref, o_ref):
    i = pl.program_id(1)
    t0 = i * QBLK
    grp = BRANCH_HEADS // D_KV_HEADS
    rows = grp * QBLK
    cb = jnp.maximum(i - 1, 0)
    tpos = _tile_rows(t0 + lax.broadcasted_iota(jnp.int32, (QBLK, 1), 0), grp)
    kpos = cb * QBLK + lax.broadcasted_iota(jnp.int32, (rows, 2 * QBLK), 1)
    diff = tpos - kpos
    ok = jnp.where(diff >= 0, diff, SWA_WINDOW) < SWA_WINDOW
    for c in range(D_KV_HEADS):
        qb = q_ref[0, c * grp:(c + 1) * grp].reshape(rows, LANES).astype(BF16)
        s = jnp.concatenate([jnp.dot(qb, kt_ref[0, c, cb + r], preferred_element_type=F32)
                             for r in range(2)], axis=1)
        s = jnp.where(ok, s, NEG)
        sink = jnp.concatenate([jnp.full((QBLK, 1), sink_ref[c * grp + gi], F32) for gi in range(grp)],
                               axis=0)
        m = jnp.maximum(jnp.max(s, axis=-1, keepdims=True), sink)
        e = jnp.where(ok, jnp.exp(s - m), 0.0)
        den = jnp.sum(e, axis=-1, keepdims=True) + jnp.exp(sink - m)
        p = (e / jnp.maximum(den, TINY)).astype(BF16)
        o = jnp.zeros((rows, LANES), F32)
        for r in range(2):
            o = o + jnp.dot(p[:, r * QBLK:(r + 1) * QBLK], kv_ref[0, c, cb + r],
                            preferred_element_type=F32)
        o_ref[0, c * grp:(c + 1) * grp] = o.reshape(grp, QBLK, LANES)


def _swa(sink, q, kt, kv, seq):
    b = q.shape[0]
    nq = seq // QBLK
    return pl.pallas_call(
        _swa_kernel,
        grid=(b, nq),
        in_specs=[pl.BlockSpec(memory_space=pltpu.SMEM),
                  pl.BlockSpec((1, BRANCH_HEADS, QBLK, LANES), lambda bi, i: (bi, 0, i, 0)),
                  pl.BlockSpec((1, D_KV_HEADS, nq, LANES, QBLK), lambda bi, i: (bi, 0, 0, 0, 0)),
                  pl.BlockSpec((1, D_KV_HEADS, nq, QBLK, LANES), lambda bi, i: (bi, 0, 0, 0, 0))],
        out_specs=pl.BlockSpec((1, BRANCH_HEADS, QBLK, LANES), lambda bi, i: (bi, 0, i, 0)),
        out_shape=jax.ShapeDtypeStruct((b, BRANCH_HEADS, seq, LANES), F32),
        compiler_params=_cparams(("parallel", "arbitrary")),
        name="swa",
    )(sink, q, kt, kv)


def _merge_kernel(x_ref, y_ref, g0_ref, g1_ref, g2_ref, g3_ref, wb_ref, wo_ref, o_ref):
    merged = jnp.zeros(x_ref.shape, F32)
    for n, g_ref in enumerate((g0_ref, g1_ref, g2_ref, g3_ref)):
        br = jnp.dot(y_ref[:, n * BRANCH_WIDTH:(n + 1) * BRANCH_WIDTH], wb_ref[n],
                     preferred_element_type=F32)
        gate = 1.0 / (1.0 + jnp.exp(-g_ref[...]))
        merged = merged + gate * br
    o_ref[...] = x_ref[...] + jnp.dot(merged.astype(BF16), wo_ref[...], preferred_element_type=F32)


def _merge(x2, ys, proj, wb, wo):
    n = x2.shape[0]
    tm = 256
    gb = C_GBR // D_MODEL
    gate_specs = [pl.BlockSpec((tm, D_MODEL), functools.partial(lambda i, k: (i, gb + k), k=k))
                  for k in range(N_BRANCH)]
    return pl.pallas_call(
        _merge_kernel,
        grid=(n // tm,),
        in_specs=[pl.BlockSpec((tm, D_MODEL), lambda i: (i, 0)),
                  pl.BlockSpec((tm, N_BRANCH * BRANCH_WIDTH), lambda i: (i, 0))] + gate_specs +
                 [pl.BlockSpec(wb.shape, lambda i: (0, 0, 0)),
                  pl.BlockSpec(wo.shape, lambda i: (0, 0))],
        out_specs=pl.BlockSpec((tm, D_MODEL), lambda i: (i, 0)),
        out_shape=jax.ShapeDtypeStruct((n, D_MODEL), F32),
        compiler_params=_cparams(("parallel",)),
        name="merge",
    )(x2, ys, proj, proj, proj, proj, wb, wo)


FF_CHUNK = D_FF // 2
FF_HALO = 2 * SUBLANES


def _ffn_kernel(x_ref, xp_ref, g_ref, wg_ref, wv_ref, cw_ref, cb_ref, wd_ref, o_ref, h_ref, acc_ref,
                *, tiles_per_seq):
    i = pl.program_id(0)
    f = pl.program_id(1)
    tm = x_ref.shape[0]

    @pl.when(f == 0)
    def _():
        def norm(x):
            ms = jnp.mean(x * x, axis=-1, keepdims=True)
            return (x * lax.rsqrt(ms + EPS) * g_ref[...]).astype(BF16)
        h_ref[FF_HALO:, :] = norm(x_ref[...])
        h_ref[:FF_HALO, :] = norm(xp_ref[...])
        acc_ref[...] = jnp.zeros(acc_ref.shape, F32)

    h = h_ref[...]
    row = lax.broadcasted_iota(jnp.int32, (tm, 1), 0)
    first_tile = (i % tiles_per_seq) == 0

    def conv(w_ref, half):
        u = jnp.dot(h, w_ref[...], preferred_element_type=F32)
        cw = cw_ref[half, 0]
        out = u[FF_HALO:] * cw[2:3] + cb_ref[half, 0]
        for d in (1, 2):
            prev = pltpu.roll(u, d, 0)[FF_HALO:]
            prev = jnp.where(jnp.logical_and(first_tile, row < d), 0.0, prev)
            out = out + prev * cw[2 - d:3 - d]
        return out

    gate = conv(wg_ref, 0)
    val = conv(wv_ref, 1)
    act = gate / (1.0 + jnp.exp(-gate)) * val
    acc_ref[...] += jnp.dot(act.astype(BF16), wd_ref[...], preferred_element_type=F32)

    @pl.when(f == pl.num_programs(1) - 1)
    def _():
        o_ref[...] = x_ref[...] + acc_ref[...]


def _ffn(x2, g, wup, cw, cb, wd, seq):
    n = x2.shape[0]
    tm = 512
    nf = D_FF // FF_CHUNK
    hb = tm // FF_HALO
    return pl.pallas_call(
        functools.partial(_ffn_kernel, tiles_per_seq=seq // tm),
        grid=(n // tm, nf),
        in_specs=[pl.BlockSpec((tm, D_MODEL), lambda i, f: (i, 0)),
                  pl.BlockSpec((FF_HALO, D_MODEL), lambda i, f: (jnp.maximum(i * hb - 1, 0), 0)),
                  pl.BlockSpec((1, D_MODEL), lambda i, f: (0, 0)),
                  pl.BlockSpec((D_MODEL, FF_CHUNK), lambda i, f: (0, f)),
                  pl.BlockSpec((D_MODEL, FF_CHUNK), lambda i, f: (0, nf + f)),
                  pl.BlockSpec((2, 1, CONV_W, FF_CHUNK), lambda i, f: (0, f, 0, 0)),
                  pl.BlockSpec((2, 1, 1, FF_CHUNK), lambda i, f: (0, f, 0, 0)),
                  pl.BlockSpec((FF_CHUNK, D_MODEL), lambda i, f: (f, 0))],
        out_specs=pl.BlockSpec((tm, D_MODEL), lambda i, f: (i, 0)),
        out_shape=jax.ShapeDtypeStruct((n, D_MODEL), F32),
        scratch_shapes=[pltpu.VMEM((tm + FF_HALO, D_MODEL), BF16), pltpu.VMEM((tm, D_MODEL), F32)],
        compiler_params=_cparams(("parallel", "arbitrary")),
        name="convffn",
    )(x2, x2, g, wup, wup, cw, cb, wd)


def _heads_pad(a, b, s, nh):
    a = a.reshape(b, s, nh, HEAD_DIM).transpose(0, 2, 1, 3)
    return jnp.pad(a, ((0, 0), (0, 0), (0, 0), (0, LANES - HEAD_DIM)))


def _chunk_t(a, c):
    lead = a.shape[:-2]
    s = a.shape[-2]
    a = a.reshape(lead + (s // c, c, LANES))
    return jnp.swapaxes(a, -1, -2)


def _chunk(a, c):
    lead = a.shape[:-2]
    s = a.shape[-2]
    return a.reshape(lead + (s // c, c, LANES))


def _rope_tables(seq):
    def tab(dim, reps):
        inv = ROPE_THETA ** (-jnp.arange(0, dim, 2, dtype=F32) / dim)
        ang = jnp.arange(seq, dtype=F32)[:, None] * inv[None, :]
        cos, sin = jnp.cos(ang), jnp.sin(ang)
        return (jnp.tile(jnp.concatenate([cos, cos], axis=1), (1, reps)),
                jnp.tile(jnp.concatenate([-sin, sin], axis=1), (1, reps)))
    c64, s64 = tab(HEAD_DIM, 256 // HEAD_DIM)
    c32, s32 = tab(IDX_DIM, 256 // IDX_DIM)
    a64 = np.kron(np.eye(256 // HEAD_DIM), np.full((HEAD_DIM, HEAD_DIM), 1.0 / HEAD_DIM)).astype(np.float32)
    a32 = np.kron(np.eye(LANES // IDX_DIM), np.full((IDX_DIM, IDX_DIM), 1.0 / IDX_DIM)).astype(np.float32)
    return c64, s64, c32, s32, jnp.asarray(a64), jnp.asarray(a32)


def _cmp_to_slc(seq, ncp):
    c_start = np.arange(ncp) * CMP_STRIDE
    s_start = np.arange(LANES) * SLC_BLOCK
    m = ((c_start[:, None] < s_start[None, :] + SLC_BLOCK) & (c_start[:, None] + CMP_LEN > s_start[None, :]))
    real = (np.arange(ncp) < (seq - CMP_LEN) // CMP_STRIDE + 1)[:, None] & (np.arange(LANES) < seq // SLC_BLOCK)[None, :]
    return jnp.asarray((m & real).astype(np.float32))


def _pad_w_in(w):
    z = lambda k: jnp.zeros((w.shape[0], k), w.dtype)
    return jnp.concatenate([w[:, :_O_AIW_END], z(C_BQ - _O_AIW_END), w[:, _O_AIW_END:_O_CG_END],
                            z(C_DQ - C_CG - 3 * BRANCH_HEADS), w[:, _O_CG_END:_O_GBR], z(C_GBR - W_MIX),
                            w[:, _O_GBR:]], axis=1)


def _gain_row(a_qk_g, a_lat_g, a_idx_k_g, b_qk_g, c_qk_g, d_qk_g):
    one = lambda k: jnp.ones((k,), F32)
    zero = lambda k: jnp.zeros((k,), F32)
    t4 = lambda g: jnp.tile(g, BRANCH_HEADS)
    row = jnp.concatenate([
        t4(a_qk_g[0]) * ATTN_SCALE, a_lat_g, one(256), a_idx_k_g, zero(LANES - IDX_DIM),
        t4(b_qk_g[0]) * ATTN_SCALE, t4(b_qk_g[1]), one(256),
        t4(c_qk_g[0]) * ATTN_SCALE, one(LANES), c_qk_g[2], one(HEAD_DIM), c_qk_g[3], one(HEAD_DIM), one(LANES),
        t4(d_qk_g[0]) * ATTN_SCALE, jnp.tile(d_qk_g[1], D_KV_HEADS), one(LANES)])
    return row[None, :]


def _mixers(proj, b, s, l, tabs, c2s, a_qk_g, a_lat_g, a_kv_up, a_idx_k_g, b_qk_g, c_qk_g, c_cmp_pe,
            c_cmp_w1, c_cmp_w2, d_qk_g, d_sink):
    gains = _gain_row(a_qk_g[l], a_lat_g[l], a_idx_k_g[l], b_qk_g[l], c_qk_g[l], d_qk_g[l])
    g2 = jnp.concatenate([a_qk_g[l][1], jnp.ones((HEAD_DIM,), F32)])[None, :]
    pp, kmean = _prep(proj, gains, g2, a_kv_up[l], tabs, s)
    col = lambda c, w: pp[:, c:c + w]
    b3 = lambda a: a.reshape(b, s, a.shape[-1])

    def unheads(o):
        return o[..., HEAD_DIM:].transpose(0, 2, 1, 3).reshape(b * s, BRANCH_WIDTH)

    qa = _heads_pad(col(C_AQ, 256), b, s, BRANCH_HEADS).astype(BF16)
    iq = col(C_AIQ, 256).reshape(b, s, IDX_HEADS, IDX_DIM).transpose(0, 2, 1, 3)
    iq = jnp.pad(iq, ((0, 0), (0, 0), (0, 0), (0, LANES - IDX_DIM))).astype(BF16)
    ikw = b3(col(C_AIKW, LANES))
    ik = jnp.where(jnp.arange(LANES) < IDX_DIM, ikw, 0.0).astype(BF16)
    kva = b3(col(C_ALAT, LANES)).astype(BF16)
    ya = unheads(_dsa(qa, iq, ikw, _chunk_t(ik, KC), _chunk_t(kva, KC), _chunk(kva, KC), s))
    qbf = _heads_pad(col(C_BQ, 256), b, s, BRANCH_HEADS)
    kb = col(C_BK, 256).reshape(b, s, BRANCH_HEADS, HEAD_DIM)
    vb = col(C_BV, 256).reshape(b, s, BRANCH_HEADS, HEAD_DIM)
    kvb = jnp.concatenate([kb, vb], axis=-1).transpose(0, 2, 1, 3).astype(BF16)
    nb = s // MOBA_BLOCK
    km = kmean.reshape(b, nb, BRANCH_HEADS, HEAD_DIM).transpose(0, 2, 3, 1)
    km = jnp.pad(km, ((0, 0), (0, 0), (0, LANES - HEAD_DIM), (0, LANES - nb)))
    kcm = min(MOBA_KC, s)
    yb = unheads(_moba(qbf, km, _chunk_t(kvb, kcm), _chunk(kvb, kcm), s))
    qc = _heads_pad(col(C_CQ, 256), b, s, BRANCH_HEADS)
    kvc = b3(col(C_CKV, LANES))
    nrow = s // CMP_STRIDE
    hk = kvc[..., :HEAD_DIM].reshape(b, nrow, CMP_STRIDE * HEAD_DIM)
    hv = kvc[..., HEAD_DIM:].reshape(b, nrow, CMP_STRIDE * HEAD_DIM)
    w2 = c_cmp_w2[l]
    w2p = jnp.stack([jnp.pad(w2[0], ((0, 0), (0, HEAD_DIM))), jnp.pad(w2[1], ((0, 0), (HEAD_DIM, 0)))])
    pe = c_cmp_pe[l].reshape(2, 1, CMP_LEN * HEAD_DIM)
    gk = jnp.concatenate([c_qk_g[l][1], jnp.ones((HEAD_DIM,), F32)])[None, :]
    cmp = _compress(hk, hv, c_cmp_w1[l], pe, w2p, gk)
    ncp = c2s.shape[0]
    cmp = jnp.pad(cmp, ((0, 0), (0, ncp - nrow), (0, 0)))
    cmpt = jnp.where(jnp.arange(LANES)[:, None] < HEAD_DIM, jnp.swapaxes(cmp, 1, 2), 0.0)
    kvs = b3(col(C_CKV + LANES, LANES)).astype(BF16)
    kvw = b3(col(C_CKV + 2 * LANES, LANES)).astype(BF16)
    gc = b3(col(C_CG, LANES))
    yc = unheads(_nsa(qc, gc, cmpt, cmp, c2s, _chunk_t(kvs, KC), _chunk(kvs, KC),
                      _chunk_t(kvw, QBLK), _chunk(kvw, QBLK), s))
    qd = _heads_pad(col(C_DQ, 256), b, s, BRANCH_HEADS)
    kd = col(C_DK, LANES).reshape(b, s, D_KV_HEADS, HEAD_DIM)
    vd = col(C_DV, LANES).reshape(b, s, D_KV_HEADS, HEAD_DIM)
    kvd = jnp.concatenate([kd, vd], axis=-1).transpose(0, 2, 1, 3).astype(BF16)
    yd = unheads(_swa(d_sink[l], qd, _chunk_t(kvd, QBLK), _chunk(kvd, QBLK), s))
    return jnp.concatenate([ya, yb, yc, yd], axis=1).astype(BF16)


def kernel(x, norm1_g, w_in, a_qk_g, a_lat_g, a_kv_up, a_idx_k_g, b_qk_g, c_qk_g, c_cmp_pe, c_cmp_w1,
           c_cmp_w2, d_qk_g, d_sink, w_branch, w_out, norm2_g, w_up, conv_w, conv_b, w_down):
    b, s, _ = x.shape
    depth = w_in.shape[0]
    assert s % KC == 0 and s >= NSA_WINDOW + NSA_QB and KC >= min(DSA_TOPK, s // 4)
    assert s // LANES <= 256
    assert (s // KC) % 2 == 0
    tabs = _rope_tables(s)
    ncp = max(LANES, s // CMP_STRIDE)
    c2s = _cmp_to_slc(s, ncp)
    x2 = x.reshape(b * s, D_MODEL)
    nf = D_FF // FF_CHUNK
    for l in range(depth):
        proj = _inproj(x2, norm1_g[l][None, :], _pad_w_in(w_in[l]).astype(BF16))
        ys = _mixers(proj, b, s, l, tabs, c2s, a_qk_g, a_lat_g, a_kv_up, a_idx_k_g, b_qk_g, c_qk_g,
                     c_cmp_pe, c_cmp_w1, c_cmp_w2, d_qk_g, d_sink)
        x2 = _merge(x2, ys, proj, w_branch[l].astype(BF16), w_out[l].astype(BF16))
        cw = conv_w[l].reshape(CONV_W, 2, nf, FF_CHUNK).transpose(1, 2, 0, 3)
        cb = conv_b[l].reshape(2, nf, 1, FF_CHUNK)
        x2 = _ffn(x2, norm2_g[l][None, :], w_up[l].astype(BF16), cw, cb, w_down[l].astype(BF16), s)
    return x2.reshape(b, s, D_MODEL)
```

```python
import functools
import math

import numpy as np
import jax
import jax.numpy as jnp
from jax import lax
from jax.experimental import pallas as pl
from jax.experimental.pallas import tpu as pltpu

F32 = jnp.float32
BF16 = jnp.bfloat16
HIGHEST = lax.Precision.HIGHEST

LANES = 128
SUBLANES = 8
VMEM_LIMIT = 48 * 1024 * 1024

D_MODEL = 1024
HEAD_DIM = 64
N_BRANCH = 4
BRANCH_HEADS = 4
BRANCH_WIDTH = BRANCH_HEADS * HEAD_DIM
ROPE_THETA = 10000.0
QBLK = 128
NEG = -1e30
TINY = 1e-30
EPS = 1e-6
ATTN_SCALE = HEAD_DIM ** -0.5
A_LATENT = 128
IDX_HEADS = 8
IDX_DIM = 32
DSA_TOPK = 256
MOBA_BLOCK = 256
MOBA_TOPK = 3
CMP_LEN = 32
CMP_STRIDE = 16
CMP_HIDDEN = 128
SLC_BLOCK = 64
SLC_TOPK = 16
NSA_WINDOW = 512
FORCE_SCORE = 1e9
NSA_KV = 6
SWA_WINDOW = 128
D_KV_HEADS = 2
D_FF = 256 * ((8 * D_MODEL // 3 + 255) // 256)
CONV_W = 3
IW_SCALE = IDX_HEADS ** -0.5 * IDX_DIM ** -0.5
LOG2E = math.log2(math.e)
INT_MIN = -2 ** 31

_O_AIW_END = 680
_O_CG_END = 2100
_O_GBR = 2612
C_AQ, C_ALAT, C_AIQ, C_AIKW = 0, 256, 384, 640
C_BQ, C_BK, C_BV = 768, 1024, 1280
C_CQ, C_CKV, C_CG = 1536, 1792, 2176
C_DQ, C_DK, C_DV = 2304, 2560, 2688
W_MIX = 2816
C_GBR = 3072
W_TOT = C_GBR + N_BRANCH * D_MODEL

KC = 512


def _cparams(sem):
    return pltpu.CompilerParams(dimension_semantics=sem, vmem_limit_bytes=VMEM_LIMIT)


def _inproj_kernel(x_ref, g_ref, w_ref, o_ref, h_ref):
    @pl.when(pl.program_id(1) == 0)
    def _():
        x = x_ref[...]
        ms = jnp.mean(x * x, axis=-1, keepdims=True)
        h_ref[...] = (x * lax.rsqrt(ms + EPS) * g_ref[...]).astype(h_ref.dtype)

    o_ref[...] = jnp.dot(h_ref[...], w_ref[...], preferred_element_type=F32)


def _inproj(x2, g, w):
    n = x2.shape[0]
    tm, tn = 512, 1024
    return pl.pallas_call(
        _inproj_kernel,
        grid=(n // tm, W_TOT // tn),
        in_specs=[pl.BlockSpec((tm, D_MODEL), lambda i, j: (i, 0)),
                  pl.BlockSpec((1, D_MODEL), lambda i, j: (0, 0)),
                  pl.BlockSpec((D_MODEL, tn), lambda i, j: (0, j))],
        out_specs=pl.BlockSpec((tm, tn), lambda i, j: (i, j)),
        out_shape=jax.ShapeDtypeStruct((n, W_TOT), F32),
        scratch_shapes=[pltpu.VMEM((tm, D_MODEL), BF16)],
        compiler_params=_cparams(("parallel", "arbitrary")),
        name="inproj",
    )(x2, g, w)


def _rope(x, cosw, sinw, half):
    w = x.shape[1]
    lane = lax.broadcasted_iota(jnp.int32, x.shape, 1)
    first = (lane % (2 * half)) < half
    rot = jnp.where(first, pltpu.roll(x, w - half, 1), pltpu.roll(x, half, 1))
    return x * cosw + rot * sinw


def _gnorm(x, avg, gain):
    ms = jnp.dot(x * x, avg, precision=HIGHEST, preferred_element_type=F32)
    return x * lax.rsqrt(ms + EPS) * gain


def _prep_kernel(p_ref, g_ref, g2_ref, kvup_ref, c64_ref, s64_ref, c32_ref, s32_ref,
                 a64_ref, a32_ref, o_ref, km_ref):
    c64, s64 = c64_ref[...], s64_ref[...]
    c32, s32 = c32_ref[...], s32_ref[...]
    a64, a32 = a64_ref[...], a32_ref[...]
    c64h, s64h, a64h = c64[:, :LANES], s64[:, :LANES], a64[:LANES, :LANES]
    tm = p_ref.shape[0]
    lane1 = lax.broadcasted_iota(jnp.int32, (tm, LANES), 1)
    lo64 = lane1 < HEAD_DIM

    def seg(c, w):
        return p_ref[:, c:c + w], g_ref[:, c:c + w]

    def normrope256(c):
        x, g = seg(c, 256)
        return _rope(_gnorm(x, a64, g), c64, s64, HEAD_DIM // 2)

    o_ref[:, C_AQ:C_AQ + 256] = normrope256(C_AQ)
    x, g = seg(C_ALAT, A_LATENT)
    ms = jnp.mean(x * x, axis=-1, keepdims=True)
    latn = x * lax.rsqrt(ms + EPS) * g
    kv = jnp.dot(latn, kvup_ref[...], precision=HIGHEST, preferred_element_type=F32)
    kr = _rope(_gnorm(kv, a64h, g2_ref[...]), c64h, s64h, HEAD_DIM // 2)
    o_ref[:, C_ALAT:C_ALAT + LANES] = jnp.where(lo64, kr, kv)
    x, _ = seg(C_AIQ, 256)
    o_ref[:, C_AIQ:C_AIQ + 256] = _rope(x, c32, s32, IDX_DIM // 2)
    x, g = seg(C_AIKW, LANES)
    xr = _rope(_gnorm(x, a32, g), c32[:, :LANES], s32[:, :LANES], IDX_DIM // 2)
    o_ref[:, C_AIKW:C_AIKW + LANES] = jnp.where(lane1 < IDX_DIM, xr, x * IW_SCALE)
    o_ref[:, C_BQ:C_BQ + 256] = normrope256(C_BQ)
    kb = normrope256(C_BK)
    o_ref[:, C_BK:C_BK + 256] = kb
    km_ref[0] = jnp.mean(kb, axis=0, keepdims=True)
    o_ref[:, C_BV:C_BV + 256] = p_ref[:, C_BV:C_BV + 256]
    o_ref[:, C_CQ:C_CQ + 256] = normrope256(C_CQ)
    for r in range(3):
        c = C_CKV + r * LANES
        x, g = seg(c, LANES)
        xn = x if r == 0 else _gnorm(x, a64h, g)
        o_ref[:, c:c + LANES] = jnp.where(lo64, _rope(xn, c64h, s64h, HEAD_DIM // 2), x)
    x, _ = seg(C_CG, LANES)
    o_ref[:, C_CG:C_CG + LANES] = 1.0 / (1.0 + jnp.exp(-x))
    o_ref[:, C_DQ:C_DQ + 256] = normrope256(C_DQ)
    x, g = seg(C_DK, LANES)
    o_ref[:, C_DK:C_DK + LANES] = _rope(_gnorm(x, a64h, g), c64h, s64h, HEAD_DIM // 2)
    o_ref[:, C_DV:C_DV + LANES] = p_ref[:, C_DV:C_DV + LANES]


def _prep(proj, gains, g2, kvup, tabs, seq):
    n = proj.shape[0]
    tm = MOBA_BLOCK
    nt = seq // tm
    c64, s64, c32, s32, a64, a32 = tabs
    tab_spec = pl.BlockSpec((tm, 256), lambda i: (i % nt, 0))
    full = lambda a: pl.BlockSpec(a.shape, lambda i: (0,) * a.ndim)
    return pl.pallas_call(
        _prep_kernel,
        grid=(n // tm,),
        in_specs=[pl.BlockSpec((tm, W_MIX), lambda i: (i, 0)), full(gains), full(g2), full(kvup),
                  tab_spec, tab_spec, tab_spec, tab_spec, full(a64), full(a32)],
        out_specs=[pl.BlockSpec((tm, W_MIX), lambda i: (i, 0)),
                   pl.BlockSpec((1, 1, 256), lambda i: (i, 0, 0))],
        out_shape=[jax.ShapeDtypeStruct((n, W_MIX), F32),
                   jax.ShapeDtypeStruct((n // tm, 1, 256), F32)],
        compiler_params=_cparams(("parallel",)),
        name="prep",
    )(proj, gains, g2, kvup, c64, s64, c32, s32, a64, a32)


def _gelu_tanh(x):
    return 0.5 * x * (1.0 + jnp.tanh(math.sqrt(2.0 / math.pi) * (x + 0.044715 * (x * x * x))))


def _cmp_kernel(hk_ref, hv_ref, w1_ref, pe_ref, w2_ref, g_ref, o_ref):
    nrow = hk_ref.shape[1]
    half = CMP_STRIDE * HEAD_DIM
    out = jnp.zeros((nrow, LANES), F32)
    for r, h_ref in enumerate((hk_ref, hv_ref)):
        hb = h_ref[0]
        u = jnp.dot(hb, w1_ref[r, :half, :], precision=HIGHEST, preferred_element_type=F32)
        v = jnp.dot(hb, w1_ref[r, half:, :], precision=HIGHEST, preferred_element_type=F32)
        bias = jnp.dot(pe_ref[r], w1_ref[r], precision=HIGHEST, preferred_element_type=F32)
        pre = u + pltpu.roll(v, nrow - 1, 0) + bias
        out = out + jnp.dot(_gelu_tanh(pre), w2_ref[r], precision=HIGHEST, preferred_element_type=F32)
    lane = lax.broadcasted_iota(jnp.int32, out.shape, 1)
    lo = lane < HEAD_DIM
    ms = jnp.sum(jnp.where(lo, out * out, 0.0), axis=-1, keepdims=True) * (1.0 / HEAD_DIM)
    o_ref[0] = jnp.where(lo, out * lax.rsqrt(ms + EPS) * g_ref[...], out)


def _compress(hk, hv, w1, pe, w2p, gk):
    b, nrow, _ = hk.shape
    full = lambda a: pl.BlockSpec(a.shape, lambda i: (0,) * a.ndim)
    hspec = pl.BlockSpec((1, nrow, CMP_STRIDE * HEAD_DIM), lambda i: (i, 0, 0))
    return pl.pallas_call(
        _cmp_kernel,
        grid=(b,),
        in_specs=[hspec, hspec, full(w1), full(pe), full(w2p), full(gk)],
        out_specs=pl.BlockSpec((1, nrow, LANES), lambda i: (i, 0, 0)),
        out_shape=jax.ShapeDtypeStruct((b, nrow, LANES), F32),
        compiler_params=_cparams(("parallel",)),
        name="nsa_compress",
    )(hk, hv, w1, pe, w2p, gk)


def _tile_rows(x, n):
    return jnp.concatenate([x] * n, axis=0)


def _online_step(carry, s, ov):
    m, acc = carry
    m_new = jnp.maximum(m, jnp.max(s, axis=-1, keepdims=True))
    e = jnp.exp2(s - m_new)
    acc = jnp.exp2(m - m_new) * acc + jnp.dot(e.astype(BF16), ov, preferred_element_type=F32)
    return m_new, acc


def _online_init(rows):
    return (jnp.full((rows, 1), NEG, F32), jnp.zeros((rows, LANES), F32))


def _online_finish(carry):
    _, acc = carry
    den = jnp.maximum(pltpu.roll(acc, HEAD_DIM, 1), TINY)
    lane = lax.broadcasted_iota(jnp.int32, acc.shape, 1)
    return jnp.where(lane >= HEAD_DIM, acc / den, 0.0)


def _top_lowest_index(work, jj, k):
    picks = jnp.zeros(work.shape, F32)
    jf = jj.astype(F32)
    big = float(work.shape[1])
    for _ in range(k):
        mx = jnp.max(work, axis=-1, keepdims=True)
        idx = jnp.min(jnp.where(work == mx, jf, big), axis=-1, keepdims=True)
        pick = jf == idx
        picks = jnp.where(pick, 1.0, picks)
        work = jnp.where(pick, -jnp.inf, work)
    return picks


def _split_bf16(a):
    hi = a.astype(BF16)
    return hi, (a - hi.astype(F32)).astype(BF16)


def _dsa_kernel(q_ref, iqt_ref, iw_ref, ik_ref, kt_ref, kv_ref, o_ref, sc_ref, *, seq, topk):
    i = pl.program_id(1)
    rows = QBLK
    t0 = i * rows
    nck = (t0 + rows + KC - 1) // KC
    tq = t0 + lax.broadcasted_iota(jnp.int32, (1, LANES), 1)
    krow = lax.broadcasted_iota(jnp.int32, (KC, LANES), 0)
    iqt = iqt_ref[0, 0]
    iw = iw_ref[0, 0]
    ftop = float(topk)

    def score_chunk(c, carry):
        d = jnp.dot(ik_ref[0, c], iqt, preferred_element_type=F32)
        acc = jnp.zeros((KC, LANES), F32)
        for h in range(IDX_HEADS):
            acc = acc + jnp.maximum(d[:, h * LANES:(h + 1) * LANES], 0.0) * iw[h:h + 1, :]
        acc = jnp.where(c * KC + krow <= tq, acc, NEG)
        bits = lax.bitcast_convert_type(acc, jnp.int32)
        sc_ref[c] = bits ^ ((bits >> 31) & 0x7FFFFFFF)
        return carry

    lax.fori_loop(0, nck, score_chunk, 0)

    def count(pred):
        def body(c, acc):
            hit = jnp.where(pred(sc_ref[c], c * KC + krow), 1.0, 0.0)
            while hit.shape[0] > SUBLANES:
                half = hit.shape[0] // 2
                hit = hit[:half] + hit[half:]
            return acc + hit
        acc = lax.fori_loop(0, nck, body, jnp.zeros((SUBLANES, LANES), F32))
        return jnp.sum(acc, axis=0, keepdims=True)

    zero = jnp.zeros((1, LANES), jnp.int32)
    thr = jnp.where(count(lambda k, idx: k >= zero) >= ftop, zero, jnp.full_like(zero, INT_MIN))

    def bit_step(b, thr):
        cand = thr | lax.shift_left(jnp.int32(1), 30 - b)
        return jnp.where(count(lambda k, idx: k >= cand) >= ftop, cand, thr)

    thr = lax.fori_loop(0, 31, bit_step, thr)
    need = ftop - count(lambda k, idx: k > thr)
    n_eq = count(lambda k, idx: k == thr)
    nbits = max(1, int(math.ceil(math.log2(seq))))

    def tie_search(_):
        def jstep(b, jb):
            cand = jb | lax.shift_left(jnp.int32(1), nbits - 1 - b)
            c = count(lambda k, idx: jnp.where(k == thr, idx, seq + cand) < cand)
            return jnp.where(c < need, cand, jb)
        return lax.fori_loop(0, nbits, jstep, zero)

    jbound = lax.cond(jnp.max(n_eq - need) > 0.0, tie_search, lambda _: jnp.full_like(zero, seq), 0)

    q = q_ref[0].reshape(BRANCH_HEADS * rows, LANES)
    eye = jnp.where(lax.broadcasted_iota(jnp.int32, (LANES, LANES), 0)
                    == lax.broadcasted_iota(jnp.int32, (LANES, LANES), 1), 1.0, 0.0).astype(BF16)

    def attend(c, carry):
        k = sc_ref[c]
        idx = c * KC + krow
        tie_ok = jnp.where(k == thr, idx, seq + jbound) <= jbound
        sel = jnp.where(k > thr, idx, jnp.where(tie_ok, idx, seq + tq)) <= tq
        sel_t = jnp.where(sel, 1.0, 0.0).astype(BF16)
        hit = lax.dot_general(eye, sel_t, (((1,), (1,)), ((), ())), preferred_element_type=F32)
        bias = jnp.where(hit > 0.5, 0.0, -jnp.inf)
        s = jnp.dot(q, kt_ref[0, c], preferred_element_type=F32)
        s = (s.reshape(BRANCH_HEADS, rows, KC) + bias[None]).reshape(BRANCH_HEADS * rows, KC)
        return _online_step(carry, s, kv_ref[0, c])

    out = _online_finish(lax.fori_loop(0, nck, attend, _online_init(BRANCH_HEADS * rows)))
    o_ref[0] = out.reshape(BRANCH_HEADS, rows, LANES)


def _dsa(q, iqt, iw, ik, kt, kv, seq):
    b = q.shape[0]
    nq = seq // QBLK
    nc = seq // KC
    topk = min(DSA_TOPK, seq // 4)
    return pl.pallas_call(
        functools.partial(_dsa_kernel, seq=seq, topk=topk),
        grid=(b, nq),
        in_specs=[pl.BlockSpec((1, BRANCH_HEADS, QBLK, LANES), lambda bi, i: (bi, 0, i, 0)),
                  pl.BlockSpec((1, 1, LANES, IDX_HEADS * QBLK), lambda bi, i: (bi, i, 0, 0)),
                  pl.BlockSpec((1, 1, IDX_HEADS, QBLK), lambda bi, i: (bi, i, 0, 0)),
                  pl.BlockSpec((1, nc, KC, LANES), lambda bi, i: (bi, 0, 0, 0)),
                  pl.BlockSpec((1, nc, LANES, KC), lambda bi, i: (bi, 0, 0, 0)),
                  pl.BlockSpec((1, nc, KC, LANES), lambda bi, i: (bi, 0, 0, 0))],
        out_specs=pl.BlockSpec((1, BRANCH_HEADS, QBLK, LANES), lambda bi, i: (bi, 0, i, 0)),
        out_shape=jax.ShapeDtypeStruct((b, BRANCH_HEADS, seq, LANES), F32),
        scratch_shapes=[pltpu.VMEM((nc, KC, QBLK), jnp.int32)],
        compiler_params=_cparams(("parallel", "arbitrary")),
        name="dsa",
    )(q, iqt, iw, ik, kt, kv)


def _moba_kernel(q_ref, kmt_ref, kt_ref, kv_ref, o_ref, *, seq):
    own = pl.program_id(2)
    rows = MOBA_BLOCK
    qf = q_ref[0, 0]
    jj = lax.broadcasted_iota(jnp.int32, (rows, LANES), 1)
    gate = jnp.dot(qf, kmt_ref[0, 0], precision=HIGHEST, preferred_element_type=F32)
    gate = jnp.where(jj < own, gate, NEG)
    picks = _top_lowest_index(gate, jj, MOBA_TOPK)
    picks = jnp.where(jj < own, picks, 0.0).astype(BF16)
    qb = qf.astype(BF16)
    kcm = kt_ref.shape[4]
    per = kcm // MOBA_BLOCK
    blk_row = lax.broadcasted_iota(jnp.int32, (LANES, kcm), 0)
    blk_col = lax.broadcasted_iota(jnp.int32, (LANES, kcm), 1) // MOBA_BLOCK
    col = lax.broadcasted_iota(jnp.int32, (rows, kcm), 1)
    own0 = own * MOBA_BLOCK
    tq = own0 + lax.broadcasted_iota(jnp.int32, (rows, 1), 0)

    def group(g, carry):
        expand = jnp.where(blk_row == blk_col + g * per, 1.0, 0.0).astype(BF16)
        hit = jnp.dot(picks, expand, preferred_element_type=F32) > 0.5
        kpos = g * kcm + col
        ok = jnp.where(hit, kpos, jnp.where(kpos >= own0, kpos, seq + tq)) <= tq
        s = jnp.dot(qb, kt_ref[0, 0, g], preferred_element_type=F32)
        return _online_step(carry, jnp.where(ok, s, -jnp.inf), kv_ref[0, 0, g])

    o_ref[0, 0] = _online_finish(lax.fori_loop(0, own // per + 1, group, _online_init(rows)))


MOBA_KC = 4 * MOBA_BLOCK


def _moba(q, kmt, kt, kv, seq):
    b = q.shape[0]
    nb = seq // MOBA_BLOCK
    kcm = min(MOBA_KC, seq)
    ng = seq // kcm
    return pl.pallas_call(
        functools.partial(_moba_kernel, seq=seq),
        grid=(b, BRANCH_HEADS, nb),
        in_specs=[pl.BlockSpec((1, 1, MOBA_BLOCK, LANES), lambda bi, h, i: (bi, h, i, 0)),
                  pl.BlockSpec((1, 1, LANES, LANES), lambda bi, h, i: (bi, h, 0, 0)),
                  pl.BlockSpec((1, 1, ng, LANES, kcm), lambda bi, h, i: (bi, h, 0, 0, 0)),
                  pl.BlockSpec((1, 1, ng, kcm, LANES), lambda bi, h, i: (bi, h, 0, 0, 0))],
        out_specs=pl.BlockSpec((1, 1, MOBA_BLOCK, LANES), lambda bi, h, i: (bi, h, i, 0)),
        out_shape=jax.ShapeDtypeStruct((b, BRANCH_HEADS, seq, LANES), F32),
        compiler_params=_cparams(("parallel", "parallel", "arbitrary")),
        name="moba",
    )(q, kmt, kt, kv)


def _nsa_kernel(q_ref, g_ref, cmpt_ref, cmp_ref, c2s_ref, kst_ref, kvs_ref, kwt_ref, kvw_ref,
                o_ref, *, seq):
    i = pl.program_id(1)
    qr = q_ref.shape[2]
    t0 = i * qr
    rows = BRANCH_HEADS * qr
    qf = q_ref[0].reshape(rows, LANES)
    qb = qf.astype(BF16)
    tpos = t0 + lax.broadcasted_iota(jnp.int32, (qr, 1), 0)
    tpos4 = _tile_rows(tpos, BRANCH_HEADS)

    ncp = cmpt_ref.shape[2]
    q_hi, q_lo = _split_bf16(qf)
    c_hi, c_lo = _split_bf16(cmpt_ref[0])
    s = (jnp.dot(q_hi, c_hi, preferred_element_type=F32) + jnp.dot(q_hi, c_lo, preferred_element_type=F32)
         + jnp.dot(q_lo, c_hi, preferred_element_type=F32))
    cend = lax.broadcasted_iota(jnp.int32, (rows, ncp), 1) * CMP_STRIDE + (CMP_LEN - 1)
    ok = cend <= tpos4
    s = jnp.where(ok, s, NEG)
    e = jnp.where(ok, jnp.exp(s - jnp.max(s, axis=-1, keepdims=True)), 0.0)
    p = e / jnp.maximum(jnp.sum(e, axis=-1, keepdims=True), TINY)
    o_c = jnp.dot(p.astype(BF16), cmp_ref[0].astype(BF16), preferred_element_type=F32)
    psum = p[0:qr]
    for h in range(1, BRANCH_HEADS):
        psum = psum + p[h * qr:(h + 1) * qr]
    p_hi, p_lo = _split_bf16(psum)
    c2s = c2s_ref[...].astype(BF16)
    imp = (jnp.dot(p_hi, c2s, preferred_element_type=F32)
           + jnp.dot(p_lo, c2s, preferred_element_type=F32))

    jj = lax.broadcasted_iota(jnp.int32, (qr, LANES), 1)
    cur = tpos // SLC_BLOCK
    forced = (jj == 0) | (jj == cur) | (jj == cur - 1)
    imp = jnp.where(forced, FORCE_SCORE, imp)
    imp = jnp.where(jj <= cur, imp, NEG)
    imp = jnp.where(jj < seq // SLC_BLOCK, imp, -jnp.inf)
    ntop = min(SLC_TOPK, seq // SLC_BLOCK)
    picks = _top_lowest_index(imp, jj, ntop)
    picks = jnp.where(jj <= cur, picks, 0.0).astype(BF16)

    nck = (t0 + qr + KC - 1) // KC
    blk_row = lax.broadcasted_iota(jnp.int32, (LANES, KC), 0)
    blk_col = lax.broadcasted_iota(jnp.int32, (LANES, KC), 1) // SLC_BLOCK
    lane = lax.broadcasted_iota(jnp.int32, (qr, KC), 1)

    def slc(c, carry):
        expand = jnp.where(blk_row == blk_col + c * (KC // SLC_BLOCK), 1.0, 0.0).astype(BF16)
        hit = jnp.dot(picks, expand, preferred_element_type=F32) > 0.5
        ok = jnp.where(hit, c * KC + lane, seq + tpos) <= tpos
        bias = jnp.where(ok, 0.0, -jnp.inf)
        sc = jnp.dot(qb, kst_ref[0, c], preferred_element_type=F32)
        sc = (sc.reshape(BRANCH_HEADS, qr, KC) + bias[None]).reshape(rows, KC)
        return _online_step(carry, sc, kvs_ref[0, c])

    o_s = _online_finish(lax.fori_loop(0, nck, slc, _online_init(rows)))

    nwin = (NSA_WINDOW + qr) // QBLK
    cb = jnp.maximum(i * (qr // QBLK) - NSA_WINDOW // QBLK, 0)
    sw = jnp.concatenate([jnp.dot(qb, kwt_ref[0, cb + r], preferred_element_type=F32)
                          for r in range(nwin)], axis=1)
    kpos = cb * QBLK + lax.broadcasted_iota(jnp.int32, (rows, nwin * QBLK), 1)
    diff = tpos4 - kpos
    ok = jnp.where(diff >= 0, diff, NSA_WINDOW) < NSA_WINDOW
    sw = jnp.where(ok, sw, NEG)
    e = jnp.where(ok, jnp.exp(sw - jnp.max(sw, axis=-1, keepdims=True)), 0.0)
    pw = (e / jnp.maximum(jnp.sum(e, axis=-1, keepdims=True), TINY)).astype(BF16)
    o_w = jnp.zeros((rows, LANES), F32)
    for r in range(nwin):
        o_w = o_w + jnp.dot(pw[:, r * QBLK:(r + 1) * QBLK], kvw_ref[0, cb + r],
                            preferred_element_type=F32)

    g = g_ref[0]
    for h in range(BRANCH_HEADS):
        sl = slice(h * qr, (h + 1) * qr)
        o_ref[0, h] = (g[:, 3 * h:3 * h + 1] * o_c[sl] + g[:, 3 * h + 1:3 * h + 2] * o_s[sl]
                       + g[:, 3 * h + 2:3 * h + 3] * o_w[sl])


NSA_QB = 256


def _nsa(q, g, cmpt, cmp, c2s, kst, kvs, kwt, kvw, seq):
    b = q.shape[0]
    qb = NSA_QB
    nq = seq // QBLK
    nc = seq // KC
    ncp = cmp.shape[1]
    return pl.pallas_call(
        functools.partial(_nsa_kernel, seq=seq),
        grid=(b, seq // qb),
        in_specs=[pl.BlockSpec((1, BRANCH_HEADS, qb, LANES), lambda bi, i: (bi, 0, i, 0)),
                  pl.BlockSpec((1, qb, LANES), lambda bi, i: (bi, i, 0)),
                  pl.BlockSpec((1, LANES, ncp), lambda bi, i: (bi, 0, 0)),
                  pl.BlockSpec((1, ncp, LANES), lambda bi, i: (bi, 0, 0)),
                  pl.BlockSpec((ncp, LANES), lambda bi, i: (0, 0)),
                  pl.BlockSpec((1, nc, LANES, KC), lambda bi, i: (bi, 0, 0, 0)),
                  pl.BlockSpec((1, nc, KC, LANES), lambda bi, i: (bi, 0, 0, 0)),
                  pl.BlockSpec((1, nq, LANES, QBLK), lambda bi, i: (bi, 0, 0, 0)),
                  pl.BlockSpec((1, nq, QBLK, LANES), lambda bi, i: (bi, 0, 0, 0))],
        out_specs=pl.BlockSpec((1, BRANCH_HEADS, qb, LANES), lambda bi, i: (bi, 0, i, 0)),
        out_shape=jax.ShapeDtypeStruct((b, BRANCH_HEADS, seq, LANES), F32),
        compiler_params=_cparams(("parallel", "arbitrary")),
        name="nsa",
    )(q, g, cmpt, cmp, c2s, kst, kvs, kwt, kvw)


def _swa_kernel(sink_ref, q_ref, kt_ref, kv_ref, o_ref):
    i = pl.program_id(1)
    t0 = i * QBLK
    grp = BRANCH_HEADS // D_KV_HEADS
    rows = grp * QBLK
    cb = jnp.maximum(i - 1, 0)
    tpos = _tile_rows(t0 + lax.broadcasted_iota(jnp.int32, (QBLK, 1), 0), grp)
    kpos = cb * QBLK + lax.broadcasted_iota(jnp.int32, (rows, 2 * QBLK), 1)
    diff = tpos - kpos
    ok = jnp.where(diff >= 0, diff, SWA_WINDOW) < SWA_WINDOW
    for c in range(D_KV_HEADS):
        qb = q_ref[0, c * grp:(c + 1) * grp].reshape(rows, LANES).astype(BF16)
        s = jnp.concatenate([jnp.dot(qb, kt_ref[0, c, cb + r], preferred_element_type=F32)
                             for r in range(2)], axis=1)
        s = jnp.where(ok, s, NEG)
        sink = jnp.concatenate([jnp.full((QBLK, 1), sink_ref[c * grp + gi], F32) for gi in range(grp)],
                               axis=0)
        m = jnp.maximum(jnp.max(s, axis=-1, keepdims=True), sink)
        e = jnp.where(ok, jnp.exp(s - m), 0.0)
        den = jnp.sum(e, axis=-1, keepdims=True) + jnp.exp(sink - m)
        p = (e / jnp.maximum(den, TINY)).astype(BF16)
        o = jnp.zeros((rows, LANES), F32)
        for r in range(2):
            o = o + jnp.dot(p[:, r * QBLK:(r + 1) * QBLK], kv_ref[0, c, cb + r],
                            preferred_element_type=F32)
        o_ref[0, c * grp:(c + 1) * grp] = o.reshape(grp, QBLK, LANES)


def _swa(sink, q, kt, kv, seq):
    b = q.shape[0]
    nq = seq // QBLK
    return pl.pallas_call(
        _swa_kernel,
        grid=(b, nq),
        in_specs=[pl.BlockSpec(memory_space=pltpu.SMEM),
                  pl.BlockSpec((1, BRANCH_HEADS, QBLK, LANES), lambda bi, i: (bi, 0, i, 0)),
                  pl.BlockSpec((1, D_KV_HEADS, nq, LANES, QBLK), lambda bi, i: (bi, 0, 0, 0, 0)),
                  pl.BlockSpec((1, D_KV_HEADS, nq, QBLK, LANES), lambda bi, i: (bi, 0, 0, 0, 0))],
        out_specs=pl.BlockSpec((1, BRANCH_HEADS, QBLK, LANES), lambda bi, i: (bi, 0, i, 0)),
        out_shape=jax.ShapeDtypeStruct((b, BRANCH_HEADS, seq, LANES), F32),
        compiler_params=_cparams(("parallel", "arbitrary")),
        name="swa",
    )(sink, q, kt, kv)


def _merge_kernel(x_ref, y_ref, g0_ref, g1_ref, g2_ref, g3_ref, wb_ref, wo_ref, o_ref):
    merged = jnp.zeros(x_ref.shape, F32)
    for n, g_ref in enumerate((g0_ref, g1_ref, g2_ref, g3_ref)):
        br = jnp.dot(y_ref[:, n * BRANCH_WIDTH:(n + 1) * BRANCH_WIDTH], wb_ref[n],
                     preferred_element_type=F32)
        gate = 1.0 / (1.0 + jnp.exp(-g_ref[...]))
        merged = merged + gate * br
    o_ref[...] = x_ref[...] + jnp.dot(merged.astype(BF16), wo_ref[...], preferred_element_type=F32)


def _merge(x2, ys, proj, wb, wo):
    n = x2.shape[0]
    tm = 256
    gb = C_GBR // D_MODEL
    gate_specs = [pl.BlockSpec((tm, D_MODEL), functools.partial(lambda i, k: (i, gb + k), k=k))
                  for k in range(N_BRANCH)]
    return pl.pallas_call(
        _merge_kernel,
        grid=(n // tm,),
        in_specs=[pl.BlockSpec((tm, D_MODEL), lambda i: (i, 0)),
                  pl.BlockSpec((tm, N_BRANCH * BRANCH_WIDTH), lambda i: (i, 0))] + gate_specs +
                 [pl.BlockSpec(wb.shape, lambda i: (0, 0, 0)),
                  pl.BlockSpec(wo.shape, lambda i: (0, 0))],
        out_specs=pl.BlockSpec((tm, D_MODEL), lambda i: (i, 0)),
        out_shape=jax.ShapeDtypeStruct((n, D_MODEL), F32),
        compiler_params=_cparams(("parallel",)),
        name="merge",
    )(x2, ys, proj, proj, proj, proj, wb, wo)


FF_CHUNK = D_FF // 2
FF_HALO = 2 * SUBLANES


def _ffn_kernel(x_ref, xp_ref, g_ref, wg_ref, wv_ref, cw_ref, cb_ref, wd_ref, o_ref, h_ref, acc_ref,
                *, tiles_per_seq):
    i = pl.program_id(0)
    f = pl.program_id(1)
    tm = x_ref.shape[0]

    @pl.when(f == 0)
    def _():
        def norm(x):
            ms = jnp.mean(x * x, axis=-1, keepdims=True)
            return (x * lax.rsqrt(ms + EPS) * g_ref[...]).astype(BF16)
        h_ref[FF_HALO:, :] = norm(x_ref[...])
        h_ref[:FF_HALO, :] = norm(xp_ref[...])
        acc_ref[...] = jnp.zeros(acc_ref.shape, F32)

    h = h_ref[...]
    row = lax.broadcasted_iota(jnp.int32, (tm, 1), 0)
    first_tile = (i % tiles_per_seq) == 0

    def conv(w_ref, half):
        u = jnp.dot(h, w_ref[...], preferred_element_type=F32)
        cw = cw_ref[half, 0]
        out = u[FF_HALO:] * cw[2:3] + cb_ref[half, 0]
        for d in (1, 2):
            prev = pltpu.roll(u, d, 0)[FF_HALO:]
            prev = jnp.where(jnp.logical_and(first_tile, row < d), 0.0, prev)
            out = out + prev * cw[2 - d:3 - d]
        return out

    gate = conv(wg_ref, 0)
    val = conv(wv_ref, 1)
    act = gate / (1.0 + jnp.exp(-gate)) * val
    acc_ref[...] += jnp.dot(act.astype(BF16), wd_ref[...], preferred_element_type=F32)

    @pl.when(f == pl.num_programs(1) - 1)
    def _():
        o_ref[...] = x_ref[...] + acc_ref[...]


def _ffn(x2, g, wup, cw, cb, wd, seq):
    n = x2.shape[0]
    tm = 512
    nf = D_FF // FF_CHUNK
    hb = tm // FF_HALO
    return pl.pallas_call(
        functools.partial(_ffn_kernel, tiles_per_seq=seq // tm),
        grid=(n // tm, nf),
        in_specs=[pl.BlockSpec((tm, D_MODEL), lambda i, f: (i, 0)),
                  pl.BlockSpec((FF_HALO, D_MODEL), lambda i, f: (jnp.maximum(i * hb - 1, 0), 0)),
                  pl.BlockSpec((1, D_MODEL), lambda i, f: (0, 0)),
                  pl.BlockSpec((D_MODEL, FF_CHUNK), lambda i, f: (0, f)),
                  pl.BlockSpec((D_MODEL, FF_CHUNK), lambda i, f: (0, nf + f)),
                  pl.BlockSpec((2, 1, CONV_W, FF_CHUNK), lambda i, f: (0, f, 0, 0)),
                  pl.BlockSpec((2, 1, 1, FF_CHUNK), lambda i, f: (0, f, 0, 0)),
                  pl.BlockSpec((FF_CHUNK, D_MODEL), lambda i, f: (f, 0))],
        out_specs=pl.BlockSpec((tm, D_MODEL), lambda i, f: (i, 0)),
        out_shape=jax.ShapeDtypeStruct((n, D_MODEL), F32),
        scratch_shapes=[pltpu.VMEM((tm + FF_HALO, D_MODEL), BF16), pltpu.VMEM((tm, D_MODEL), F32)],
        compiler_params=_cparams(("parallel", "arbitrary")),
        name="convffn",
    )(x2, x2, g, wup, wup, cw, cb, wd)


def _heads_pad(a, b, s, nh):
    a = a.reshape(b, s, nh, HEAD_DIM).transpose(0, 2, 1, 3)
    return jnp.pad(a, ((0, 0), (0, 0), (0, 0), (0, LANES - HEAD_DIM)))


def _chunk_t(a, c):
    lead = a.shape[:-2]
    s = a.shape[-2]
    a = a.reshape(lead + (s // c, c, LANES))
    return jnp.swapaxes(a, -1, -2)


def _chunk(a, c):
    lead = a.shape[:-2]
    s = a.shape[-2]
    return a.reshape(lead + (s // c, c, LANES))


def _rope_tables(seq):
    def tab(dim, reps):
        inv = ROPE_THETA ** (-jnp.arange(0, dim, 2, dtype=F32) / dim)
        ang = jnp.arange(seq, dtype=F32)[:, None] * inv[None, :]
        cos, sin = jnp.cos(ang), jnp.sin(ang)
        return (jnp.tile(jnp.concatenate([cos, cos], axis=1), (1, reps)),
                jnp.tile(jnp.concatenate([-sin, sin], axis=1), (1, reps)))
    c64, s64 = tab(HEAD_DIM, 256 // HEAD_DIM)
    c32, s32 = tab(IDX_DIM, 256 // IDX_DIM)
    a64 = np.kron(np.eye(256 // HEAD_DIM), np.full((HEAD_DIM, HEAD_DIM), 1.0 / HEAD_DIM)).astype(np.float32)
    a32 = np.kron(np.eye(LANES // IDX_DIM), np.full((IDX_DIM, IDX_DIM), 1.0 / IDX_DIM)).astype(np.float32)
    return c64, s64, c32, s32, jnp.asarray(a64), jnp.asarray(a32)


def _cmp_to_slc(seq, ncp):
    c_start = np.arange(ncp) * CMP_STRIDE
    s_start = np.arange(LANES) * SLC_BLOCK
    m = ((c_start[:, None] < s_start[None, :] + SLC_BLOCK) & (c_start[:, None] + CMP_LEN > s_start[None, :]))
    real = (np.arange(ncp) < (seq - CMP_LEN) // CMP_STRIDE + 1)[:, None] & (np.arange(LANES) < seq // SLC_BLOCK)[None, :]
    return jnp.asarray((m & real).astype(np.float32))


def _pad_w_in(w):
    z = lambda k: jnp.zeros((w.shape[0], k), w.dtype)
    return jnp.concatenate([w[:, :_O_AIW_END], z(C_BQ - _O_AIW_END), w[:, _O_AIW_END:_O_CG_END],
                            z(C_DQ - C_CG - 3 * BRANCH_HEADS), w[:, _O_CG_END:_O_GBR], z(C_GBR - W_MIX),
                            w[:, _O_GBR:]], axis=1)


def _gain_row(a_qk_g, a_lat_g, a_idx_k_g, b_qk_g, c_qk_g, d_qk_g):
    one = lambda k: jnp.ones((k,), F32)
    zero = lambda k: jnp.zeros((k,), F32)
    t4 = lambda g: jnp.tile(g, BRANCH_HEADS)
    row = jnp.concatenate([
        t4(a_qk_g[0]) * ATTN_SCALE, a_lat_g, one(256), a_idx_k_g, zero(LANES - IDX_DIM),
        t4(b_qk_g[0]) * ATTN_SCALE, t4(b_qk_g[1]), one(256),
        t4(c_qk_g[0]) * ATTN_SCALE, one(LANES), c_qk_g[2], one(HEAD_DIM), c_qk_g[3], one(HEAD_DIM), one(LANES),
        t4(d_qk_g[0]) * ATTN_SCALE, jnp.tile(d_qk_g[1], D_KV_HEADS), one(LANES)])
    return row[None, :]


def _mixers(proj, b, s, l, tabs, c2s, a_qk_g, a_lat_g, a_kv_up, a_idx_k_g, b_qk_g, c_qk_g, c_cmp_pe,
            c_cmp_w1, c_cmp_w2, d_qk_g, d_sink):
    gains = _gain_row(a_qk_g[l], a_lat_g[l], a_idx_k_g[l], b_qk_g[l], c_qk_g[l], d_qk_g[l])
    g2 = jnp.concatenate([a_qk_g[l][1], jnp.ones((HEAD_DIM,), F32)])[None, :]
    pp, kmean = _prep(proj, gains, g2, a_kv_up[l], tabs, s)
    col = lambda c, w: pp[:, c:c + w]
    b3 = lambda a: a.reshape(b, s, a.shape[-1])

    def unheads(o):
        return o[..., HEAD_DIM:].transpose(0, 2, 1, 3).reshape(b * s, BRANCH_WIDTH)

    lo = jnp.arange(LANES) < HEAD_DIM

    def keys2(kv):
        return jnp.where(lo, kv * LOG2E, kv).astype(BF16)

    def ones_v(kv):
        return jnp.where(lo, 1.0, kv).astype(BF16)

    qa = _heads_pad(col(C_AQ, 256), b, s, BRANCH_HEADS).astype(BF16)
    nqa = s // QBLK
    iqt = col(C_AIQ, 256).reshape(b, nqa, QBLK, IDX_HEADS, IDX_DIM).transpose(0, 1, 4, 3, 2)
    iqt = jnp.pad(iqt, ((0, 0), (0, 0), (0, LANES - IDX_DIM), (0, 0), (0, 0)))
    iqt = iqt.reshape(b, nqa, LANES, IDX_HEADS * QBLK).astype(BF16)
    ikw = b3(col(C_AIKW, LANES))
    iw = ikw[..., IDX_DIM:IDX_DIM + IDX_HEADS].reshape(b, nqa, QBLK, IDX_HEADS).transpose(0, 1, 3, 2)
    ik = jnp.where(jnp.arange(LANES) < IDX_DIM, ikw, 0.0).astype(BF16)
    kva = b3(col(C_ALAT, LANES))
    ya = unheads(_dsa(qa, iqt, iw, _chunk(ik, KC), _chunk_t(keys2(kva), KC), _chunk(ones_v(kva), KC), s))
    qbf = _heads_pad(col(C_BQ, 256), b, s, BRANCH_HEADS)
    kb = col(C_BK, 256).reshape(b, s, BRANCH_HEADS, HEAD_DIM)
    vb = col(C_BV, 256).reshape(b, s, BRANCH_HEADS, HEAD_DIM)
    kvb = jnp.concatenate([kb, vb], axis=-1).transpose(0, 2, 1, 3)
    nb = s // MOBA_BLOCK
    km = kmean.reshape(b, nb, BRANCH_HEADS, HEAD_DIM).transpose(0, 2, 3, 1)
    km = jnp.pad(km, ((0, 0), (0, 0), (0, LANES - HEAD_DIM), (0, LANES - nb)))
    kcm = min(MOBA_KC, s)
    yb = unheads(_moba(qbf, km, _chunk_t(keys2(kvb), kcm), _chunk(ones_v(kvb), kcm), s))
    qc = _heads_pad(col(C_CQ, 256), b, s, BRANCH_HEADS)
    kvc = b3(col(C_CKV, LANES))
    nrow = s // CMP_STRIDE
    hk = kvc[..., :HEAD_DIM].reshape(b, nrow, CMP_STRIDE * HEAD_DIM)
    hv = kvc[..., HEAD_DIM:].reshape(b, nrow, CMP_STRIDE * HEAD_DIM)
    w2 = c_cmp_w2[l]
    w2p = jnp.stack([jnp.pad(w2[0], ((0, 0), (0, HEAD_DIM))), jnp.pad(w2[1], ((0, 0), (HEAD_DIM, 0)))])
    pe = c_cmp_pe[l].reshape(2, 1, CMP_LEN * HEAD_DIM)
    gk = jnp.concatenate([c_qk_g[l][1], jnp.ones((HEAD_DIM,), F32)])[None, :]
    cmp = _compress(hk, hv, c_cmp_w1[l], pe, w2p, gk)
    ncp = c2s.shape[0]
    cmp = jnp.pad(cmp, ((0, 0), (0, ncp - nrow), (0, 0)))
    cmpt = jnp.where(jnp.arange(LANES)[:, None] < HEAD_DIM, jnp.swapaxes(cmp, 1, 2), 0.0)
    kvs = b3(col(C_CKV + LANES, LANES))
    kvw = b3(col(C_CKV + 2 * LANES, LANES)).astype(BF16)
    gc = b3(col(C_CG, LANES))
    yc = unheads(_nsa(qc, gc, cmpt, cmp, c2s, _chunk_t(keys2(kvs), KC), _chunk(ones_v(kvs), KC),
                      _chunk_t(kvw, QBLK), _chunk(kvw, QBLK), s))
    qd = _heads_pad(col(C_DQ, 256), b, s, BRANCH_HEADS)
    kd = col(C_DK, LANES).reshape(b, s, D_KV_HEADS, HEAD_DIM)
    vd = col(C_DV, LANES).reshape(b, s, D_KV_HEADS, HEAD_DIM)
    kvd = jnp.concatenate([kd, vd], axis=-1).transpose(0, 2, 1, 3).astype(BF16)
    yd = unheads(_swa(d_sink[l], qd, _chunk_t(kvd, QBLK), _chunk(kvd, QBLK), s))
    return jnp.concatenate([ya, yb, yc, yd], axis=1).astype(BF16)


def kernel(x, norm1_g, w_in, a_qk_g, a_lat_g, a_kv_up, a_idx_k_g, b_qk_g, c_qk_g, c_cmp_pe, c_cmp_w1,
           c_cmp_w2, d_qk_g, d_sink, w_branch, w_out, norm2_g, w_up, conv_w, conv_b, w_down):
    b, s, _ = x.shape
    depth = w_in.shape[0]
    assert s % KC == 0 and s >= NSA_WINDOW + NSA_QB and KC >= min(DSA_TOPK, s // 4)
    tabs = _rope_tables(s)
    ncp = max(LANES, s // CMP_STRIDE)
    c2s = _cmp_to_slc(s, ncp)
    x2 = x.reshape(b * s, D_MODEL)
    nf = D_FF // FF_CHUNK
    for l in range(depth):
        proj = _inproj(x2, norm1_g[l][None, :], _pad_w_in(w_in[l]).astype(BF16))
        ys = _mixers(proj, b, s, l, tabs, c2s, a_qk_g, a_lat_g, a_kv_up, a_idx_k_g, b_qk_g, c_qk_g,
                     c_cmp_pe, c_cmp_w1, c_cmp_w2, d_qk_g, d_sink)
        x2 = _merge(x2, ys, proj, w_branch[l].astype(BF16), w_out[l].astype(BF16))
        cw = conv_w[l].reshape(CONV_W, 2, nf, FF_CHUNK).transpose(1, 2, 0, 3)
        cb = conv_b[l].reshape(2, nf, 1, FF_CHUNK)
        x2 = _ffn(x2, norm2_g[l][None, :], w_up[l].astype(BF16), cw, cb, w_down[l].astype(BF16), s)
    return x2.reshape(b, s, D_MODEL)
```

```python
import functools
import math

import numpy as np
import jax
import jax.numpy as jnp
from jax import lax
from jax.experimental import pallas as pl
from jax.experimental.pallas import tpu as pltpu

F32 = jnp.float32
BF16 = jnp.bfloat16
HIGHEST = lax.Precision.HIGHEST

LANES = 128
SUBLANES = 8
VMEM_LIMIT = 48 * 1024 * 1024

D_MODEL = 1024
HEAD_DIM = 64
N_BRANCH = 4
BRANCH_HEADS = 4
BRANCH_WIDTH = BRANCH_HEADS * HEAD_DIM
ROPE_THETA = 10000.0
QBLK = 128
NEG = -1e30
TINY = 1e-30
EPS = 1e-6
ATTN_SCALE = HEAD_DIM ** -0.5
A_LATENT = 128
IDX_HEADS = 8
IDX_DIM = 32
DSA_TOPK = 256
MOBA_BLOCK = 256
MOBA_TOPK = 3
CMP_LEN = 32
CMP_STRIDE = 16
CMP_HIDDEN = 128
SLC_BLOCK = 64
SLC_TOPK = 16
NSA_WINDOW = 512
FORCE_SCORE = 1e9
NSA_KV = 6
SWA_WINDOW = 128
D_KV_HEADS = 2
D_FF = 256 * ((8 * D_MODEL // 3 + 255) // 256)
CONV_W = 3
IW_SCALE = IDX_HEADS ** -0.5 * IDX_DIM ** -0.5
LOG2E = math.log2(math.e)
INT_MIN = -2 ** 31

_O_AIW_END = 680
_O_CG_END = 2100
_O_GBR = 2612
C_AQ, C_ALAT, C_AIQ, C_AIKW = 0, 256, 384, 640
C_BQ, C_BK, C_BV = 768, 1024, 1280
C_CQ, C_CKV, C_CG = 1536, 1792, 2176
C_DQ, C_DK, C_DV = 2304, 2560, 2688
W_MIX = 2816
C_GBR = 3072
W_TOT = C_GBR + N_BRANCH * D_MODEL

KC = 512


def _cparams(sem):
    return pltpu.CompilerParams(dimension_semantics=sem, vmem_limit_bytes=VMEM_LIMIT)


def _inproj_kernel(x_ref, g_ref, w_ref, o_ref, h_ref):
    @pl.when(pl.program_id(1) == 0)
    def _():
        x = x_ref[...]
        ms = jnp.mean(x * x, axis=-1, keepdims=True)
        h_ref[...] = (x * lax.rsqrt(ms + EPS) * g_ref[...]).astype(h_ref.dtype)

    o_ref[...] = jnp.dot(h_ref[...], w_ref[...], preferred_element_type=F32)


def _inproj(x2, g, w):
    n = x2.shape[0]
    tm, tn = 512, 1024
    return pl.pallas_call(
        _inproj_kernel,
        grid=(n // tm, W_TOT // tn),
        in_specs=[pl.BlockSpec((tm, D_MODEL), lambda i, j: (i, 0)),
                  pl.BlockSpec((1, D_MODEL), lambda i, j: (0, 0)),
                  pl.BlockSpec((D_MODEL, tn), lambda i, j: (0, j))],
        out_specs=pl.BlockSpec((tm, tn), lambda i, j: (i, j)),
        out_shape=jax.ShapeDtypeStruct((n, W_TOT), F32),
        scratch_shapes=[pltpu.VMEM((tm, D_MODEL), BF16)],
        compiler_params=_cparams(("parallel", "arbitrary")),
        name="inproj",
    )(x2, g, w)


def _rope(x, cosw, sinw, half):
    w = x.shape[1]
    lane = lax.broadcasted_iota(jnp.int32, x.shape, 1)
    first = (lane % (2 * half)) < half
    rot = jnp.where(first, pltpu.roll(x, w - half, 1), pltpu.roll(x, half, 1))
    return x * cosw + rot * sinw


def _gnorm(x, avg, gain):
    ms = jnp.dot(x * x, avg, precision=HIGHEST, preferred_element_type=F32)
    return x * lax.rsqrt(ms + EPS) * gain


def _prep_kernel(p_ref, g_ref, g2_ref, kvup_ref, c64_ref, s64_ref, c32_ref, s32_ref, a64_ref, a32_ref,
                 qa_ref, iq_ref, ikw_ref, ik_ref, kxa_ref, ova_ref,
                 qb_ref, kxb_ref, ovb_ref, km_ref,
                 qc_ref, kvc_ref, kxs_ref, ovs_ref, kvw_ref, gc_ref,
                 qd_ref, kvd_ref):
    c64, s64 = c64_ref[...], s64_ref[...]
    c32, s32 = c32_ref[...], s32_ref[...]
    a64, a32 = a64_ref[...], a32_ref[...]
    c64h, s64h, a64h = c64[:, :LANES], s64[:, :LANES], a64[:LANES, :LANES]
    tm = p_ref.shape[0]
    lane1 = lax.broadcasted_iota(jnp.int32, (tm, LANES), 1)
    lo64 = lane1 < HEAD_DIM

    def seg(c, w):
        return p_ref[:, c:c + w], g_ref[:, c:c + w]

    def normrope256(c):
        x, g = seg(c, 256)
        return _rope(_gnorm(x, a64, g), c64, s64, HEAD_DIM // 2)

    def swap(x):
        return pltpu.roll(x, HEAD_DIM, 1)

    def put_heads(q, ref):
        for p in range(BRANCH_HEADS // 2):
            pair = q[:, p * LANES:(p + 1) * LANES]
            ref[0, 2 * p] = jnp.where(lo64, pair, 0.0).astype(ref.dtype)
            ref[0, 2 * p + 1] = jnp.where(lo64, swap(pair), 0.0).astype(ref.dtype)

    def put_kv(kv, kx_ref, ov_ref, idx):
        kx_ref[idx] = jnp.where(lo64, kv * LOG2E, kv).astype(BF16)
        ov_ref[idx] = jnp.where(lo64, 1.0, kv).astype(BF16)

    put_heads(normrope256(C_AQ), qa_ref)
    x, g = seg(C_ALAT, A_LATENT)
    ms = jnp.mean(x * x, axis=-1, keepdims=True)
    latn = x * lax.rsqrt(ms + EPS) * g
    kv = jnp.dot(latn, kvup_ref[...], precision=HIGHEST, preferred_element_type=F32)
    kr = _rope(_gnorm(kv, a64h, g2_ref[...]), c64h, s64h, HEAD_DIM // 2)
    put_kv(jnp.where(lo64, kr, kv), kxa_ref, ova_ref, 0)
    x, _ = seg(C_AIQ, 256)
    iq = _rope(x, c32, s32, IDX_DIM // 2)
    per = LANES // IDX_DIM
    for h in range(IDX_HEADS):
        tile = iq[:, (h // per) * LANES:(h // per + 1) * LANES]
        if h % per:
            tile = pltpu.roll(tile, LANES - (h % per) * IDX_DIM, 1)
        iq_ref[0, h] = jnp.where(lane1 < IDX_DIM, tile, 0.0).astype(BF16)
    x, g = seg(C_AIKW, LANES)
    xr = _rope(_gnorm(x, a32, g), c32[:, :LANES], s32[:, :LANES], IDX_DIM // 2)
    ikw_ref[0] = jnp.where(lane1 < IDX_DIM, xr, x * IW_SCALE)
    ik_ref[0] = jnp.where(lane1 < IDX_DIM, xr, 0.0).astype(BF16)
    put_heads(normrope256(C_BQ), qb_ref)
    kb = normrope256(C_BK)
    km_ref[0] = jnp.mean(kb, axis=0, keepdims=True)
    for p in range(BRANCH_HEADS // 2):
        kp = kb[:, p * LANES:(p + 1) * LANES]
        vp = p_ref[:, C_BV + p * LANES:C_BV + (p + 1) * LANES]
        put_kv(jnp.where(lo64, kp, swap(vp)), kxb_ref, ovb_ref, (0, 2 * p))
        put_kv(jnp.where(lo64, swap(kp), vp), kxb_ref, ovb_ref, (0, 2 * p + 1))
    put_heads(normrope256(C_CQ), qc_ref)
    for r in range(3):
        c = C_CKV + r * LANES
        x, g = seg(c, LANES)
        xn = x if r == 0 else _gnorm(x, a64h, g)
        kv = jnp.where(lo64, _rope(xn, c64h, s64h, HEAD_DIM // 2), x)
        if r == 0:
            kvc_ref[0] = kv
        elif r == 1:
            put_kv(kv, kxs_ref, ovs_ref, 0)
        else:
            kvw_ref[0] = kv.astype(BF16)
    x, _ = seg(C_CG, LANES)
    gc_ref[0] = 1.0 / (1.0 + jnp.exp(-x))
    put_heads(normrope256(C_DQ), qd_ref)
    x, g = seg(C_DK, LANES)
    kd = _rope(_gnorm(x, a64h, g), c64h, s64h, HEAD_DIM // 2)
    vd = p_ref[:, C_DV:C_DV + LANES]
    kvd_ref[0, 0] = jnp.where(lo64, kd, swap(vd)).astype(BF16)
    kvd_ref[0, 1] = jnp.where(lo64, swap(kd), vd).astype(BF16)


def _prep(proj, gains, g2, kvup, tabs, b, seq):
    n = proj.shape[0]
    tm = MOBA_BLOCK
    nt = seq // tm
    c64, s64, c32, s32, a64, a32 = tabs
    tab_spec = pl.BlockSpec((tm, 256), lambda i: (i % nt, 0))
    full = lambda a: pl.BlockSpec(a.shape, lambda i: (0,) * a.ndim)
    tok = lambda dt: (jax.ShapeDtypeStruct((b, seq, LANES), dt),
                      pl.BlockSpec((1, tm, LANES), lambda i: (i // nt, i % nt, 0)))
    hd = lambda nh, dt: (jax.ShapeDtypeStruct((b, nh, seq, LANES), dt),
                         pl.BlockSpec((1, nh, tm, LANES), lambda i: (i // nt, 0, i % nt, 0)))
    outs = [hd(BRANCH_HEADS, BF16), hd(IDX_HEADS, BF16), tok(F32), tok(BF16), tok(BF16), tok(BF16),
            hd(BRANCH_HEADS, F32), hd(BRANCH_HEADS, BF16), hd(BRANCH_HEADS, BF16),
            (jax.ShapeDtypeStruct((n // tm, 1, 256), F32), pl.BlockSpec((1, 1, 256), lambda i: (i, 0, 0))),
            hd(BRANCH_HEADS, F32), tok(F32), tok(BF16), tok(BF16), tok(BF16), tok(F32),
            hd(BRANCH_HEADS, BF16), hd(D_KV_HEADS, BF16)]
    return pl.pallas_call(
        _prep_kernel,
        grid=(n // tm,),
        in_specs=[pl.BlockSpec((tm, W_MIX), lambda i: (i, 0)), full(gains), full(g2), full(kvup),
                  tab_spec, tab_spec, tab_spec, tab_spec, full(a64), full(a32)],
        out_specs=[o[1] for o in outs],
        out_shape=[o[0] for o in outs],
        compiler_params=_cparams(("parallel",)),
        name="prep",
    )(proj, gains, g2, kvup, c64, s64, c32, s32, a64, a32)


def _gelu_tanh(x):
    return 0.5 * x * (1.0 + jnp.tanh(math.sqrt(2.0 / math.pi) * (x + 0.044715 * (x * x * x))))


def _cmp_kernel(hk_ref, hv_ref, w1_ref, pe_ref, w2_ref, g_ref, o_ref):
    nrow = hk_ref.shape[1]
    half = CMP_STRIDE * HEAD_DIM
    out = jnp.zeros((nrow, LANES), F32)
    for r, h_ref in enumerate((hk_ref, hv_ref)):
        hb = h_ref[0]
        u = jnp.dot(hb, w1_ref[r, :half, :], precision=HIGHEST, preferred_element_type=F32)
        v = jnp.dot(hb, w1_ref[r, half:, :], precision=HIGHEST, preferred_element_type=F32)
        bias = jnp.dot(pe_ref[r], w1_ref[r], precision=HIGHEST, preferred_element_type=F32)
        pre = u + pltpu.roll(v, nrow - 1, 0) + bias
        out = out + jnp.dot(_gelu_tanh(pre), w2_ref[r], precision=HIGHEST, preferred_element_type=F32)
    lane = lax.broadcasted_iota(jnp.int32, out.shape, 1)
    lo = lane < HEAD_DIM
    ms = jnp.sum(jnp.where(lo, out * out, 0.0), axis=-1, keepdims=True) * (1.0 / HEAD_DIM)
    o_ref[0] = jnp.where(lo, out * lax.rsqrt(ms + EPS) * g_ref[...], out)


def _compress(hk, hv, w1, pe, w2p, gk):
    b, nrow, _ = hk.shape
    full = lambda a: pl.BlockSpec(a.shape, lambda i: (0,) * a.ndim)
    hspec = pl.BlockSpec((1, nrow, CMP_STRIDE * HEAD_DIM), lambda i: (i, 0, 0))
    return pl.pallas_call(
        _cmp_kernel,
        grid=(b,),
        in_specs=[hspec, hspec, full(w1), full(pe), full(w2p), full(gk)],
        out_specs=pl.BlockSpec((1, nrow, LANES), lambda i: (i, 0, 0)),
        out_shape=jax.ShapeDtypeStruct((b, nrow, LANES), F32),
        compiler_params=_cparams(("parallel",)),
        name="nsa_compress",
    )(hk, hv, w1, pe, w2p, gk)


def _tile_rows(x, n):
    return jnp.concatenate([x] * n, axis=0)


def _online_step(carry, s, ov):
    m, acc = carry
    m_new = jnp.maximum(m, jnp.max(s, axis=-1, keepdims=True))
    e = jnp.exp2(s - m_new)
    acc = jnp.exp2(m - m_new) * acc + jnp.dot(e.astype(BF16), ov, preferred_element_type=F32)
    return m_new, acc


def _online_init(rows):
    return (jnp.full((rows, 1), NEG, F32), jnp.zeros((rows, LANES), F32))


def _online_finish(carry):
    _, acc = carry
    den = jnp.maximum(pltpu.roll(acc, HEAD_DIM, 1), TINY)
    lane = lax.broadcasted_iota(jnp.int32, acc.shape, 1)
    return jnp.where(lane >= HEAD_DIM, acc / den, 0.0)


def _top_lowest_index(work, jj, k):
    picks = jnp.zeros(work.shape, F32)
    jf = jj.astype(F32)
    big = float(work.shape[1])
    for _ in range(k):
        mx = jnp.max(work, axis=-1, keepdims=True)
        idx = jnp.min(jnp.where(work == mx, jf, big), axis=-1, keepdims=True)
        pick = jf == idx
        picks = jnp.where(pick, 1.0, picks)
        work = jnp.where(pick, -jnp.inf, work)
    return picks


def _split_bf16(a):
    hi = a.astype(BF16)
    return hi, (a - hi.astype(F32)).astype(BF16)


_NT = (((1,), (1,)), ((), ()))


def _pair_heads(o_even, o_odd):
    lane = lax.broadcasted_iota(jnp.int32, o_even.shape, 1)
    return jnp.where(lane < HEAD_DIM, pltpu.roll(o_even, HEAD_DIM, 1), o_odd)


def _dsa_kernel(q_ref, iq_ref, iw_ref, ik_ref, kx_ref, ov_ref, o_ref, sc_ref, *, seq, topk):
    i = pl.program_id(1)
    rows = QBLK
    t0 = i * rows
    nck = (t0 + rows + KC - 1) // KC
    tq = t0 + lax.broadcasted_iota(jnp.int32, (1, LANES), 1)
    krow = lax.broadcasted_iota(jnp.int32, (KC, LANES), 0)
    iq = iq_ref[0].reshape(IDX_HEADS * rows, LANES)
    pick = jnp.where(lax.broadcasted_iota(jnp.int32, (SUBLANES, LANES), 1)
                     == lax.broadcasted_iota(jnp.int32, (SUBLANES, LANES), 0) + IDX_DIM, 1.0, 0.0)
    iw = lax.dot_general(pick, iw_ref[0], _NT, precision=HIGHEST, preferred_element_type=F32)
    ftop = float(topk)

    def score_chunk(c, carry):
        d = lax.dot_general(ik_ref[0, c], iq, _NT, preferred_element_type=F32)
        acc = jnp.zeros((KC, LANES), F32)
        for h in range(IDX_HEADS):
            acc = acc + jnp.maximum(d[:, h * LANES:(h + 1) * LANES], 0.0) * iw[h:h + 1, :]
        acc = jnp.where(c * KC + krow <= tq, acc, NEG)
        bits = lax.bitcast_convert_type(acc, jnp.int32)
        sc_ref[c] = bits ^ ((bits >> 31) & 0x7FFFFFFF)
        return carry

    lax.fori_loop(0, nck, score_chunk, 0)

    def count(pred):
        def body(c, acc):
            hit = jnp.where(pred(sc_ref[c], c * KC + krow), 1.0, 0.0)
            while hit.shape[0] > SUBLANES:
                half = hit.shape[0] // 2
                hit = hit[:half] + hit[half:]
            return acc + hit
        acc = lax.fori_loop(0, nck, body, jnp.zeros((SUBLANES, LANES), F32))
        return jnp.sum(acc, axis=0, keepdims=True)

    zero = jnp.zeros((1, LANES), jnp.int32)
    thr = jnp.where(count(lambda k, idx: k >= zero) >= ftop, zero, jnp.full_like(zero, INT_MIN))

    def bit_step(b, thr):
        cand = thr | lax.shift_left(jnp.int32(1), 30 - b)
        return jnp.where(count(lambda k, idx: k >= cand) >= ftop, cand, thr)

    thr = lax.fori_loop(0, 31, bit_step, thr)
    need = ftop - count(lambda k, idx: k > thr)
    n_eq = count(lambda k, idx: k == thr)
    nbits = max(1, int(math.ceil(math.log2(seq))))

    def tie_search(_):
        def jstep(b, jb):
            cand = jb | lax.shift_left(jnp.int32(1), nbits - 1 - b)
            c = count(lambda k, idx: jnp.where(k == thr, idx, seq + cand) < cand)
            return jnp.where(c < need, cand, jb)
        return lax.fori_loop(0, nbits, jstep, zero)

    jbound = lax.cond(jnp.max(n_eq - need) > 0.0, tie_search, lambda _: jnp.full_like(zero, seq), 0)

    q = q_ref[0].reshape(BRANCH_HEADS * rows, LANES)
    eye = jnp.where(lax.broadcasted_iota(jnp.int32, (LANES, LANES), 0)
                    == lax.broadcasted_iota(jnp.int32, (LANES, LANES), 1), 1.0, 0.0).astype(BF16)

    def attend(c, carry):
        k = sc_ref[c]
        idx = c * KC + krow
        tie_ok = jnp.where(k == thr, idx, seq + jbound) <= jbound
        sel = jnp.where(k > thr, idx, jnp.where(tie_ok, idx, seq + tq)) <= tq
        sel_t = jnp.where(sel, 1.0, 0.0).astype(BF16)
        hit = lax.dot_general(eye, sel_t, _NT, preferred_element_type=F32)
        bias = jnp.where(hit > 0.5, 0.0, -jnp.inf)
        s = lax.dot_general(q, kx_ref[0, c], _NT, preferred_element_type=F32)
        s = (s.reshape(BRANCH_HEADS, rows, KC) + bias[None]).reshape(BRANCH_HEADS * rows, KC)
        return _online_step(carry, s, ov_ref[0, c])

    out = _online_finish(lax.fori_loop(0, nck, attend, _online_init(BRANCH_HEADS * rows)))
    for p in range(BRANCH_HEADS // 2):
        o_ref[0, :, p * LANES:(p + 1) * LANES] = _pair_heads(
            out[2 * p * rows:(2 * p + 1) * rows], out[(2 * p + 1) * rows:(2 * p + 2) * rows]).astype(o_ref.dtype)


def _dsa(q, iq, ikw, ik, kx, ov, seq):
    b = q.shape[0]
    nq = seq // QBLK
    nc = seq // KC
    topk = min(DSA_TOPK, seq // 4)
    return pl.pallas_call(
        functools.partial(_dsa_kernel, seq=seq, topk=topk),
        grid=(b, nq),
        in_specs=[pl.BlockSpec((1, BRANCH_HEADS, QBLK, LANES), lambda bi, i: (bi, 0, i, 0)),
                  pl.BlockSpec((1, IDX_HEADS, QBLK, LANES), lambda bi, i: (bi, 0, i, 0)),
                  pl.BlockSpec((1, QBLK, LANES), lambda bi, i: (bi, i, 0)),
                  pl.BlockSpec((1, nc, KC, LANES), lambda bi, i: (bi, 0, 0, 0)),
                  pl.BlockSpec((1, nc, KC, LANES), lambda bi, i: (bi, 0, 0, 0)),
                  pl.BlockSpec((1, nc, KC, LANES), lambda bi, i: (bi, 0, 0, 0))],
        out_specs=pl.BlockSpec((1, QBLK, BRANCH_WIDTH), lambda bi, i: (bi, i, 0)),
        out_shape=jax.ShapeDtypeStruct((b, seq, BRANCH_WIDTH), BF16),
        scratch_shapes=[pltpu.VMEM((nc, KC, QBLK), jnp.int32)],
        compiler_params=_cparams(("parallel", "arbitrary")),
        name="dsa",
    )(q, iq, ikw, ik, kx, ov)


def _moba_kernel(q_ref, km_ref, kx_ref, ov_ref, o_ref, *, seq):
    own = pl.program_id(2)
    rows = MOBA_BLOCK
    jj = lax.broadcasted_iota(jnp.int32, (rows, LANES), 1)
    kcm = kx_ref.shape[3]
    per = kcm // MOBA_BLOCK
    blk_row = lax.broadcasted_iota(jnp.int32, (LANES, kcm), 0)
    blk_col = lax.broadcasted_iota(jnp.int32, (LANES, kcm), 1) // MOBA_BLOCK
    col = lax.broadcasted_iota(jnp.int32, (rows, kcm), 1)
    own0 = own * MOBA_BLOCK
    tq = own0 + lax.broadcasted_iota(jnp.int32, (rows, 1), 0)
    outs = []
    for hh in range(2):
        qf = q_ref[0, hh]
        gate = lax.dot_general(qf, km_ref[0, hh], _NT, precision=HIGHEST, preferred_element_type=F32)
        gate = jnp.where(jj < own, gate, NEG)
        picks = _top_lowest_index(gate, jj, MOBA_TOPK)
        picks = jnp.where(jj < own, picks, 0.0).astype(BF16)
        qb = qf.astype(BF16)

        def group(g, carry, hh=hh, picks=picks, qb=qb):
            expand = jnp.where(blk_row == blk_col + g * per, 1.0, 0.0).astype(BF16)
            hit = jnp.dot(picks, expand, preferred_element_type=F32) > 0.5
            kpos = g * kcm + col
            ok = jnp.where(hit, kpos, jnp.where(kpos >= own0, kpos, seq + tq)) <= tq
            s = lax.dot_general(qb, kx_ref[0, hh, g], _NT, preferred_element_type=F32)
            return _online_step(carry, jnp.where(ok, s, -jnp.inf), ov_ref[0, hh, g])

        outs.append(_online_finish(lax.fori_loop(0, own // per + 1, group, _online_init(rows))))
    o_ref[0] = _pair_heads(outs[0], outs[1]).astype(o_ref.dtype)


MOBA_KC = 4 * MOBA_BLOCK


def _moba(q, km, kx, ov, seq):
    b = q.shape[0]
    nb = seq // MOBA_BLOCK
    kcm = min(MOBA_KC, seq)
    ng = seq // kcm
    return pl.pallas_call(
        functools.partial(_moba_kernel, seq=seq),
        grid=(b, BRANCH_HEADS // 2, nb),
        in_specs=[pl.BlockSpec((1, 2, MOBA_BLOCK, LANES), lambda bi, p, i: (bi, p, i, 0)),
                  pl.BlockSpec((1, 2, LANES, LANES), lambda bi, p, i: (bi, p, 0, 0)),
                  pl.BlockSpec((1, 2, ng, kcm, LANES), lambda bi, p, i: (bi, p, 0, 0, 0)),
                  pl.BlockSpec((1, 2, ng, kcm, LANES), lambda bi, p, i: (bi, p, 0, 0, 0))],
        out_specs=pl.BlockSpec((1, MOBA_BLOCK, LANES), lambda bi, p, i: (bi, i, p)),
        out_shape=jax.ShapeDtypeStruct((b, seq, BRANCH_WIDTH), BF16),
        compiler_params=_cparams(("parallel", "parallel", "arbitrary")),
        name="moba",
    )(q, km, kx, ov)


def _nsa_kernel(q_ref, g_ref, cmp_ref, c2s_ref, kxs_ref, ovs_ref, kvw_ref,
                o_ref, *, seq):
    i = pl.program_id(1)
    qr = q_ref.shape[2]
    t0 = i * qr
    rows = BRANCH_HEADS * qr
    qf = q_ref[0].reshape(rows, LANES)
    qb = qf.astype(BF16)
    tpos = t0 + lax.broadcasted_iota(jnp.int32, (qr, 1), 0)
    tpos4 = _tile_rows(tpos, BRANCH_HEADS)

    ncp = cmp_ref.shape[1]
    q_hi, q_lo = _split_bf16(qf)
    c_hi, c_lo = _split_bf16(cmp_ref[0])
    s = (lax.dot_general(q_hi, c_hi, _NT, preferred_element_type=F32)
         + lax.dot_general(q_hi, c_lo, _NT, preferred_element_type=F32)
         + lax.dot_general(q_lo, c_hi, _NT, preferred_element_type=F32))
    cend = lax.broadcasted_iota(jnp.int32, (rows, ncp), 1) * CMP_STRIDE + (CMP_LEN - 1)
    ok = cend <= tpos4
    s = jnp.where(ok, s, NEG)
    e = jnp.where(ok, jnp.exp(s - jnp.max(s, axis=-1, keepdims=True)), 0.0)
    p = e / jnp.maximum(jnp.sum(e, axis=-1, keepdims=True), TINY)
    o_c = jnp.dot(p.astype(BF16), c_hi, preferred_element_type=F32)
    psum = p[0:qr]
    for h in range(1, BRANCH_HEADS):
        psum = psum + p[h * qr:(h + 1) * qr]
    p_hi, p_lo = _split_bf16(psum)
    c2s = c2s_ref[...].astype(BF16)
    imp = (jnp.dot(p_hi, c2s, preferred_element_type=F32)
           + jnp.dot(p_lo, c2s, preferred_element_type=F32))

    jj = lax.broadcasted_iota(jnp.int32, (qr, LANES), 1)
    cur = tpos // SLC_BLOCK
    forced = (jj == 0) | (jj == cur) | (jj == cur - 1)
    imp = jnp.where(forced, FORCE_SCORE, imp)
    imp = jnp.where(jj <= cur, imp, NEG)
    imp = jnp.where(jj < seq // SLC_BLOCK, imp, -jnp.inf)
    ntop = min(SLC_TOPK, seq // SLC_BLOCK)
    picks = _top_lowest_index(imp, jj, ntop)
    picks = jnp.where(jj <= cur, picks, 0.0).astype(BF16)

    nck = (t0 + qr + KC - 1) // KC
    blk_row = lax.broadcasted_iota(jnp.int32, (LANES, KC), 0)
    blk_col = lax.broadcasted_iota(jnp.int32, (LANES, KC), 1) // SLC_BLOCK
    lane = lax.broadcasted_iota(jnp.int32, (qr, KC), 1)

    def slc(c, carry):
        expand = jnp.where(blk_row == blk_col + c * (KC // SLC_BLOCK), 1.0, 0.0).astype(BF16)
        hit = jnp.dot(picks, expand, preferred_element_type=F32) > 0.5
        ok = jnp.where(hit, c * KC + lane, seq + tpos) <= tpos
        bias = jnp.where(ok, 0.0, -jnp.inf)
        sc = lax.dot_general(qb, kxs_ref[0, c], _NT, preferred_element_type=F32)
        sc = (sc.reshape(BRANCH_HEADS, qr, KC) + bias[None]).reshape(rows, KC)
        return _online_step(carry, sc, ovs_ref[0, c])

    o_s = _online_finish(lax.fori_loop(0, nck, slc, _online_init(rows)))

    nwin = (NSA_WINDOW + qr) // QBLK
    cb = jnp.maximum(i * (qr // QBLK) - NSA_WINDOW // QBLK, 0)
    sw = jnp.concatenate([lax.dot_general(qb, kvw_ref[0, cb + r], _NT, preferred_element_type=F32)
                          for r in range(nwin)], axis=1)
    kpos = cb * QBLK + lax.broadcasted_iota(jnp.int32, (rows, nwin * QBLK), 1)
    diff = tpos4 - kpos
    ok = jnp.where(diff >= 0, diff, NSA_WINDOW) < NSA_WINDOW
    sw = jnp.where(ok, sw, NEG)
    e = jnp.where(ok, jnp.exp(sw - jnp.max(sw, axis=-1, keepdims=True)), 0.0)
    pw = (e / jnp.maximum(jnp.sum(e, axis=-1, keepdims=True), TINY)).astype(BF16)
    o_w = jnp.zeros((rows, LANES), F32)
    for r in range(nwin):
        o_w = o_w + jnp.dot(pw[:, r * QBLK:(r + 1) * QBLK], kvw_ref[0, cb + r],
                            preferred_element_type=F32)

    g = g_ref[0]
    heads = []
    for h in range(BRANCH_HEADS):
        sl = slice(h * qr, (h + 1) * qr)
        heads.append(g[:, 3 * h:3 * h + 1] * o_c[sl] + g[:, 3 * h + 1:3 * h + 2] * o_s[sl]
                     + g[:, 3 * h + 2:3 * h + 3] * o_w[sl])
    for p in range(BRANCH_HEADS // 2):
        o_ref[0, :, p * LANES:(p + 1) * LANES] = _pair_heads(heads[2 * p], heads[2 * p + 1]).astype(o_ref.dtype)


NSA_QB = 256


def _nsa(q, g, cmp, c2s, kxs, ovs, kvw, seq):
    b = q.shape[0]
    qb = NSA_QB
    nq = seq // QBLK
    nc = seq // KC
    ncp = cmp.shape[1]
    return pl.pallas_call(
        functools.partial(_nsa_kernel, seq=seq),
        grid=(b, seq // qb),
        in_specs=[pl.BlockSpec((1, BRANCH_HEADS, qb, LANES), lambda bi, i: (bi, 0, i, 0)),
                  pl.BlockSpec((1, qb, LANES), lambda bi, i: (bi, i, 0)),
                  pl.BlockSpec((1, ncp, LANES), lambda bi, i: (bi, 0, 0)),
                  pl.BlockSpec((ncp, LANES), lambda bi, i: (0, 0)),
                  pl.BlockSpec((1, nc, KC, LANES), lambda bi, i: (bi, 0, 0, 0)),
                  pl.BlockSpec((1, nc, KC, LANES), lambda bi, i: (bi, 0, 0, 0)),
                  pl.BlockSpec((1, nq, QBLK, LANES), lambda bi, i: (bi, 0, 0, 0))],
        out_specs=pl.BlockSpec((1, qb, BRANCH_WIDTH), lambda bi, i: (bi, i, 0)),
        out_shape=jax.ShapeDtypeStruct((b, seq, BRANCH_WIDTH), BF16),
        compiler_params=_cparams(("parallel", "arbitrary")),
        name="nsa",
    )(q, g, cmp, c2s, kxs, ovs, kvw)


def _swa_kernel(sink_ref, q_ref, kv_ref, o_ref):
    i = pl.program_id(1)
    t0 = i * QBLK
    grp = BRANCH_HEADS // D_KV_HEADS
    rows = grp * QBLK
    cb = jnp.maximum(i - 1, 0)
    tpos = _tile_rows(t0 + lax.broadcasted_iota(jnp.int32, (QBLK, 1), 0), grp)
    kpos = cb * QBLK + lax.broadcasted_iota(jnp.int32, (rows, 2 * QBLK), 1)
    diff = tpos - kpos
    ok = jnp.where(diff >= 0, diff, SWA_WINDOW) < SWA_WINDOW
    for c in range(D_KV_HEADS):
        qb = q_ref[0, c * grp:(c + 1) * grp].reshape(rows, LANES)
        s = jnp.concatenate([lax.dot_general(qb, kv_ref[0, c, cb + r], _NT, preferred_element_type=F32)
                             for r in range(2)], axis=1)
        s = jnp.where(ok, s, NEG)
        sink = jnp.concatenate([jnp.full((QBLK, 1), sink_ref[c * grp + gi], F32) for gi in range(grp)],
                               axis=0)
        m = jnp.maximum(jnp.max(s, axis=-1, keepdims=True), sink)
        e = jnp.where(ok, jnp.exp(s - m), 0.0)
        den = jnp.sum(e, axis=-1, keepdims=True) + jnp.exp(sink - m)
        p = (e / jnp.maximum(den, TINY)).astype(BF16)
        o = jnp.zeros((rows, LANES), F32)
        for r in range(2):
            o = o + jnp.dot(p[:, r * QBLK:(r + 1) * QBLK], kv_ref[0, c, cb + r],
                            preferred_element_type=F32)
        o_ref[0, :, c * LANES:(c + 1) * LANES] = _pair_heads(o[:QBLK], o[QBLK:]).astype(o_ref.dtype)


def _swa(sink, q, kv, seq):
    b = q.shape[0]
    nq = seq // QBLK
    assert BRANCH_HEADS // D_KV_HEADS == 2
    return pl.pallas_call(
        _swa_kernel,
        grid=(b, nq),
        in_specs=[pl.BlockSpec(memory_space=pltpu.SMEM),
                  pl.BlockSpec((1, BRANCH_HEADS, QBLK, LANES), lambda bi, i: (bi, 0, i, 0)),
                  pl.BlockSpec((1, D_KV_HEADS, nq, QBLK, LANES), lambda bi, i: (bi, 0, 0, 0, 0))],
        out_specs=pl.BlockSpec((1, QBLK, BRANCH_WIDTH), lambda bi, i: (bi, i, 0)),
        out_shape=jax.ShapeDtypeStruct((b, seq, BRANCH_WIDTH), BF16),
        compiler_params=_cparams(("parallel", "arbitrary")),
        name="swa",
    )(sink, q, kv)


def _merge_kernel(x_ref, y0_ref, y1_ref, y2_ref, y3_ref, g0_ref, g1_ref, g2_ref, g3_ref, wb_ref, wo_ref,
                  o_ref):
    merged = jnp.zeros(x_ref.shape, F32)
    ys = (y0_ref, y1_ref, y2_ref, y3_ref)
    for n, g_ref in enumerate((g0_ref, g1_ref, g2_ref, g3_ref)):
        br = jnp.dot(ys[n][...], wb_ref[n], preferred_element_type=F32)
        gate = 1.0 / (1.0 + jnp.exp(-g_ref[...]))
        merged = merged + gate * br
    o_ref[...] = x_ref[...] + jnp.dot(merged.astype(BF16), wo_ref[...], preferred_element_type=F32)


def _merge(x2, ys, proj, wb, wo):
    n = x2.shape[0]
    tm = 256
    gb = C_GBR // D_MODEL
    gate_specs = [pl.BlockSpec((tm, D_MODEL), functools.partial(lambda i, k: (i, gb + k), k=k))
                  for k in range(N_BRANCH)]
    return pl.pallas_call(
        _merge_kernel,
        grid=(n // tm,),
        in_specs=[pl.BlockSpec((tm, D_MODEL), lambda i: (i, 0))]
                 + [pl.BlockSpec((tm, BRANCH_WIDTH), lambda i: (i, 0))] * N_BRANCH + gate_specs +
                 [pl.BlockSpec(wb.shape, lambda i: (0, 0, 0)),
                  pl.BlockSpec(wo.shape, lambda i: (0, 0))],
        out_specs=pl.BlockSpec((tm, D_MODEL), lambda i: (i, 0)),
        out_shape=jax.ShapeDtypeStruct((n, D_MODEL), F32),
        compiler_params=_cparams(("parallel",)),
        name="merge",
    )(x2, *ys, proj, proj, proj, proj, wb, wo)


FF_CHUNK = D_FF // 2
FF_HALO = 2 * SUBLANES


def _ffn_kernel(x_ref, xp_ref, g_ref, wg_ref, wv_ref, cw_ref, cb_ref, wd_ref, o_ref, h_ref, acc_ref,
                *, tiles_per_seq):
    i = pl.program_id(0)
    f = pl.program_id(1)
    tm = x_ref.shape[0]

    @pl.when(f == 0)
    def _():
        def norm(x):
            ms = jnp.mean(x * x, axis=-1, keepdims=True)
            return (x * lax.rsqrt(ms + EPS) * g_ref[...]).astype(BF16)
        h_ref[FF_HALO:, :] = norm(x_ref[...])
        h_ref[:FF_HALO, :] = norm(xp_ref[...])
        acc_ref[...] = jnp.zeros(acc_ref.shape, F32)

    h = h_ref[...]
    row = lax.broadcasted_iota(jnp.int32, (tm, 1), 0)
    first_tile = (i % tiles_per_seq) == 0

    def conv(w_ref, half):
        u = jnp.dot(h, w_ref[...], preferred_element_type=F32)
        cw = cw_ref[half, 0]
        out = u[FF_HALO:] * cw[2:3] + cb_ref[half, 0]
        for d in (1, 2):
            prev = pltpu.roll(u, d, 0)[FF_HALO:]
            prev = jnp.where(jnp.logical_and(first_tile, row < d), 0.0, prev)
            out = out + prev * cw[2 - d:3 - d]
        return out

    gate = conv(wg_ref, 0)
    val = conv(wv_ref, 1)
    act = gate / (1.0 + jnp.exp(-gate)) * val
    acc_ref[...] += jnp.dot(act.astype(BF16), wd_ref[...], preferred_element_type=F32)

    @pl.when(f == pl.num_programs(1) - 1)
    def _():
        o_ref[...] = x_ref[...] + acc_ref[...]


def _ffn(x2, g, wup, cw, cb, wd, seq):
    n = x2.shape[0]
    tm = 512
    nf = D_FF // FF_CHUNK
    hb = tm // FF_HALO
    return pl.pallas_call(
        functools.partial(_ffn_kernel, tiles_per_seq=seq // tm),
        grid=(n // tm, nf),
        in_specs=[pl.BlockSpec((tm, D_MODEL), lambda i, f: (i, 0)),
                  pl.BlockSpec((FF_HALO, D_MODEL), lambda i, f: (jnp.maximum(i * hb - 1, 0), 0)),
                  pl.BlockSpec((1, D_MODEL), lambda i, f: (0, 0)),
                  pl.BlockSpec((D_MODEL, FF_CHUNK), lambda i, f: (0, f)),
                  pl.BlockSpec((D_MODEL, FF_CHUNK), lambda i, f: (0, nf + f)),
                  pl.BlockSpec((2, 1, CONV_W, FF_CHUNK), lambda i, f: (0, f, 0, 0)),
                  pl.BlockSpec((2, 1, 1, FF_CHUNK), lambda i, f: (0, f, 0, 0)),
                  pl.BlockSpec((FF_CHUNK, D_MODEL), lambda i, f: (f, 0))],
        out_specs=pl.BlockSpec((tm, D_MODEL), lambda i, f: (i, 0)),
        out_shape=jax.ShapeDtypeStruct((n, D_MODEL), F32),
        scratch_shapes=[pltpu.VMEM((tm + FF_HALO, D_MODEL), BF16), pltpu.VMEM((tm, D_MODEL), F32)],
        compiler_params=_cparams(("parallel", "arbitrary")),
        name="convffn",
    )(x2, x2, g, wup, wup, cw, cb, wd)


def _chunk(a, c):
    lead = a.shape[:-2]
    s = a.shape[-2]
    return a.reshape(lead + (s // c, c, LANES))


def _rope_tables(seq):
    def tab(dim, reps):
        inv = ROPE_THETA ** (-jnp.arange(0, dim, 2, dtype=F32) / dim)
        ang = jnp.arange(seq, dtype=F32)[:, None] * inv[None, :]
        cos, sin = jnp.cos(ang), jnp.sin(ang)
        return (jnp.tile(jnp.concatenate([cos, cos], axis=1), (1, reps)),
                jnp.tile(jnp.concatenate([-sin, sin], axis=1), (1, reps)))
    c64, s64 = tab(HEAD_DIM, 256 // HEAD_DIM)
    c32, s32 = tab(IDX_DIM, 256 // IDX_DIM)
    a64 = np.kron(np.eye(256 // HEAD_DIM), np.full((HEAD_DIM, HEAD_DIM), 1.0 / HEAD_DIM)).astype(np.float32)
    a32 = np.kron(np.eye(LANES // IDX_DIM), np.full((IDX_DIM, IDX_DIM), 1.0 / IDX_DIM)).astype(np.float32)
    return c64, s64, c32, s32, jnp.asarray(a64), jnp.asarray(a32)


def _cmp_to_slc(seq, ncp):
    c_start = np.arange(ncp) * CMP_STRIDE
    s_start = np.arange(LANES) * SLC_BLOCK
    m = ((c_start[:, None] < s_start[None, :] + SLC_BLOCK) & (c_start[:, None] + CMP_LEN > s_start[None, :]))
    real = (np.arange(ncp) < (seq - CMP_LEN) // CMP_STRIDE + 1)[:, None] & (np.arange(LANES) < seq // SLC_BLOCK)[None, :]
    return jnp.asarray((m & real).astype(np.float32))


def _pad_w_in(w):
    z = lambda k: jnp.zeros((w.shape[0], k), w.dtype)
    return jnp.concatenate([w[:, :_O_AIW_END], z(C_BQ - _O_AIW_END), w[:, _O_AIW_END:_O_CG_END],
                            z(C_DQ - C_CG - 3 * BRANCH_HEADS), w[:, _O_CG_END:_O_GBR], z(C_GBR - W_MIX),
                            w[:, _O_GBR:]], axis=1)


def _gain_row(a_qk_g, a_lat_g, a_idx_k_g, b_qk_g, c_qk_g, d_qk_g):
    one = lambda k: jnp.ones((k,), F32)
    zero = lambda k: jnp.zeros((k,), F32)
    t4 = lambda g: jnp.tile(g, BRANCH_HEADS)
    row = jnp.concatenate([
        t4(a_qk_g[0]) * ATTN_SCALE, a_lat_g, one(256), a_idx_k_g, zero(LANES - IDX_DIM),
        t4(b_qk_g[0]) * ATTN_SCALE, t4(b_qk_g[1]), one(256),
        t4(c_qk_g[0]) * ATTN_SCALE, one(LANES), c_qk_g[2], one(HEAD_DIM), c_qk_g[3], one(HEAD_DIM), one(LANES),
        t4(d_qk_g[0]) * ATTN_SCALE, jnp.tile(d_qk_g[1], D_KV_HEADS), one(LANES)])
    return row[None, :]


def _mixers(proj, b, s, l, tabs, c2s, a_qk_g, a_lat_g, a_kv_up, a_idx_k_g, b_qk_g, c_qk_g, c_cmp_pe,
            c_cmp_w1, c_cmp_w2, d_qk_g, d_sink):
    gains = _gain_row(a_qk_g[l], a_lat_g[l], a_idx_k_g[l], b_qk_g[l], c_qk_g[l], d_qk_g[l])
    g2 = jnp.concatenate([a_qk_g[l][1], jnp.ones((HEAD_DIM,), F32)])[None, :]
    (qa, iq, ikw, ik, kxa, ova, qb, kxb, ovb, kmean, qc, kvc, kxs, ovs, kvw, gc, qd, kvd) = _prep(
        proj, gains, g2, a_kv_up[l], tabs, b, s)
    flat = lambda y: y.reshape(b * s, BRANCH_WIDTH)
    ya = flat(_dsa(qa, iq, ikw, _chunk(ik, KC), _chunk(kxa, KC), _chunk(ova, KC), s))
    nb = s // MOBA_BLOCK
    km = kmean.reshape(b, nb, BRANCH_HEADS, HEAD_DIM).transpose(0, 2, 1, 3)
    km = jnp.pad(km, ((0, 0), (0, 0), (0, LANES - nb), (0, LANES - HEAD_DIM)))
    kcm = min(MOBA_KC, s)
    yb = flat(_moba(qb, km, _chunk(kxb, kcm), _chunk(ovb, kcm), s))
    nrow = s // CMP_STRIDE
    hk = kvc[..., :HEAD_DIM].reshape(b, nrow, CMP_STRIDE * HEAD_DIM)
    hv = kvc[..., HEAD_DIM:].reshape(b, nrow, CMP_STRIDE * HEAD_DIM)
    w2 = c_cmp_w2[l]
    w2p = jnp.stack([jnp.pad(w2[0], ((0, 0), (0, HEAD_DIM))), jnp.pad(w2[1], ((0, 0), (HEAD_DIM, 0)))])
    pe = c_cmp_pe[l].reshape(2, 1, CMP_LEN * HEAD_DIM)
    gk = jnp.concatenate([c_qk_g[l][1], jnp.ones((HEAD_DIM,), F32)])[None, :]
    cmp = _compress(hk, hv, c_cmp_w1[l], pe, w2p, gk)
    ncp = c2s.shape[0]
    if ncp > nrow:
        cmp = jnp.pad(cmp, ((0, 0), (0, ncp - nrow), (0, 0)))
    yc = flat(_nsa(qc, gc, cmp, c2s, _chunk(kxs, KC), _chunk(ovs, KC), _chunk(kvw, QBLK), s))
    yd = flat(_swa(d_sink[l], qd, _chunk(kvd, QBLK), s))
    return ya, yb, yc, yd


def kernel(x, norm1_g, w_in, a_qk_g, a_lat_g, a_kv_up, a_idx_k_g, b_qk_g, c_qk_g, c_cmp_pe, c_cmp_w1,
           c_cmp_w2, d_qk_g, d_sink, w_branch, w_out, norm2_g, w_up, conv_w, conv_b, w_down):
    b, s, _ = x.shape
    depth = w_in.shape[0]
    assert s % KC == 0 and s >= NSA_WINDOW + NSA_QB and KC >= min(DSA_TOPK, s // 4)
    tabs = _rope_tables(s)
    ncp = max(LANES, s // CMP_STRIDE)
    c2s = _cmp_to_slc(s, ncp)
    x2 = x.reshape(b * s, D_MODEL)
    nf = D_FF // FF_CHUNK
    for l in range(depth):
        proj = _inproj(x2, norm1_g[l][None, :], _pad_w_in(w_in[l]).astype(BF16))
        ys = _mixers(proj, b, s, l, tabs, c2s, a_qk_g, a_lat_g, a_kv_up, a_idx_k_g, b_qk_g, c_qk_g,
                     c_cmp_pe, c_cmp_w1, c_cmp_w2, d_qk_g, d_sink)
        x2 = _merge(x2, ys, proj, w_branch[l].astype(BF16), w_out[l].astype(BF16))
        cw = conv_w[l].reshape(CONV_W, 2, nf, FF_CHUNK).transpose(1, 2, 0, 3)
        cb = conv_b[l].reshape(2, nf, 1, FF_CHUNK)
        x2 = _ffn(x2, norm2_g[l][None, :], w_up[l].astype(BF16), cw, cb, w_down[l].astype(BF16), s)
    return x2.reshape(b, s, D_MODEL)
```

```python
import functools
import math

import numpy as np
import jax
import jax.numpy as jnp
from jax import lax
from jax.experimental import pallas as pl
from jax.experimental.pallas import tpu as pltpu

F32 = jnp.float32
BF16 = jnp.bfloat16
HIGHEST = lax.Precision.HIGHEST

LANES = 128
SUBLANES = 8
VMEM_LIMIT = 48 * 1024 * 1024

D_MODEL = 1024
HEAD_DIM = 64
N_BRANCH = 4
BRANCH_HEADS = 4
BRANCH_WIDTH = BRANCH_HEADS * HEAD_DIM
ROPE_THETA = 10000.0
QBLK = 128
NEG = -1e30
TINY = 1e-30
EPS = 1e-6
ATTN_SCALE = HEAD_DIM ** -0.5
A_LATENT = 128
IDX_HEADS = 8
IDX_DIM = 32
DSA_TOPK = 256
MOBA_BLOCK = 256
MOBA_TOPK = 3
CMP_LEN = 32
CMP_STRIDE = 16
CMP_HIDDEN = 128
SLC_BLOCK = 64
SLC_TOPK = 16
NSA_WINDOW = 512
FORCE_SCORE = 1e9
NSA_KV = 6
SWA_WINDOW = 128
D_KV_HEADS = 2
D_FF = 256 * ((8 * D_MODEL // 3 + 255) // 256)
CONV_W = 3
IW_SCALE = IDX_HEADS ** -0.5 * IDX_DIM ** -0.5
LOG2E = math.log2(math.e)
INT_MIN = -2 ** 31

_O_AIW_END = 680
_O_CG_END = 2100
_O_GBR = 2612
C_AQ, C_ALAT, C_AIQ, C_AIKW = 0, 256, 384, 640
C_BQ, C_BK, C_BV = 768, 1024, 1280
C_CQ, C_CKV, C_CG = 1536, 1792, 2176
C_DQ, C_DK, C_DV = 2304, 2560, 2688
W_MIX = 2816
C_GBR = 3072
W_TOT = C_GBR + N_BRANCH * D_MODEL

KC = 512


def _cparams(sem):
    return pltpu.CompilerParams(dimension_semantics=sem, vmem_limit_bytes=VMEM_LIMIT)


def _inproj_kernel(x_ref, g_ref, w_ref, o_ref, h_ref):
    @pl.when(pl.program_id(1) == 0)
    def _():
        x = x_ref[...]
        ms = jnp.mean(x * x, axis=-1, keepdims=True)
        h_ref[...] = (x * lax.rsqrt(ms + EPS) * g_ref[...]).astype(h_ref.dtype)

    o_ref[...] = lax.dot_general(h_ref[...], w_ref[...], (((1,), (1,)), ((), ())),
                                 preferred_element_type=F32)


def _inproj(x2, g, wt):
    n = x2.shape[0]
    tm, tn = 512, 1024
    return pl.pallas_call(
        _inproj_kernel,
        grid=(n // tm, W_TOT // tn),
        in_specs=[pl.BlockSpec((tm, D_MODEL), lambda i, j: (i, 0)),
                  pl.BlockSpec((1, D_MODEL), lambda i, j: (0, 0)),
                  pl.BlockSpec((tn, D_MODEL), lambda i, j: (j, 0))],
        out_specs=pl.BlockSpec((tm, tn), lambda i, j: (i, j)),
        out_shape=jax.ShapeDtypeStruct((n, W_TOT), F32),
        scratch_shapes=[pltpu.VMEM((tm, D_MODEL), BF16)],
        compiler_params=_cparams(("parallel", "arbitrary")),
        name="inproj",
    )(x2, g, wt)


def _rope(x, cosw, sinw, half):
    w = x.shape[1]
    lane = lax.broadcasted_iota(jnp.int32, x.shape, 1)
    first = (lane % (2 * half)) < half
    rot = jnp.where(first, pltpu.roll(x, w - half, 1), pltpu.roll(x, half, 1))
    return x * cosw + rot * sinw


def _gnorm(x, avg, gain):
    ms = jnp.dot(x * x, avg, precision=HIGHEST, preferred_element_type=F32)
    return x * lax.rsqrt(ms + EPS) * gain


def _prep_kernel(p_ref, g_ref, g2_ref, kvup_ref, c64_ref, s64_ref, c32_ref, s32_ref, a64_ref, a32_ref,
                 qa_ref, iq_ref, ikw_ref, ik_ref, kxa_ref, ova_ref,
                 qb_ref, kxb_ref, ovb_ref, km_ref,
                 qc_ref, kvc_ref, kxs_ref, ovs_ref, kvw_ref, gc_ref,
                 qd_ref, kvd_ref):
    c64, s64 = c64_ref[...], s64_ref[...]
    c32, s32 = c32_ref[...], s32_ref[...]
    a64, a32 = a64_ref[...], a32_ref[...]
    c64h, s64h, a64h = c64[:, :LANES], s64[:, :LANES], a64[:LANES, :LANES]
    tm = p_ref.shape[0]
    lane1 = lax.broadcasted_iota(jnp.int32, (tm, LANES), 1)
    lo64 = lane1 < HEAD_DIM

    def seg(c, w):
        return p_ref[:, c:c + w], g_ref[:, c:c + w]

    def normrope256(c):
        x, g = seg(c, 256)
        return _rope(_gnorm(x, a64, g), c64, s64, HEAD_DIM // 2)

    def swap(x):
        return pltpu.roll(x, HEAD_DIM, 1)

    def put_heads(q, ref):
        for p in range(BRANCH_HEADS // 2):
            pair = q[:, p * LANES:(p + 1) * LANES]
            ref[0, 2 * p] = jnp.where(lo64, pair, 0.0).astype(ref.dtype)
            ref[0, 2 * p + 1] = jnp.where(lo64, swap(pair), 0.0).astype(ref.dtype)

    def put_kv(kv, kx_ref, ov_ref, idx):
        kx_ref[idx] = jnp.where(lo64, kv * LOG2E, kv).astype(BF16)
        ov_ref[idx] = jnp.where(lo64, 1.0, kv).astype(BF16)

    put_heads(normrope256(C_AQ), qa_ref)
    x, g = seg(C_ALAT, A_LATENT)
    ms = jnp.mean(x * x, axis=-1, keepdims=True)
    latn = x * lax.rsqrt(ms + EPS) * g
    kv = jnp.dot(latn, kvup_ref[...], precision=HIGHEST, preferred_element_type=F32)
    kr = _rope(_gnorm(kv, a64h, g2_ref[...]), c64h, s64h, HEAD_DIM // 2)
    put_kv(jnp.where(lo64, kr, kv), kxa_ref, ova_ref, 0)
    x, _ = seg(C_AIQ, 256)
    iq = _rope(x, c32, s32, IDX_DIM // 2)
    per = LANES // IDX_DIM
    for h in range(IDX_HEADS):
        tile = iq[:, (h // per) * LANES:(h // per + 1) * LANES]
        if h % per:
            tile = pltpu.roll(tile, LANES - (h % per) * IDX_DIM, 1)
        iq_ref[0, h] = jnp.where(lane1 < IDX_DIM, tile, 0.0).astype(BF16)
    x, g = seg(C_AIKW, LANES)
    xr = _rope(_gnorm(x, a32, g), c32[:, :LANES], s32[:, :LANES], IDX_DIM // 2)
    ikw_ref[0] = jnp.where(lane1 < IDX_DIM, xr, x * IW_SCALE)
    ik_ref[0] = jnp.where(lane1 < IDX_DIM, xr, 0.0).astype(BF16)
    put_heads(normrope256(C_BQ), qb_ref)
    kb = normrope256(C_BK)
    km_ref[0] = jnp.mean(kb, axis=0, keepdims=True)
    for p in range(BRANCH_HEADS // 2):
        kp = kb[:, p * LANES:(p + 1) * LANES]
        vp = p_ref[:, C_BV + p * LANES:C_BV + (p + 1) * LANES]
        put_kv(jnp.where(lo64, kp, swap(vp)), kxb_ref, ovb_ref, (0, 2 * p))
        put_kv(jnp.where(lo64, swap(kp), vp), kxb_ref, ovb_ref, (0, 2 * p + 1))
    put_heads(normrope256(C_CQ), qc_ref)
    for r in range(3):
        c = C_CKV + r * LANES
        x, g = seg(c, LANES)
        xn = x if r == 0 else _gnorm(x, a64h, g)
        kv = jnp.where(lo64, _rope(xn, c64h, s64h, HEAD_DIM // 2), x)
        if r == 0:
            kvc_ref[0] = kv
        elif r == 1:
            put_kv(kv, kxs_ref, ovs_ref, 0)
        else:
            kvw_ref[0] = kv.astype(BF16)
    x, _ = seg(C_CG, LANES)
    gc_ref[0] = 1.0 / (1.0 + jnp.exp(-x))
    put_heads(normrope256(C_DQ), qd_ref)
    x, g = seg(C_DK, LANES)
    kd = _rope(_gnorm(x, a64h, g), c64h, s64h, HEAD_DIM // 2)
    vd = p_ref[:, C_DV:C_DV + LANES]
    kvd_ref[0, 0] = jnp.where(lo64, kd, swap(vd)).astype(BF16)
    kvd_ref[0, 1] = jnp.where(lo64, swap(kd), vd).astype(BF16)


def _prep(proj, gains, g2, kvup, tabs, b, seq):
    n = proj.shape[0]
    tm = MOBA_BLOCK
    nt = seq // tm
    c64, s64, c32, s32, a64, a32 = tabs
    tab_spec = pl.BlockSpec((tm, 256), lambda i: (i % nt, 0))
    full = lambda a: pl.BlockSpec(a.shape, lambda i: (0,) * a.ndim)
    tok = lambda dt: (jax.ShapeDtypeStruct((b, seq, LANES), dt),
                      pl.BlockSpec((1, tm, LANES), lambda i: (i // nt, i % nt, 0)))
    hd = lambda nh, dt: (jax.ShapeDtypeStruct((b, nh, seq, LANES), dt),
                         pl.BlockSpec((1, nh, tm, LANES), lambda i: (i // nt, 0, i % nt, 0)))
    outs = [hd(BRANCH_HEADS, BF16), hd(IDX_HEADS, BF16), tok(F32), tok(BF16), tok(BF16), tok(BF16),
            hd(BRANCH_HEADS, F32), hd(BRANCH_HEADS, BF16), hd(BRANCH_HEADS, BF16),
            (jax.ShapeDtypeStruct((n // tm, 1, 256), F32), pl.BlockSpec((1, 1, 256), lambda i: (i, 0, 0))),
            hd(BRANCH_HEADS, F32), tok(F32), tok(BF16), tok(BF16), tok(BF16), tok(F32),
            hd(BRANCH_HEADS, BF16), hd(D_KV_HEADS, BF16)]
    return pl.pallas_call(
        _prep_kernel,
        grid=(n // tm,),
        in_specs=[pl.BlockSpec((tm, W_MIX), lambda i: (i, 0)), full(gains), full(g2), full(kvup),
                  tab_spec, tab_spec, tab_spec, tab_spec, full(a64), full(a32)],
        out_specs=[o[1] for o in outs],
        out_shape=[o[0] for o in outs],
        compiler_params=_cparams(("parallel",)),
        name="prep",
    )(proj, gains, g2, kvup, c64, s64, c32, s32, a64, a32)


def _gelu_tanh(x):
    return 0.5 * x * (1.0 + jnp.tanh(math.sqrt(2.0 / math.pi) * (x + 0.044715 * (x * x * x))))


def _cmp_kernel(x_ref, w1_ref, pe_ref, w2_ref, g_ref, o_ref):
    nrow = o_ref.shape[1]
    u = jnp.zeros((nrow, 2 * CMP_HIDDEN), F32)
    v = jnp.zeros((nrow, 2 * CMP_HIDDEN), F32)
    for l in range(CMP_STRIDE):
        xl = x_ref[0, pl.ds(l, nrow, stride=CMP_STRIDE), :]
        u = u + jnp.dot(xl + pe_ref[l:l + 1, :], w1_ref[l], precision=HIGHEST, preferred_element_type=F32)
        v = v + jnp.dot(xl + pe_ref[CMP_STRIDE + l:CMP_STRIDE + l + 1, :], w1_ref[CMP_STRIDE + l],
                        precision=HIGHEST, preferred_element_type=F32)
    pre = u + pltpu.roll(v, nrow - 1, 0)
    out = jnp.dot(_gelu_tanh(pre), w2_ref[...], precision=HIGHEST, preferred_element_type=F32)
    lane = lax.broadcasted_iota(jnp.int32, out.shape, 1)
    lo = lane < HEAD_DIM
    ms = jnp.sum(jnp.where(lo, out * out, 0.0), axis=-1, keepdims=True) * (1.0 / HEAD_DIM)
    o_ref[0] = jnp.where(lo, out * lax.rsqrt(ms + EPS) * g_ref[...], out)


def _compress(kvc, w1, pe, w2, gk):
    b, seq, _ = kvc.shape
    nrow = seq // CMP_STRIDE
    full = lambda a: pl.BlockSpec(a.shape, lambda i: (0,) * a.ndim)
    return pl.pallas_call(
        _cmp_kernel,
        grid=(b,),
        in_specs=[pl.BlockSpec((1, seq, LANES), lambda i: (i, 0, 0)), full(w1), full(pe), full(w2), full(gk)],
        out_specs=pl.BlockSpec((1, nrow, LANES), lambda i: (i, 0, 0)),
        out_shape=jax.ShapeDtypeStruct((b, nrow, LANES), F32),
        compiler_params=_cparams(("parallel",)),
        name="nsa_compress",
    )(kvc, w1, pe, w2, gk)


def _tile_rows(x, n):
    return jnp.concatenate([x] * n, axis=0)


def _online_step(carry, s, ov):
    m, acc = carry
    m_new = jnp.maximum(m, jnp.max(s, axis=-1, keepdims=True))
    e = jnp.exp2(s - m_new)
    acc = jnp.exp2(m - m_new) * acc + jnp.dot(e.astype(BF16), ov, preferred_element_type=F32)
    return m_new, acc


def _online_init(rows):
    return (jnp.full((rows, 1), NEG, F32), jnp.zeros((rows, LANES), F32))


def _online_finish(carry):
    _, acc = carry
    den = jnp.maximum(pltpu.roll(acc, HEAD_DIM, 1), TINY)
    lane = lax.broadcasted_iota(jnp.int32, acc.shape, 1)
    return jnp.where(lane >= HEAD_DIM, acc / den, 0.0)


def _top_lowest_index(work, jj, k):
    picks = jnp.zeros(work.shape, F32)
    jf = jj.astype(F32)
    big = float(work.shape[1])
    for _ in range(k):
        mx = jnp.max(work, axis=-1, keepdims=True)
        idx = jnp.min(jnp.where(work == mx, jf, big), axis=-1, keepdims=True)
        pick = jf == idx
        picks = jnp.where(pick, 1.0, picks)
        work = jnp.where(pick, -jnp.inf, work)
    return picks


def _split_bf16(a):
    hi = a.astype(BF16)
    return hi, (a - hi.astype(F32)).astype(BF16)


_NT = (((1,), (1,)), ((), ()))


def _pair_heads(o_even, o_odd):
    lane = lax.broadcasted_iota(jnp.int32, o_even.shape, 1)
    return jnp.where(lane < HEAD_DIM, pltpu.roll(o_even, HEAD_DIM, 1), o_odd)


def _dsa_kernel(q_ref, iq_ref, iw_ref, ik_ref, kx_ref, ov_ref, o_ref, sc_ref, *, seq, topk):
    i = pl.program_id(1)
    rows = QBLK
    t0 = i * rows
    nck = (t0 + rows + KC - 1) // KC
    tq = t0 + lax.broadcasted_iota(jnp.int32, (1, LANES), 1)
    krow = lax.broadcasted_iota(jnp.int32, (KC, LANES), 0)
    iq = iq_ref[0].reshape(IDX_HEADS * rows, LANES)
    pick = jnp.where(lax.broadcasted_iota(jnp.int32, (SUBLANES, LANES), 1)
                     == lax.broadcasted_iota(jnp.int32, (SUBLANES, LANES), 0) + IDX_DIM, 1.0, 0.0)
    iw = lax.dot_general(pick, iw_ref[0], _NT, precision=HIGHEST, preferred_element_type=F32)
    ftop = float(topk)

    def score_chunk(c, carry):
        d = lax.dot_general(ik_ref[0, c], iq, _NT, preferred_element_type=F32)
        acc = jnp.zeros((KC, LANES), F32)
        for h in range(IDX_HEADS):
            acc = acc + jnp.maximum(d[:, h * LANES:(h + 1) * LANES], 0.0) * iw[h:h + 1, :]
        acc = jnp.where(c * KC + krow <= tq, acc, NEG)
        bits = lax.bitcast_convert_type(acc, jnp.int32)
        sc_ref[c] = bits ^ ((bits >> 31) & 0x7FFFFFFF)
        return carry

    lax.fori_loop(0, nck, score_chunk, 0)

    def count(pred):
        def body(c, acc):
            hit = jnp.where(pred(sc_ref[c], c * KC + krow), 1.0, 0.0)
            while hit.shape[0] > SUBLANES:
                half = hit.shape[0] // 2
                hit = hit[:half] + hit[half:]
            return acc + hit
        acc = lax.fori_loop(0, nck, body, jnp.zeros((SUBLANES, LANES), F32))
        return jnp.sum(acc, axis=0, keepdims=True)

    zero = jnp.zeros((1, LANES), jnp.int32)
    thr = jnp.where(count(lambda k, idx: k >= zero) >= ftop, zero, jnp.full_like(zero, INT_MIN))

    def bit_step(b, thr):
        cand = thr | lax.shift_left(jnp.int32(1), 30 - b)
        return jnp.where(count(lambda k, idx: k >= cand) >= ftop, cand, thr)

    thr = lax.fori_loop(0, 31, bit_step, thr)
    need = ftop - count(lambda k, idx: k > thr)
    n_eq = count(lambda k, idx: k == thr)
    nbits = max(1, int(math.ceil(math.log2(seq))))

    def tie_search(_):
        def jstep(b, jb):
            cand = jb | lax.shift_left(jnp.int32(1), nbits - 1 - b)
            c = count(lambda k, idx: jnp.where(k == thr, idx, seq + cand) < cand)
            return jnp.where(c < need, cand, jb)
        return lax.fori_loop(0, nbits, jstep, zero)

    jbound = lax.cond(jnp.max(n_eq - need) > 0.0, tie_search, lambda _: jnp.full_like(zero, seq), 0)

    q = q_ref[0].reshape(BRANCH_HEADS * rows, LANES)
    eye = jnp.where(lax.broadcasted_iota(jnp.int32, (LANES, LANES), 0)
                    == lax.broadcasted_iota(jnp.int32, (LANES, LANES), 1), 1.0, 0.0).astype(BF16)

    def attend(c, carry):
        k = sc_ref[c]
        idx = c * KC + krow
        tie_ok = jnp.where(k == thr, idx, seq + jbound) <= jbound
        sel = jnp.where(k > thr, idx, jnp.where(tie_ok, idx, seq + tq)) <= tq
        sel_t = jnp.where(sel, 1.0, 0.0).astype(BF16)
        hit = lax.dot_general(eye, sel_t, _NT, preferred_element_type=F32)
        bias = jnp.where(hit > 0.5, 0.0, -jnp.inf)
        s = lax.dot_general(q, kx_ref[0, c], _NT, preferred_element_type=F32)
        s = (s.reshape(BRANCH_HEADS, rows, KC) + bias[None]).reshape(BRANCH_HEADS * rows, KC)
        return _online_step(carry, s, ov_ref[0, c])

    out = _online_finish(lax.fori_loop(0, nck, attend, _online_init(BRANCH_HEADS * rows)))
    for p in range(BRANCH_HEADS // 2):
        o_ref[0, :, p * LANES:(p + 1) * LANES] = _pair_heads(
            out[2 * p * rows:(2 * p + 1) * rows], out[(2 * p + 1) * rows:(2 * p + 2) * rows]).astype(o_ref.dtype)


def _dsa(q, iq, ikw, ik, kx, ov, seq):
    b = q.shape[0]
    nq = seq // QBLK
    nc = seq // KC
    topk = min(DSA_TOPK, seq // 4)
    return pl.pallas_call(
        functools.partial(_dsa_kernel, seq=seq, topk=topk),
        grid=(b, nq),
        in_specs=[pl.BlockSpec((1, BRANCH_HEADS, QBLK, LANES), lambda bi, i: (bi, 0, i, 0)),
                  pl.BlockSpec((1, IDX_HEADS, QBLK, LANES), lambda bi, i: (bi, 0, i, 0)),
                  pl.BlockSpec((1, QBLK, LANES), lambda bi, i: (bi, i, 0)),
                  pl.BlockSpec((1, nc, KC, LANES), lambda bi, i: (bi, 0, 0, 0)),
                  pl.BlockSpec((1, nc, KC, LANES), lambda bi, i: (bi, 0, 0, 0)),
                  pl.BlockSpec((1, nc, KC, LANES), lambda bi, i: (bi, 0, 0, 0))],
        out_specs=pl.BlockSpec((1, QBLK, BRANCH_WIDTH), lambda bi, i: (bi, i, 0)),
        out_shape=jax.ShapeDtypeStruct((b, seq, BRANCH_WIDTH), BF16),
        scratch_shapes=[pltpu.VMEM((nc, KC, QBLK), jnp.int32)],
        compiler_params=_cparams(("parallel", "arbitrary")),
        name="dsa",
    )(q, iq, ikw, ik, kx, ov)


def _moba_kernel(q_ref, km_ref, kx_ref, ov_ref, o_ref, *, seq):
    own = pl.program_id(2)
    rows = MOBA_BLOCK
    jj = lax.broadcasted_iota(jnp.int32, (rows, LANES), 1)
    kcm = kx_ref.shape[3]
    per = kcm // MOBA_BLOCK
    blk_row = lax.broadcasted_iota(jnp.int32, (LANES, kcm), 0)
    blk_col = lax.broadcasted_iota(jnp.int32, (LANES, kcm), 1) // MOBA_BLOCK
    col = lax.broadcasted_iota(jnp.int32, (rows, kcm), 1)
    own0 = own * MOBA_BLOCK
    tq = own0 + lax.broadcasted_iota(jnp.int32, (rows, 1), 0)
    outs = []
    for hh in range(2):
        qf = q_ref[0, hh]
        gate = lax.dot_general(qf, km_ref[0, hh], _NT, precision=HIGHEST, preferred_element_type=F32)
        gate = jnp.where(jj < own, gate, NEG)
        picks = _top_lowest_index(gate, jj, MOBA_TOPK)
        picks = jnp.where(jj < own, picks, 0.0).astype(BF16)
        qb = qf.astype(BF16)

        def group(g, carry, hh=hh, picks=picks, qb=qb):
            expand = jnp.where(blk_row == blk_col + g * per, 1.0, 0.0).astype(BF16)
            hit = jnp.dot(picks, expand, preferred_element_type=F32) > 0.5
            kpos = g * kcm + col
            ok = jnp.where(hit, kpos, jnp.where(kpos >= own0, kpos, seq + tq)) <= tq
            s = lax.dot_general(qb, kx_ref[0, hh, g], _NT, preferred_element_type=F32)
            return _online_step(carry, jnp.where(ok, s, -jnp.inf), ov_ref[0, hh, g])

        outs.append(_online_finish(lax.fori_loop(0, own // per + 1, group, _online_init(rows))))
    o_ref[0] = _pair_heads(outs[0], outs[1]).astype(o_ref.dtype)


MOBA_KC = 4 * MOBA_BLOCK


def _moba(q, km, kx, ov, seq):
    b = q.shape[0]
    nb = seq // MOBA_BLOCK
    kcm = min(MOBA_KC, seq)
    ng = seq // kcm
    return pl.pallas_call(
        functools.partial(_moba_kernel, seq=seq),
        grid=(b, BRANCH_HEADS // 2, nb),
        in_specs=[pl.BlockSpec((1, 2, MOBA_BLOCK, LANES), lambda bi, p, i: (bi, p, i, 0)),
                  pl.BlockSpec((1, 2, LANES, LANES), lambda bi, p, i: (bi, p, 0, 0)),
                  pl.BlockSpec((1, 2, ng, kcm, LANES), lambda bi, p, i: (bi, p, 0, 0, 0)),
                  pl.BlockSpec((1, 2, ng, kcm, LANES), lambda bi, p, i: (bi, p, 0, 0, 0))],
        out_specs=pl.BlockSpec((1, MOBA_BLOCK, LANES), lambda bi, p, i: (bi, i, p)),
        out_shape=jax.ShapeDtypeStruct((b, seq, BRANCH_WIDTH), BF16),
        compiler_params=_cparams(("parallel", "parallel", "arbitrary")),
        name="moba",
    )(q, km, kx, ov)


def _nsa_kernel(q_ref, g_ref, cmp_ref, c2s_ref, kxs_ref, ovs_ref, kvw_ref,
                o_ref, *, seq):
    i = pl.program_id(1)
    qr = q_ref.shape[2]
    t0 = i * qr
    rows = BRANCH_HEADS * qr
    qf = q_ref[0].reshape(rows, LANES)
    qb = qf.astype(BF16)
    tpos = t0 + lax.broadcasted_iota(jnp.int32, (qr, 1), 0)
    tpos4 = _tile_rows(tpos, BRANCH_HEADS)

    ncp = cmp_ref.shape[1]
    q_hi, q_lo = _split_bf16(qf)
    c_hi, c_lo = _split_bf16(cmp_ref[0])
    s = (lax.dot_general(q_hi, c_hi, _NT, preferred_element_type=F32)
         + lax.dot_general(q_hi, c_lo, _NT, preferred_element_type=F32)
         + lax.dot_general(q_lo, c_hi, _NT, preferred_element_type=F32))
    cend = lax.broadcasted_iota(jnp.int32, (rows, ncp), 1) * CMP_STRIDE + (CMP_LEN - 1)
    ok = cend <= tpos4
    s = jnp.where(ok, s, NEG)
    e = jnp.where(ok, jnp.exp(s - jnp.max(s, axis=-1, keepdims=True)), 0.0)
    p = e / jnp.maximum(jnp.sum(e, axis=-1, keepdims=True), TINY)
    o_c = jnp.dot(p.astype(BF16), c_hi, preferred_element_type=F32)
    psum = p[0:qr]
    for h in range(1, BRANCH_HEADS):
        psum = psum + p[h * qr:(h + 1) * qr]
    p_hi, p_lo = _split_bf16(psum)
    c2s = c2s_ref[...].astype(BF16)
    imp = (jnp.dot(p_hi, c2s, preferred_element_type=F32)
           + jnp.dot(p_lo, c2s, preferred_element_type=F32))

    jj = lax.broadcasted_iota(jnp.int32, (qr, LANES), 1)
    cur = tpos // SLC_BLOCK
    forced = (jj == 0) | (jj == cur) | (jj == cur - 1)
    imp = jnp.where(forced, FORCE_SCORE, imp)
    imp = jnp.where(jj <= cur, imp, NEG)
    imp = jnp.where(jj < seq // SLC_BLOCK, imp, -jnp.inf)
    ntop = min(SLC_TOPK, seq // SLC_BLOCK)
    picks = _top_lowest_index(imp, jj, ntop)
    picks = jnp.where(jj <= cur, picks, 0.0).astype(BF16)

    nck = (t0 + qr + KC - 1) // KC
    blk_row = lax.broadcasted_iota(jnp.int32, (LANES, KC), 0)
    blk_col = lax.broadcasted_iota(jnp.int32, (LANES, KC), 1) // SLC_BLOCK
    lane = lax.broadcasted_iota(jnp.int32, (qr, KC), 1)

    def slc(c, carry):
        expand = jnp.where(blk_row == blk_col + c * (KC // SLC_BLOCK), 1.0, 0.0).astype(BF16)
        hit = jnp.dot(picks, expand, preferred_element_type=F32) > 0.5
        ok = jnp.where(hit, c * KC + lane, seq + tpos) <= tpos
        bias = jnp.where(ok, 0.0, -jnp.inf)
        sc = lax.dot_general(qb, kxs_ref[0, c], _NT, preferred_element_type=F32)
        sc = (sc.reshape(BRANCH_HEADS, qr, KC) + bias[None]).reshape(rows, KC)
        return _online_step(carry, sc, ovs_ref[0, c])

    o_s = _online_finish(lax.fori_loop(0, nck, slc, _online_init(rows)))

    nwin = (NSA_WINDOW + qr) // QBLK
    cb = jnp.maximum(i * (qr // QBLK) - NSA_WINDOW // QBLK, 0)
    sw = jnp.concatenate([lax.dot_general(qb, kvw_ref[0, cb + r], _NT, preferred_element_type=F32)
                          for r in range(nwin)], axis=1)
    kpos = cb * QBLK + lax.broadcasted_iota(jnp.int32, (rows, nwin * QBLK), 1)
    diff = tpos4 - kpos
    ok = jnp.where(diff >= 0, diff, NSA_WINDOW) < NSA_WINDOW
    sw = jnp.where(ok, sw, NEG)
    e = jnp.where(ok, jnp.exp(sw - jnp.max(sw, axis=-1, keepdims=True)), 0.0)
    pw = (e / jnp.maximum(jnp.sum(e, axis=-1, keepdims=True), TINY)).astype(BF16)
    o_w = jnp.zeros((rows, LANES), F32)
    for r in range(nwin):
        o_w = o_w + jnp.dot(pw[:, r * QBLK:(r + 1) * QBLK], kvw_ref[0, cb + r],
                            preferred_element_type=F32)

    g = g_ref[0]
    heads = []
    for h in range(BRANCH_HEADS):
        sl = slice(h * qr, (h + 1) * qr)
        heads.append(g[:, 3 * h:3 * h + 1] * o_c[sl] + g[:, 3 * h + 1:3 * h + 2] * o_s[sl]
                     + g[:, 3 * h + 2:3 * h + 3] * o_w[sl])
    for p in range(BRANCH_HEADS // 2):
        o_ref[0, :, p * LANES:(p + 1) * LANES] = _pair_heads(heads[2 * p], heads[2 * p + 1]).astype(o_ref.dtype)


NSA_QB = 256


def _nsa(q, g, cmp, c2s, kxs, ovs, kvw, seq):
    b = q.shape[0]
    qb = NSA_QB
    nq = seq // QBLK
    nc = seq // KC
    ncp = cmp.shape[1]
    return pl.pallas_call(
        functools.partial(_nsa_kernel, seq=seq),
        grid=(b, seq // qb),
        in_specs=[pl.BlockSpec((1, BRANCH_HEADS, qb, LANES), lambda bi, i: (bi, 0, i, 0)),
                  pl.BlockSpec((1, qb, LANES), lambda bi, i: (bi, i, 0)),
                  pl.BlockSpec((1, ncp, LANES), lambda bi, i: (bi, 0, 0)),
                  pl.BlockSpec((ncp, LANES), lambda bi, i: (0, 0)),
                  pl.BlockSpec((1, nc, KC, LANES), lambda bi, i: (bi, 0, 0, 0)),
                  pl.BlockSpec((1, nc, KC, LANES), lambda bi, i: (bi, 0, 0, 0)),
                  pl.BlockSpec((1, nq, QBLK, LANES), lambda bi, i: (bi, 0, 0, 0))],
        out_specs=pl.BlockSpec((1, qb, BRANCH_WIDTH), lambda bi, i: (bi, i, 0)),
        out_shape=jax.ShapeDtypeStruct((b, seq, BRANCH_WIDTH), BF16),
        compiler_params=_cparams(("parallel", "arbitrary")),
        name="nsa",
    )(q, g, cmp, c2s, kxs, ovs, kvw)


def _swa_kernel(sink_ref, q_ref, kv_ref, o_ref):
    i = pl.program_id(1)
    t0 = i * QBLK
    grp = BRANCH_HEADS // D_KV_HEADS
    rows = grp * QBLK
    cb = jnp.maximum(i - 1, 0)
    tpos = _tile_rows(t0 + lax.broadcasted_iota(jnp.int32, (QBLK, 1), 0), grp)
    kpos = cb * QBLK + lax.broadcasted_iota(jnp.int32, (rows, 2 * QBLK), 1)
    diff = tpos - kpos
    ok = jnp.where(diff >= 0, diff, SWA_WINDOW) < SWA_WINDOW
    for c in range(D_KV_HEADS):
        qb = q_ref[0, c * grp:(c + 1) * grp].reshape(rows, LANES)
        s = jnp.concatenate([lax.dot_general(qb, kv_ref[0, c, cb + r], _NT, preferred_element_type=F32)
                             for r in range(2)], axis=1)
        s = jnp.where(ok, s, NEG)
        sink = jnp.concatenate([jnp.full((QBLK, 1), sink_ref[c * grp + gi], F32) for gi in range(grp)],
                               axis=0)
        m = jnp.maximum(jnp.max(s, axis=-1, keepdims=True), sink)
        e = jnp.where(ok, jnp.exp(s - m), 0.0)
        den = jnp.sum(e, axis=-1, keepdims=True) + jnp.exp(sink - m)
        p = (e / jnp.maximum(den, TINY)).astype(BF16)
        o = jnp.zeros((rows, LANES), F32)
        for r in range(2):
            o = o + jnp.dot(p[:, r * QBLK:(r + 1) * QBLK], kv_ref[0, c, cb + r],
                            preferred_element_type=F32)
        o_ref[0, :, c * LANES:(c + 1) * LANES] = _pair_heads(o[:QBLK], o[QBLK:]).astype(o_ref.dtype)


def _swa(sink, q, kv, seq):
    b = q.shape[0]
    nq = seq // QBLK
    assert BRANCH_HEADS // D_KV_HEADS == 2
    return pl.pallas_call(
        _swa_kernel,
        grid=(b, nq),
        in_specs=[pl.BlockSpec(memory_space=pltpu.SMEM),
                  pl.BlockSpec((1, BRANCH_HEADS, QBLK, LANES), lambda bi, i: (bi, 0, i, 0)),
                  pl.BlockSpec((1, D_KV_HEADS, nq, QBLK, LANES), lambda bi, i: (bi, 0, 0, 0, 0))],
        out_specs=pl.BlockSpec((1, QBLK, BRANCH_WIDTH), lambda bi, i: (bi, i, 0)),
        out_shape=jax.ShapeDtypeStruct((b, seq, BRANCH_WIDTH), BF16),
        compiler_params=_cparams(("parallel", "arbitrary")),
        name="swa",
    )(sink, q, kv)


def _merge_kernel(x_ref, y0_ref, y1_ref, y2_ref, y3_ref, g0_ref, g1_ref, g2_ref, g3_ref, wb_ref, wo_ref,
                  o_ref):
    merged = jnp.zeros(x_ref.shape, F32)
    ys = (y0_ref, y1_ref, y2_ref, y3_ref)
    for n, g_ref in enumerate((g0_ref, g1_ref, g2_ref, g3_ref)):
        br = jnp.dot(ys[n][...], wb_ref[n], preferred_element_type=F32)
        gate = 1.0 / (1.0 + jnp.exp(-g_ref[...]))
        merged = merged + gate * br
    o_ref[...] = x_ref[...] + jnp.dot(merged.astype(BF16), wo_ref[...], preferred_element_type=F32)


def _merge(x2, ys, proj, wb, wo):
    n = x2.shape[0]
    tm = 256
    gb = C_GBR // D_MODEL
    gate_specs = [pl.BlockSpec((tm, D_MODEL), functools.partial(lambda i, k: (i, gb + k), k=k))
                  for k in range(N_BRANCH)]
    return pl.pallas_call(
        _merge_kernel,
        grid=(n // tm,),
        in_specs=[pl.BlockSpec((tm, D_MODEL), lambda i: (i, 0))]
                 + [pl.BlockSpec((tm, BRANCH_WIDTH), lambda i: (i, 0))] * N_BRANCH + gate_specs +
                 [pl.BlockSpec(wb.shape, lambda i: (0, 0, 0)),
                  pl.BlockSpec(wo.shape, lambda i: (0, 0))],
        out_specs=pl.BlockSpec((tm, D_MODEL), lambda i: (i, 0)),
        out_shape=jax.ShapeDtypeStruct((n, D_MODEL), F32),
        compiler_params=_cparams(("parallel",)),
        name="merge",
    )(x2, *ys, proj, proj, proj, proj, wb, wo)


FF_CHUNK = D_FF // 2
FF_HALO = 2 * SUBLANES


def _ffn_kernel(x_ref, xp_ref, g_ref, wg_ref, wv_ref, cw_ref, cb_ref, wd_ref, o_ref, h_ref, acc_ref,
                *, tiles_per_seq):
    i = pl.program_id(0)
    f = pl.program_id(1)
    tm = x_ref.shape[0]

    @pl.when(f == 0)
    def _():
        def norm(x):
            ms = jnp.mean(x * x, axis=-1, keepdims=True)
            return (x * lax.rsqrt(ms + EPS) * g_ref[...]).astype(BF16)
        h_ref[FF_HALO:, :] = norm(x_ref[...])
        h_ref[:FF_HALO, :] = norm(xp_ref[...])
        acc_ref[...] = jnp.zeros(acc_ref.shape, F32)

    h = h_ref[...]
    row = lax.broadcasted_iota(jnp.int32, (tm, 1), 0)
    first_tile = (i % tiles_per_seq) == 0

    def conv(w_ref, half):
        u = jnp.dot(h, w_ref[...], preferred_element_type=F32)
        cw = cw_ref[half, 0]
        out = u[FF_HALO:] * cw[2:3] + cb_ref[half, 0]
        for d in (1, 2):
            prev = pltpu.roll(u, d, 0)[FF_HALO:]
            prev = jnp.where(jnp.logical_and(first_tile, row < d), 0.0, prev)
            out = out + prev * cw[2 - d:3 - d]
        return out

    gate = conv(wg_ref, 0)
    val = conv(wv_ref, 1)
    act = gate / (1.0 + jnp.exp(-gate)) * val
    acc_ref[...] += jnp.dot(act.astype(BF16), wd_ref[...], preferred_element_type=F32)

    @pl.when(f == pl.num_programs(1) - 1)
    def _():
        o_ref[...] = x_ref[...] + acc_ref[...]


def _ffn(x2, g, wup, cw, cb, wd, seq):
    n = x2.shape[0]
    tm = 512
    nf = D_FF // FF_CHUNK
    hb = tm // FF_HALO
    return pl.pallas_call(
        functools.partial(_ffn_kernel, tiles_per_seq=seq // tm),
        grid=(n // tm, nf),
        in_specs=[pl.BlockSpec((tm, D_MODEL), lambda i, f: (i, 0)),
                  pl.BlockSpec((FF_HALO, D_MODEL), lambda i, f: (jnp.maximum(i * hb - 1, 0), 0)),
                  pl.BlockSpec((1, D_MODEL), lambda i, f: (0, 0)),
                  pl.BlockSpec((D_MODEL, FF_CHUNK), lambda i, f: (0, f)),
                  pl.BlockSpec((D_MODEL, FF_CHUNK), lambda i, f: (0, nf + f)),
                  pl.BlockSpec((2, 1, CONV_W, FF_CHUNK), lambda i, f: (0, f, 0, 0)),
                  pl.BlockSpec((2, 1, 1, FF_CHUNK), lambda i, f: (0, f, 0, 0)),
                  pl.BlockSpec((FF_CHUNK, D_MODEL), lambda i, f: (f, 0))],
        out_specs=pl.BlockSpec((tm, D_MODEL), lambda i, f: (i, 0)),
        out_shape=jax.ShapeDtypeStruct((n, D_MODEL), F32),
        scratch_shapes=[pltpu.VMEM((tm + FF_HALO, D_MODEL), BF16), pltpu.VMEM((tm, D_MODEL), F32)],
        compiler_params=_cparams(("parallel", "arbitrary")),
        name="convffn",
    )(x2, x2, g, wup, wup, cw, cb, wd)


def _chunk(a, c):
    lead = a.shape[:-2]
    s = a.shape[-2]
    return a.reshape(lead + (s // c, c, LANES))


def _rope_tables(seq):
    def tab(dim, reps):
        inv = ROPE_THETA ** (-jnp.arange(0, dim, 2, dtype=F32) / dim)
        ang = jnp.arange(seq, dtype=F32)[:, None] * inv[None, :]
        cos, sin = jnp.cos(ang), jnp.sin(ang)
        return (jnp.tile(jnp.concatenate([cos, cos], axis=1), (1, reps)),
                jnp.tile(jnp.concatenate([-sin, sin], axis=1), (1, reps)))
    c64, s64 = tab(HEAD_DIM, 256 // HEAD_DIM)
    c32, s32 = tab(IDX_DIM, 256 // IDX_DIM)
    a64 = np.kron(np.eye(256 // HEAD_DIM), np.full((HEAD_DIM, HEAD_DIM), 1.0 / HEAD_DIM)).astype(np.float32)
    a32 = np.kron(np.eye(LANES // IDX_DIM), np.full((IDX_DIM, IDX_DIM), 1.0 / IDX_DIM)).astype(np.float32)
    return c64, s64, c32, s32, jnp.asarray(a64), jnp.asarray(a32)


def _cmp_to_slc(seq, ncp):
    c_start = np.arange(ncp) * CMP_STRIDE
    s_start = np.arange(LANES) * SLC_BLOCK
    m = ((c_start[:, None] < s_start[None, :] + SLC_BLOCK) & (c_start[:, None] + CMP_LEN > s_start[None, :]))
    real = (np.arange(ncp) < (seq - CMP_LEN) // CMP_STRIDE + 1)[:, None] & (np.arange(LANES) < seq // SLC_BLOCK)[None, :]
    return jnp.asarray((m & real).astype(np.float32))


def _pad_w_in(w):
    z = lambda k: jnp.zeros((w.shape[0], k), w.dtype)
    return jnp.concatenate([w[:, :_O_AIW_END], z(C_BQ - _O_AIW_END), w[:, _O_AIW_END:_O_CG_END],
                            z(C_DQ - C_CG - 3 * BRANCH_HEADS), w[:, _O_CG_END:_O_GBR], z(C_GBR - W_MIX),
                            w[:, _O_GBR:]], axis=1)


def _pad_w_in_t(w):
    wt = w.T
    z = lambda k: jnp.zeros((k, w.shape[0]), w.dtype)
    return jnp.concatenate([wt[:_O_AIW_END], z(C_BQ - _O_AIW_END), wt[_O_AIW_END:_O_CG_END],
                            z(C_DQ - C_CG - 3 * BRANCH_HEADS), wt[_O_CG_END:_O_GBR], z(C_GBR - W_MIX),
                            wt[_O_GBR:]], axis=0)


def _gain_row(a_qk_g, a_lat_g, a_idx_k_g, b_qk_g, c_qk_g, d_qk_g):
    one = lambda k: jnp.ones((k,), F32)
    zero = lambda k: jnp.zeros((k,), F32)
    t4 = lambda g: jnp.tile(g, BRANCH_HEADS)
    row = jnp.concatenate([
        t4(a_qk_g[0]) * ATTN_SCALE, a_lat_g, one(256), a_idx_k_g, zero(LANES - IDX_DIM),
        t4(b_qk_g[0]) * ATTN_SCALE, t4(b_qk_g[1]), one(256),
        t4(c_qk_g[0]) * ATTN_SCALE, one(LANES), c_qk_g[2], one(HEAD_DIM), c_qk_g[3], one(HEAD_DIM), one(LANES),
        t4(d_qk_g[0]) * ATTN_SCALE, jnp.tile(d_qk_g[1], D_KV_HEADS), one(LANES)])
    return row[None, :]


def _mixers(proj, b, s, l, tabs, c2s, a_qk_g, a_lat_g, a_kv_up, a_idx_k_g, b_qk_g, c_qk_g, c_cmp_pe,
            c_cmp_w1, c_cmp_w2, d_qk_g, d_sink):
    gains = _gain_row(a_qk_g[l], a_lat_g[l], a_idx_k_g[l], b_qk_g[l], c_qk_g[l], d_qk_g[l])
    g2 = jnp.concatenate([a_qk_g[l][1], jnp.ones((HEAD_DIM,), F32)])[None, :]
    (qa, iq, ikw, ik, kxa, ova, qb, kxb, ovb, kmean, qc, kvc, kxs, ovs, kvw, gc, qd, kvd) = _prep(
        proj, gains, g2, a_kv_up[l], tabs, b, s)
    flat = lambda y: y.reshape(b * s, BRANCH_WIDTH)
    ya = flat(_dsa(qa, iq, ikw, _chunk(ik, KC), _chunk(kxa, KC), _chunk(ova, KC), s))
    nb = s // MOBA_BLOCK
    km = kmean.reshape(b, nb, BRANCH_HEADS, HEAD_DIM).transpose(0, 2, 1, 3)
    km = jnp.pad(km, ((0, 0), (0, 0), (0, LANES - nb), (0, LANES - HEAD_DIM)))
    kcm = min(MOBA_KC, s)
    yb = flat(_moba(qb, km, _chunk(kxb, kcm), _chunk(ovb, kcm), s))
    nrow = s // CMP_STRIDE
    w1 = c_cmp_w1[l].reshape(2, CMP_LEN, HEAD_DIM, CMP_HIDDEN)
    zw = jnp.zeros_like(w1[0])
    w1 = jnp.concatenate([jnp.concatenate([w1[0], zw], axis=2), jnp.concatenate([zw, w1[1]], axis=2)], axis=1)
    w2 = c_cmp_w2[l]
    w2 = jnp.concatenate([jnp.pad(w2[0], ((0, 0), (0, HEAD_DIM))), jnp.pad(w2[1], ((0, 0), (HEAD_DIM, 0)))], axis=0)
    pe = jnp.concatenate([c_cmp_pe[l][0], c_cmp_pe[l][1]], axis=1)
    gk = jnp.concatenate([c_qk_g[l][1], jnp.ones((HEAD_DIM,), F32)])[None, :]
    cmp = _compress(kvc, w1, pe, w2, gk)
    ncp = c2s.shape[0]
    if ncp > nrow:
        cmp = jnp.pad(cmp, ((0, 0), (0, ncp - nrow), (0, 0)))
    yc = flat(_nsa(qc, gc, cmp, c2s, _chunk(kxs, KC), _chunk(ovs, KC), _chunk(kvw, QBLK), s))
    yd = flat(_swa(d_sink[l], qd, _chunk(kvd, QBLK), s))
    return ya, yb, yc, yd


def kernel(x, norm1_g, w_in, a_qk_g, a_lat_g, a_kv_up, a_idx_k_g, b_qk_g, c_qk_g, c_cmp_pe, c_cmp_w1,
           c_cmp_w2, d_qk_g, d_sink, w_branch, w_out, norm2_g, w_up, conv_w, conv_b, w_down):
    b, s, _ = x.shape
    depth = w_in.shape[0]
    assert s % KC == 0 and s >= NSA_WINDOW + NSA_QB and KC >= min(DSA_TOPK, s // 4)
    tabs = _rope_tables(s)
    ncp = max(LANES, s // CMP_STRIDE)
    c2s = _cmp_to_slc(s, ncp)
    x2 = x.reshape(b * s, D_MODEL)
    nf = D_FF // FF_CHUNK
    for l in range(depth):
        proj = _inproj(x2, norm1_g[l][None, :], _pad_w_in_t(w_in[l]).astype(BF16))
        ys = _mixers(proj, b, s, l, tabs, c2s, a_qk_g, a_lat_g, a_kv_up, a_idx_k_g, b_qk_g, c_qk_g,
                     c_cmp_pe, c_cmp_w1, c_cmp_w2, d_qk_g, d_sink)
        x2 = _merge(x2, ys, proj, w_branch[l].astype(BF16), w_out[l].astype(BF16))
        cw = conv_w[l].reshape(CONV_W, 2, nf, FF_CHUNK).transpose(1, 2, 0, 3)
        cb = conv_b[l].reshape(2, nf, 1, FF_CHUNK)
        x2 = _ffn(x2, norm2_g[l][None, :], w_up[l].astype(BF16), cw, cb, w_down[l].astype(BF16), s)
    return x2.reshape(b, s, D_MODEL)
```

```python
import functools
import math

import numpy as np
import jax
import jax.numpy as jnp
from jax import lax
from jax.experimental import pallas as pl
from jax.experimental.pallas import tpu as pltpu

F32 = jnp.float32
BF16 = jnp.bfloat16
HIGHEST = lax.Precision.HIGHEST

LANES = 128
SUBLANES = 8
PACK = 2 * SUBLANES
VMEM_LIMIT = 48 * 1024 * 1024

D_MODEL = 1024
HEAD_DIM = 64
N_BRANCH = 4
BRANCH_HEADS = 4
BRANCH_WIDTH = BRANCH_HEADS * HEAD_DIM
ROPE_THETA = 10000.0
QBLK = 128
NEG = -1e30
TINY = 1e-30
EPS = 1e-6
ATTN_SCALE = HEAD_DIM ** -0.5
A_LATENT = 128
IDX_HEADS = 8
IDX_DIM = 32
DSA_TOPK = 256
MOBA_BLOCK = 256
MOBA_TOPK = 3
CMP_LEN = 32
CMP_STRIDE = 16
CMP_HIDDEN = 128
SLC_BLOCK = 64
SLC_TOPK = 16
NSA_WINDOW = 512
FORCE_SCORE = 1e9
NSA_KV = 6
SWA_WINDOW = 128
D_KV_HEADS = 2
D_FF = 256 * ((8 * D_MODEL // 3 + 255) // 256)
CONV_W = 3
IW_SCALE = IDX_HEADS ** -0.5 * IDX_DIM ** -0.5
LOG2E = math.log2(math.e)
INT_MIN = -2 ** 31

_O_AIW_END = 680
_O_CG_END = 2100
_O_GBR = 2612
C_AQ, C_ALAT, C_AIQ, C_AIKW = 0, 256, 384, 640
C_BQ, C_BK, C_BV = 768, 1024, 1280
C_CQ, C_CKV, C_CG = 1536, 1792, 2176
C_DQ, C_DK, C_DV = 2304, 2560, 2688
W_MIX = 2816
C_GBR = 3072
W_TOT = C_GBR + N_BRANCH * D_MODEL

KC = 512


def _cparams(sem):
    return pltpu.CompilerParams(dimension_semantics=sem, vmem_limit_bytes=VMEM_LIMIT)


def _inproj_kernel(x_ref, g_ref, w_ref, o_ref, h_ref):
    @pl.when(pl.program_id(1) == 0)
    def _():
        x = x_ref[...]
        ms = jnp.mean(x * x, axis=-1, keepdims=True)
        h_ref[...] = (x * lax.rsqrt(ms + EPS) * g_ref[...]).astype(h_ref.dtype)

    o_ref[...] = lax.dot_general(h_ref[...], w_ref[...], (((1,), (1,)), ((), ())),
                                 preferred_element_type=F32)


def _inproj(x2, g, wt):
    n = x2.shape[0]
    tm, tn = 512, 1024
    return pl.pallas_call(
        _inproj_kernel,
        grid=(n // tm, W_TOT // tn),
        in_specs=[pl.BlockSpec((tm, D_MODEL), lambda i, j: (i, 0)),
                  pl.BlockSpec((1, D_MODEL), lambda i, j: (0, 0)),
                  pl.BlockSpec((tn, D_MODEL), lambda i, j: (j, 0))],
        out_specs=pl.BlockSpec((tm, tn), lambda i, j: (i, j)),
        out_shape=jax.ShapeDtypeStruct((n, W_TOT), F32),
        scratch_shapes=[pltpu.VMEM((tm, D_MODEL), BF16)],
        compiler_params=_cparams(("parallel", "arbitrary")),
        name="inproj",
    )(x2, g, wt)


def _rope(x, cosw, sinw, half):
    w = x.shape[1]
    lane = lax.broadcasted_iota(jnp.int32, x.shape, 1)
    first = (lane % (2 * half)) < half
    rot = jnp.where(first, pltpu.roll(x, w - half, 1), pltpu.roll(x, half, 1))
    return x * cosw + rot * sinw


def _gnorm(x, avg, gain):
    ms = jnp.dot(x * x, avg, precision=HIGHEST, preferred_element_type=F32)
    return x * lax.rsqrt(ms + EPS) * gain


def _prep_kernel(p_ref, g_ref, g2_ref, kvup_ref, c64_ref, s64_ref, c32_ref, s32_ref, a64_ref, a32_ref,
                 qa_ref, iq_ref, ikw_ref, ik_ref, kxa_ref, ova_ref,
                 qb_ref, kxb_ref, ovb_ref, km_ref,
                 qc_ref, kvc_ref, kxs_ref, ovs_ref, kvw_ref, gc_ref,
                 qd_ref, kvd_ref):
    c64, s64 = c64_ref[...], s64_ref[...]
    c32, s32 = c32_ref[...], s32_ref[...]
    a64, a32 = a64_ref[...], a32_ref[...]
    c64h, s64h, a64h = c64[:, :LANES], s64[:, :LANES], a64[:LANES, :LANES]
    tm = p_ref.shape[0]
    lane1 = lax.broadcasted_iota(jnp.int32, (tm, LANES), 1)
    lo64 = lane1 < HEAD_DIM

    def seg(c, w):
        return p_ref[:, c:c + w], g_ref[:, c:c + w]

    def normrope256(c):
        x, g = seg(c, 256)
        return _rope(_gnorm(x, a64, g), c64, s64, HEAD_DIM // 2)

    def swap(x):
        return pltpu.roll(x, HEAD_DIM, 1)

    def put_heads(q, ref):
        for p in range(BRANCH_HEADS // 2):
            pair = q[:, p * LANES:(p + 1) * LANES]
            ref[0, 2 * p] = jnp.where(lo64, pair, 0.0).astype(ref.dtype)
            ref[0, 2 * p + 1] = jnp.where(lo64, swap(pair), 0.0).astype(ref.dtype)

    def put_kv(kv, kx_ref, ov_ref, idx):
        kx_ref[idx] = jnp.where(lo64, kv * LOG2E, kv).astype(BF16)
        ov_ref[idx] = jnp.where(lo64, 1.0, kv).astype(BF16)

    put_heads(normrope256(C_AQ), qa_ref)
    x, g = seg(C_ALAT, A_LATENT)
    ms = jnp.mean(x * x, axis=-1, keepdims=True)
    latn = x * lax.rsqrt(ms + EPS) * g
    kv = jnp.dot(latn, kvup_ref[...], precision=HIGHEST, preferred_element_type=F32)
    kr = _rope(_gnorm(kv, a64h, g2_ref[...]), c64h, s64h, HEAD_DIM // 2)
    put_kv(jnp.where(lo64, kr, kv), kxa_ref, ova_ref, 0)
    x, _ = seg(C_AIQ, 256)
    iq = _rope(x, c32, s32, IDX_DIM // 2)
    per = LANES // IDX_DIM
    for h in range(IDX_HEADS):
        tile = iq[:, (h // per) * LANES:(h // per + 1) * LANES]
        if h % per:
            tile = pltpu.roll(tile, LANES - (h % per) * IDX_DIM, 1)
        iq_ref[0, h] = jnp.where(lane1 < IDX_DIM, tile, 0.0).astype(BF16)
    x, g = seg(C_AIKW, LANES)
    xr = _rope(_gnorm(x, a32, g), c32[:, :LANES], s32[:, :LANES], IDX_DIM // 2)
    ikw_ref[0] = jnp.where(lane1 < IDX_DIM, xr, x * IW_SCALE)
    ik_ref[0] = jnp.where(lane1 < IDX_DIM, xr, 0.0).astype(BF16)
    put_heads(normrope256(C_BQ), qb_ref)
    kb = normrope256(C_BK)
    km_ref[0] = jnp.mean(kb, axis=0, keepdims=True)
    for p in range(BRANCH_HEADS // 2):
        kp = kb[:, p * LANES:(p + 1) * LANES]
        vp = p_ref[:, C_BV + p * LANES:C_BV + (p + 1) * LANES]
        put_kv(jnp.where(lo64, kp, swap(vp)), kxb_ref, ovb_ref, (0, 2 * p))
        put_kv(jnp.where(lo64, swap(kp), vp), kxb_ref, ovb_ref, (0, 2 * p + 1))
    put_heads(normrope256(C_CQ), qc_ref)
    for r in range(3):
        c = C_CKV + r * LANES
        x, g = seg(c, LANES)
        xn = x if r == 0 else _gnorm(x, a64h, g)
        kv = jnp.where(lo64, _rope(xn, c64h, s64h, HEAD_DIM // 2), x)
        if r == 0:
            kvc_ref[0] = kv
        elif r == 1:
            put_kv(kv, kxs_ref, ovs_ref, 0)
        else:
            kvw_ref[0] = kv.astype(BF16)
    x, _ = seg(C_CG, LANES)
    gc_ref[0] = 1.0 / (1.0 + jnp.exp(-x))
    put_heads(normrope256(C_DQ), qd_ref)
    x, g = seg(C_DK, LANES)
    kd = _rope(_gnorm(x, a64h, g), c64h, s64h, HEAD_DIM // 2)
    vd = p_ref[:, C_DV:C_DV + LANES]
    kvd_ref[0, 0] = jnp.where(lo64, kd, swap(vd)).astype(BF16)
    kvd_ref[0, 1] = jnp.where(lo64, swap(kd), vd).astype(BF16)


def _prep(proj, gains, g2, kvup, tabs, b, seq):
    n = proj.shape[0]
    tm = MOBA_BLOCK
    nt = seq // tm
    c64, s64, c32, s32, a64, a32 = tabs
    tab_spec = pl.BlockSpec((tm, 256), lambda i: (i % nt, 0))
    full = lambda a: pl.BlockSpec(a.shape, lambda i: (0,) * a.ndim)
    tok = lambda dt: (jax.ShapeDtypeStruct((b, seq, LANES), dt),
                      pl.BlockSpec((1, tm, LANES), lambda i: (i // nt, i % nt, 0)))
    hd = lambda nh, dt: (jax.ShapeDtypeStruct((b, nh, seq, LANES), dt),
                         pl.BlockSpec((1, nh, tm, LANES), lambda i: (i // nt, 0, i % nt, 0)))
    outs = [hd(BRANCH_HEADS, BF16), hd(IDX_HEADS, BF16), tok(F32), tok(BF16), tok(BF16), tok(BF16),
            hd(BRANCH_HEADS, F32), hd(BRANCH_HEADS, BF16), hd(BRANCH_HEADS, BF16),
            (jax.ShapeDtypeStruct((n // tm, 1, 256), F32), pl.BlockSpec((1, 1, 256), lambda i: (i, 0, 0))),
            hd(BRANCH_HEADS, F32), tok(F32), tok(BF16), tok(BF16), tok(BF16), tok(F32),
            hd(BRANCH_HEADS, BF16), hd(D_KV_HEADS, BF16)]
    return pl.pallas_call(
        _prep_kernel,
        grid=(n // tm,),
        in_specs=[pl.BlockSpec((tm, W_MIX), lambda i: (i, 0)), full(gains), full(g2), full(kvup),
                  tab_spec, tab_spec, tab_spec, tab_spec, full(a64), full(a32)],
        out_specs=[o[1] for o in outs],
        out_shape=[o[0] for o in outs],
        compiler_params=_cparams(("parallel",)),
        name="prep",
    )(proj, gains, g2, kvup, c64, s64, c32, s32, a64, a32)


def _gelu_tanh(x):
    return 0.5 * x * (1.0 + jnp.tanh(math.sqrt(2.0 / math.pi) * (x + 0.044715 * (x * x * x))))


def _cmp_kernel(x_ref, w1_ref, pe_ref, w2_ref, g_ref, o_ref):
    nrow = o_ref.shape[1]
    u = jnp.zeros((nrow, 2 * CMP_HIDDEN), F32)
    v = jnp.zeros((nrow, 2 * CMP_HIDDEN), F32)
    for l in range(CMP_STRIDE):
        xl = x_ref[0, pl.ds(l, nrow, stride=CMP_STRIDE), :]
        u = u + jnp.dot(xl + pe_ref[l:l + 1, :], w1_ref[l], precision=HIGHEST, preferred_element_type=F32)
        v = v + jnp.dot(xl + pe_ref[CMP_STRIDE + l:CMP_STRIDE + l + 1, :], w1_ref[CMP_STRIDE + l],
                        precision=HIGHEST, preferred_element_type=F32)
    pre = u + pltpu.roll(v, nrow - 1, 0)
    out = jnp.dot(_gelu_tanh(pre), w2_ref[...], precision=HIGHEST, preferred_element_type=F32)
    lane = lax.broadcasted_iota(jnp.int32, out.shape, 1)
    lo = lane < HEAD_DIM
    ms = jnp.sum(jnp.where(lo, out * out, 0.0), axis=-1, keepdims=True) * (1.0 / HEAD_DIM)
    o_ref[0] = jnp.where(lo, out * lax.rsqrt(ms + EPS) * g_ref[...], out)


def _compress(kvc, w1, pe, w2, gk):
    b, seq, _ = kvc.shape
    nrow = seq // CMP_STRIDE
    full = lambda a: pl.BlockSpec(a.shape, lambda i: (0,) * a.ndim)
    return pl.pallas_call(
        _cmp_kernel,
        grid=(b,),
        in_specs=[pl.BlockSpec((1, seq, LANES), lambda i: (i, 0, 0)), full(w1), full(pe), full(w2), full(gk)],
        out_specs=pl.BlockSpec((1, nrow, LANES), lambda i: (i, 0, 0)),
        out_shape=jax.ShapeDtypeStruct((b, nrow, LANES), F32),
        compiler_params=_cparams(("parallel",)),
        name="nsa_compress",
    )(kvc, w1, pe, w2, gk)


def _tile_rows(x, n):
    return jnp.concatenate([x] * n, axis=0)


def _online_step(carry, s, ov):
    m, acc = carry
    m_new = jnp.maximum(m, jnp.max(s, axis=-1, keepdims=True))
    e = jnp.exp2(s - m_new)
    acc = jnp.exp2(m - m_new) * acc + jnp.dot(e.astype(BF16), ov, preferred_element_type=F32)
    return m_new, acc


def _online_init(rows):
    return (jnp.full((rows, 1), NEG, F32), jnp.zeros((rows, LANES), F32))


def _online_finish(carry):
    _, acc = carry
    den = jnp.maximum(pltpu.roll(acc, HEAD_DIM, 1), TINY)
    lane = lax.broadcasted_iota(jnp.int32, acc.shape, 1)
    return jnp.where(lane >= HEAD_DIM, acc / den, 0.0)


def _top_lowest_index(work, jj, k):
    picks = jnp.zeros(work.shape, F32)
    jf = jj.astype(F32)
    big = float(work.shape[1])
    for _ in range(k):
        mx = jnp.max(work, axis=-1, keepdims=True)
        idx = jnp.min(jnp.where(work == mx, jf, big), axis=-1, keepdims=True)
        pick = jf == idx
        picks = jnp.where(pick, 1.0, picks)
        work = jnp.where(pick, -jnp.inf, work)
    return picks


def _split_bf16(a):
    hi = a.astype(BF16)
    return hi, (a - hi.astype(F32)).astype(BF16)


_NT = (((1,), (1,)), ((), ()))


def _pair_heads(o_even, o_odd):
    lane = lax.broadcasted_iota(jnp.int32, o_even.shape, 1)
    return jnp.where(lane < HEAD_DIM, pltpu.roll(o_even, HEAD_DIM, 1), o_odd)


def _dsa_kernel(q_ref, iq_ref, iw_ref, ik_ref, kx_ref, ov_ref, o_ref, sc_ref, cb_ref, *, seq, topk):
    i = pl.program_id(1)
    rows = QBLK
    t0 = i * rows
    nck = (t0 + rows + KC - 1) // KC
    tq = t0 + lax.broadcasted_iota(jnp.int32, (1, LANES), 1)
    krow = lax.broadcasted_iota(jnp.int32, (KC, LANES), 0)
    iq = iq_ref[0].reshape(IDX_HEADS * rows, LANES)
    pick = jnp.where(lax.broadcasted_iota(jnp.int32, (SUBLANES, LANES), 1)
                     == lax.broadcasted_iota(jnp.int32, (SUBLANES, LANES), 0) + IDX_DIM, 1.0, 0.0)
    iw = lax.dot_general(pick, iw_ref[0], _NT, precision=HIGHEST, preferred_element_type=F32)
    ftop = float(topk)

    def score_chunk(c, carry):
        d = lax.dot_general(ik_ref[0, c], iq, _NT, preferred_element_type=F32)
        acc = jnp.zeros((KC, LANES), F32)
        for h in range(IDX_HEADS):
            acc = acc + jnp.maximum(d[:, h * LANES:(h + 1) * LANES], 0.0) * iw[h:h + 1, :]
        acc = jnp.where(acc == 0.0, 0.0, acc)
        acc = jnp.where(c * KC + krow <= tq, acc, NEG)
        bits = lax.bitcast_convert_type(acc, jnp.int32)
        sc_ref[c] = bits ^ ((bits >> 31) & 0x7FFFFFFF)
        cb_ref[c] = lax.bitcast_convert_type(bits & (-1 << 16), F32).astype(BF16)
        return carry

    lax.fori_loop(0, nck, score_chunk, 0)

    def count(pred):
        def body(c, acc):
            hit = jnp.where(pred(sc_ref[c], c * KC + krow), 1.0, 0.0)
            while hit.shape[0] > SUBLANES:
                half = hit.shape[0] // 2
                hit = hit[:half] + hit[half:]
            return acc + hit
        acc = lax.fori_loop(0, nck, body, jnp.zeros((SUBLANES, LANES), F32))
        return jnp.sum(acc, axis=0, keepdims=True)

    zero = jnp.zeros((1, LANES), jnp.int32)
    one_b, zero_b = jnp.array(1, BF16), jnp.array(0, BF16)
    ge, gt, eq = (lambda x, c: x >= c), (lambda x, c: x > c), (lambda x, c: x == c)

    def count_b(op, cand):
        c16 = jnp.broadcast_to(cand, (PACK, LANES)).astype(BF16)[None]

        def body(c, acc):
            x = cb_ref[c].reshape(KC // PACK, PACK, LANES)
            hit = jnp.where(op(x, c16), one_b, zero_b)
            while hit.shape[0] > 1:
                half = hit.shape[0] // 2
                hit = hit[:half] + hit[half:]
            return acc + hit[0].astype(F32)
        acc = lax.fori_loop(0, nck, body, jnp.zeros((PACK, LANES), F32))
        return jnp.sum(acc, axis=0, keepdims=True)

    def bsearch(nb, start, want, value):
        def step(b, t):
            cand = t | lax.shift_left(jnp.int32(1), nb - 1 - b)
            return jnp.where(count_b(ge, value(cand)) >= want, cand, t)
        return lax.fori_loop(0, nb, step, start)

    def val16(k):
        hb = jnp.where(k >= 0, k, k ^ 0x7FFF) & 0xFFFF
        tiny = jnp.where(hb > 0x8000, 0, jnp.where((hb & 0x7F) == 0, 0, 0x0080))
        hb = jnp.where((hb & 0x7F80) == 0, tiny, hb)
        return lax.bitcast_convert_type(lax.shift_left(hb, 16), F32)

    def refill(field):
        def body(c, carry):
            cb_ref[c] = field(sc_ref[c]).astype(F32).astype(BF16)
            return carry
        lax.fori_loop(0, nck, body, 0)

    as_f32 = lambda k: k.astype(F32)
    hi = jnp.where(count_b(ge, val16(zero)) >= ftop, zero, jnp.full_like(zero, -(1 << 15)))
    hi = bsearch(15, hi, ftop, val16)
    need = ftop - count_b(gt, val16(hi))
    for shift in (8, 0):
        prefix = hi
        refill(lambda k: jnp.where((k >> (shift + 8)) == prefix, (k >> shift) & 0xFF, -1))
        low = bsearch(8, zero, need, as_f32)
        need = need - count_b(gt, as_f32(low))
        hi = lax.shift_left(hi, 8) | low
    thr = hi
    n_eq = count_b(eq, as_f32(low))
    nbits = max(1, int(math.ceil(math.log2(seq))))

    def tie_search(_):
        def jstep(b, jb):
            cand = jb | lax.shift_left(jnp.int32(1), nbits - 1 - b)
            c = count(lambda k, idx: jnp.where(k == thr, idx, seq + cand) < cand)
            return jnp.where(c < need, cand, jb)
        return lax.fori_loop(0, nbits, jstep, zero)

    jbound = lax.cond(jnp.max(n_eq - need) > 0.0, tie_search, lambda _: jnp.full_like(zero, seq), 0)

    q = q_ref[0].reshape(BRANCH_HEADS * rows, LANES)
    eye = jnp.where(lax.broadcasted_iota(jnp.int32, (LANES, LANES), 0)
                    == lax.broadcasted_iota(jnp.int32, (LANES, LANES), 1), 1.0, 0.0).astype(BF16)

    def attend(c, carry):
        k = sc_ref[c]
        idx = c * KC + krow
        tie_ok = jnp.where(k == thr, idx, seq + jbound) <= jbound
        sel = jnp.where(k > thr, idx, jnp.where(tie_ok, idx, seq + tq)) <= tq
        sel_t = jnp.where(sel, 1.0, 0.0).astype(BF16)
        hit = lax.dot_general(eye, sel_t, _NT, preferred_element_type=F32)
        bias = jnp.where(hit > 0.5, 0.0, -jnp.inf)
        s = lax.dot_general(q, kx_ref[0, c], _NT, preferred_element_type=F32)
        s = (s.reshape(BRANCH_HEADS, rows, KC) + bias[None]).reshape(BRANCH_HEADS * rows, KC)
        return _online_step(carry, s, ov_ref[0, c])

    out = _online_finish(lax.fori_loop(0, nck, attend, _online_init(BRANCH_HEADS * rows)))
    for p in range(BRANCH_HEADS // 2):
        o_ref[0, :, p * LANES:(p + 1) * LANES] = _pair_heads(
            out[2 * p * rows:(2 * p + 1) * rows], out[(2 * p + 1) * rows:(2 * p + 2) * rows]).astype(o_ref.dtype)


def _dsa(q, iq, ikw, ik, kx, ov, seq):
    b = q.shape[0]
    nq = seq // QBLK
    nc = seq // KC
    topk = min(DSA_TOPK, seq // 4)
    return pl.pallas_call(
        functools.partial(_dsa_kernel, seq=seq, topk=topk),
        grid=(b, nq),
        in_specs=[pl.BlockSpec((1, BRANCH_HEADS, QBLK, LANES), lambda bi, i: (bi, 0, i, 0)),
                  pl.BlockSpec((1, IDX_HEADS, QBLK, LANES), lambda bi, i: (bi, 0, i, 0)),
                  pl.BlockSpec((1, QBLK, LANES), lambda bi, i: (bi, i, 0)),
                  pl.BlockSpec((1, nc, KC, LANES), lambda bi, i: (bi, 0, 0, 0)),
                  pl.BlockSpec((1, nc, KC, LANES), lambda bi, i: (bi, 0, 0, 0)),
                  pl.BlockSpec((1, nc, KC, LANES), lambda bi, i: (bi, 0, 0, 0))],
        out_specs=pl.BlockSpec((1, QBLK, BRANCH_WIDTH), lambda bi, i: (bi, i, 0)),
        out_shape=jax.ShapeDtypeStruct((b, seq, BRANCH_WIDTH), BF16),
        scratch_shapes=[pltpu.VMEM((nc, KC, QBLK), jnp.int32), pltpu.VMEM((nc, KC, QBLK), BF16)],
        compiler_params=_cparams(("parallel", "arbitrary")),
        name="dsa",
    )(q, iq, ikw, ik, kx, ov)


def _moba_kernel(q_ref, km_ref, kx_ref, ov_ref, o_ref, *, seq):
    own = pl.program_id(2)
    rows = MOBA_BLOCK
    jj = lax.broadcasted_iota(jnp.int32, (rows, LANES), 1)
    kcm = kx_ref.shape[3]
    per = kcm // MOBA_BLOCK
    blk_row = lax.broadcasted_iota(jnp.int32, (LANES, kcm), 0)
    blk_col = lax.broadcasted_iota(jnp.int32, (LANES, kcm), 1) // MOBA_BLOCK
    col = lax.broadcasted_iota(jnp.int32, (rows, kcm), 1)
    own0 = own * MOBA_BLOCK
    tq = own0 + lax.broadcasted_iota(jnp.int32, (rows, 1), 0)
    outs = []
    for hh in range(2):
        qf = q_ref[0, hh]
        gate = lax.dot_general(qf, km_ref[0, hh], _NT, precision=HIGHEST, preferred_element_type=F32)
        gate = jnp.where(jj < own, gate, NEG)
        picks = _top_lowest_index(gate, jj, MOBA_TOPK)
        picks = jnp.where(jj < own, picks, 0.0).astype(BF16)
        qb = qf.astype(BF16)

        def group(g, carry, hh=hh, picks=picks, qb=qb):
            expand = jnp.where(blk_row == blk_col + g * per, 1.0, 0.0).astype(BF16)
            hit = jnp.dot(picks, expand, preferred_element_type=F32) > 0.5
            kpos = g * kcm + col
            ok = jnp.where(hit, kpos, jnp.where(kpos >= own0, kpos, seq + tq)) <= tq
            s = lax.dot_general(qb, kx_ref[0, hh, g], _NT, preferred_element_type=F32)
            return _online_step(carry, jnp.where(ok, s, -jnp.inf), ov_ref[0, hh, g])

        outs.append(_online_finish(lax.fori_loop(0, own // per + 1, group, _online_init(rows))))
    o_ref[0] = _pair_heads(outs[0], outs[1]).astype(o_ref.dtype)


MOBA_KC = 4 * MOBA_BLOCK


def _moba(q, km, kx, ov, seq):
    b = q.shape[0]
    nb = seq // MOBA_BLOCK
    kcm = min(MOBA_KC, seq)
    ng = seq // kcm
    return pl.pallas_call(
        functools.partial(_moba_kernel, seq=seq),
        grid=(b, BRANCH_HEADS // 2, nb),
        in_specs=[pl.BlockSpec((1, 2, MOBA_BLOCK, LANES), lambda bi, p, i: (bi, p, i, 0)),
                  pl.BlockSpec((1, 2, LANES, LANES), lambda bi, p, i: (bi, p, 0, 0)),
                  pl.BlockSpec((1, 2, ng, kcm, LANES), lambda bi, p, i: (bi, p, 0, 0, 0)),
                  pl.BlockSpec((1, 2, ng, kcm, LANES), lambda bi, p, i: (bi, p, 0, 0, 0))],
        out_specs=pl.BlockSpec((1, MOBA_BLOCK, LANES), lambda bi, p, i: (bi, i, p)),
        out_shape=jax.ShapeDtypeStruct((b, seq, BRANCH_WIDTH), BF16),
        compiler_params=_cparams(("parallel", "parallel", "arbitrary")),
        name="moba",
    )(q, km, kx, ov)


def _nsa_kernel(q_ref, g_ref, cmp_ref, c2s_ref, kxs_ref, ovs_ref, kvw_ref,
                o_ref, *, seq):
    i = pl.program_id(1)
    qr = q_ref.shape[2]
    t0 = i * qr
    rows = BRANCH_HEADS * qr
    qf = q_ref[0].reshape(rows, LANES)
    qb = qf.astype(BF16)
    tpos = t0 + lax.broadcasted_iota(jnp.int32, (qr, 1), 0)
    tpos4 = _tile_rows(tpos, BRANCH_HEADS)

    ncp = cmp_ref.shape[1]
    q_hi, q_lo = _split_bf16(qf)
    c_hi, c_lo = _split_bf16(cmp_ref[0])
    s = (lax.dot_general(q_hi, c_hi, _NT, preferred_element_type=F32)
         + lax.dot_general(q_hi, c_lo, _NT, preferred_element_type=F32)
         + lax.dot_general(q_lo, c_hi, _NT, preferred_element_type=F32))
    cend = lax.broadcasted_iota(jnp.int32, (rows, ncp), 1) * CMP_STRIDE + (CMP_LEN - 1)
    ok = cend <= tpos4
    s = jnp.where(ok, s, NEG)
    e = jnp.where(ok, jnp.exp(s - jnp.max(s, axis=-1, keepdims=True)), 0.0)
    p = e / jnp.maximum(jnp.sum(e, axis=-1, keepdims=True), TINY)
    o_c = jnp.dot(p.astype(BF16), c_hi, preferred_element_type=F32)
    psum = p[0:qr]
    for h in range(1, BRANCH_HEADS):
        psum = psum + p[h * qr:(h + 1) * qr]
    p_hi, p_lo = _split_bf16(psum)
    c2s = c2s_ref[...].astype(BF16)
    imp = (jnp.dot(p_hi, c2s, preferred_element_type=F32)
           + jnp.dot(p_lo, c2s, preferred_element_type=F32))

    jj = lax.broadcasted_iota(jnp.int32, (qr, LANES), 1)
    cur = tpos // SLC_BLOCK
    forced = (jj == 0) | (jj == cur) | (jj == cur - 1)
    imp = jnp.where(forced, FORCE_SCORE, imp)
    imp = jnp.where(jj <= cur, imp, NEG)
    imp = jnp.where(jj < seq // SLC_BLOCK, imp, -jnp.inf)
    ntop = min(SLC_TOPK, seq // SLC_BLOCK)
    picks = _top_lowest_index(imp, jj, ntop)
    picks = jnp.where(jj <= cur, picks, 0.0).astype(BF16)

    nck = (t0 + qr + KC - 1) // KC
    blk_row = lax.broadcasted_iota(jnp.int32, (LANES, KC), 0)
    blk_col = lax.broadcasted_iota(jnp.int32, (LANES, KC), 1) // SLC_BLOCK
    lane = lax.broadcasted_iota(jnp.int32, (qr, KC), 1)

    def slc(c, carry):
        expand = jnp.where(blk_row == blk_col + c * (KC // SLC_BLOCK), 1.0, 0.0).astype(BF16)
        hit = jnp.dot(picks, expand, preferred_element_type=F32) > 0.5
        ok = jnp.where(hit, c * KC + lane, seq + tpos) <= tpos
        bias = jnp.where(ok, 0.0, -jnp.inf)
        sc = lax.dot_general(qb, kxs_ref[0, c], _NT, preferred_element_type=F32)
        sc = (sc.reshape(BRANCH_HEADS, qr, KC) + bias[None]).reshape(rows, KC)
        return _online_step(carry, sc, ovs_ref[0, c])

    o_s = _online_finish(lax.fori_loop(0, nck, slc, _online_init(rows)))

    nwin = (NSA_WINDOW + qr) // QBLK
    cb = jnp.maximum(i * (qr // QBLK) - NSA_WINDOW // QBLK, 0)
    sw = jnp.concatenate([lax.dot_general(qb, kvw_ref[0, cb + r], _NT, preferred_element_type=F32)
                          for r in range(nwin)], axis=1)
    kpos = cb * QBLK + lax.broadcasted_iota(jnp.int32, (rows, nwin * QBLK), 1)
    diff = tpos4 - kpos
    ok = jnp.where(diff >= 0, diff, NSA_WINDOW) < NSA_WINDOW
    sw = jnp.where(ok, sw, NEG)
    e = jnp.where(ok, jnp.exp(sw - jnp.max(sw, axis=-1, keepdims=True)), 0.0)
    pw = (e / jnp.maximum(jnp.sum(e, axis=-1, keepdims=True), TINY)).astype(BF16)
    o_w = jnp.zeros((rows, LANES), F32)
    for r in range(nwin):
        o_w = o_w + jnp.dot(pw[:, r * QBLK:(r + 1) * QBLK], kvw_ref[0, cb + r],
                            preferred_element_type=F32)

    g = g_ref[0]
    heads = []
    for h in range(BRANCH_HEADS):
        sl = slice(h * qr, (h + 1) * qr)
        heads.append(g[:, 3 * h:3 * h + 1] * o_c[sl] + g[:, 3 * h + 1:3 * h + 2] * o_s[sl]
                     + g[:, 3 * h + 2:3 * h + 3] * o_w[sl])
    for p in range(BRANCH_HEADS // 2):
        o_ref[0, :, p * LANES:(p + 1) * LANES] = _pair_heads(heads[2 * p], heads[2 * p + 1]).astype(o_ref.dtype)


NSA_QB = 256


def _nsa(q, g, cmp, c2s, kxs, ovs, kvw, seq):
    b = q.shape[0]
    qb = NSA_QB
    nq = seq // QBLK
    nc = seq // KC
    ncp = cmp.shape[1]
    return pl.pallas_call(
        functools.partial(_nsa_kernel, seq=seq),
        grid=(b, seq // qb),
        in_specs=[pl.BlockSpec((1, BRANCH_HEADS, qb, LANES), lambda bi, i: (bi, 0, i, 0)),
                  pl.BlockSpec((1, qb, LANES), lambda bi, i: (bi, i, 0)),
                  pl.BlockSpec((1, ncp, LANES), lambda bi, i: (bi, 0, 0)),
                  pl.BlockSpec((ncp, LANES), lambda bi, i: (0, 0)),
                  pl.BlockSpec((1, nc, KC, LANES), lambda bi, i: (bi, 0, 0, 0)),
                  pl.BlockSpec((1, nc, KC, LANES), lambda bi, i: (bi, 0, 0, 0)),
                  pl.BlockSpec((1, nq, QBLK, LANES), lambda bi, i: (bi, 0, 0, 0))],
        out_specs=pl.BlockSpec((1, qb, BRANCH_WIDTH), lambda bi, i: (bi, i, 0)),
        out_shape=jax.ShapeDtypeStruct((b, seq, BRANCH_WIDTH), BF16),
        compiler_params=_cparams(("parallel", "arbitrary")),
        name="nsa",
    )(q, g, cmp, c2s, kxs, ovs, kvw)


def _swa_kernel(sink_ref, q_ref, kv_ref, o_ref):
    i = pl.program_id(1)
    t0 = i * QBLK
    grp = BRANCH_HEADS // D_KV_HEADS
    rows = grp * QBLK
    cb = jnp.maximum(i - 1, 0)
    tpos = _tile_rows(t0 + lax.broadcasted_iota(jnp.int32, (QBLK, 1), 0), grp)
    kpos = cb * QBLK + lax.broadcasted_iota(jnp.int32, (rows, 2 * QBLK), 1)
    diff = tpos - kpos
    ok = jnp.where(diff >= 0, diff, SWA_WINDOW) < SWA_WINDOW
    for c in range(D_KV_HEADS):
        qb = q_ref[0, c * grp:(c + 1) * grp].reshape(rows, LANES)
        s = jnp.concatenate([lax.dot_general(qb, kv_ref[0, c, cb + r], _NT, preferred_element_type=F32)
                             for r in range(2)], axis=1)
        s = jnp.where(ok, s, NEG)
        sink = jnp.concatenate([jnp.full((QBLK, 1), sink_ref[c * grp + gi], F32) for gi in range(grp)],
                               axis=0)
        m = jnp.maximum(jnp.max(s, axis=-1, keepdims=True), sink)
        e = jnp.where(ok, jnp.exp(s - m), 0.0)
        den = jnp.sum(e, axis=-1, keepdims=True) + jnp.exp(sink - m)
        p = (e / jnp.maximum(den, TINY)).astype(BF16)
        o = jnp.zeros((rows, LANES), F32)
        for r in range(2):
            o = o + jnp.dot(p[:, r * QBLK:(r + 1) * QBLK], kv_ref[0, c, cb + r],
                            preferred_element_type=F32)
        o_ref[0, :, c * LANES:(c + 1) * LANES] = _pair_heads(o[:QBLK], o[QBLK:]).astype(o_ref.dtype)


def _swa(sink, q, kv, seq):
    b = q.shape[0]
    nq = seq // QBLK
    assert BRANCH_HEADS // D_KV_HEADS == 2
    return pl.pallas_call(
        _swa_kernel,
        grid=(b, nq),
        in_specs=[pl.BlockSpec(memory_space=pltpu.SMEM),
                  pl.BlockSpec((1, BRANCH_HEADS, QBLK, LANES), lambda bi, i: (bi, 0, i, 0)),
                  pl.BlockSpec((1, D_KV_HEADS, nq, QBLK, LANES), lambda bi, i: (bi, 0, 0, 0, 0))],
        out_specs=pl.BlockSpec((1, QBLK, BRANCH_WIDTH), lambda bi, i: (bi, i, 0)),
        out_shape=jax.ShapeDtypeStruct((b, seq, BRANCH_WIDTH), BF16),
        compiler_params=_cparams(("parallel", "arbitrary")),
        name="swa",
    )(sink, q, kv)


def _merge_kernel(x_ref, y0_ref, y1_ref, y2_ref, y3_ref, g0_ref, g1_ref, g2_ref, g3_ref, wb_ref, wo_ref,
                  o_ref):
    merged = jnp.zeros(x_ref.shape, F32)
    ys = (y0_ref, y1_ref, y2_ref, y3_ref)
    for n, g_ref in enumerate((g0_ref, g1_ref, g2_ref, g3_ref)):
        br = jnp.dot(ys[n][...], wb_ref[n], preferred_element_type=F32)
        gate = 1.0 / (1.0 + jnp.exp(-g_ref[...]))
        merged = merged + gate * br
    o_ref[...] = x_ref[...] + jnp.dot(merged.astype(BF16), wo_ref[...], preferred_element_type=F32)


def _merge(x2, ys, proj, wb, wo):
    n = x2.shape[0]
    tm = 256
    gb = C_GBR // D_MODEL
    gate_specs = [pl.BlockSpec((tm, D_MODEL), functools.partial(lambda i, k: (i, gb + k), k=k))
                  for k in range(N_BRANCH)]
    return pl.pallas_call(
        _merge_kernel,
        grid=(n // tm,),
        in_specs=[pl.BlockSpec((tm, D_MODEL), lambda i: (i, 0))]
                 + [pl.BlockSpec((tm, BRANCH_WIDTH), lambda i: (i, 0))] * N_BRANCH + gate_specs +
                 [pl.BlockSpec(wb.shape, lambda i: (0, 0, 0)),
                  pl.BlockSpec(wo.shape, lambda i: (0, 0))],
        out_specs=pl.BlockSpec((tm, D_MODEL), lambda i: (i, 0)),
        out_shape=jax.ShapeDtypeStruct((n, D_MODEL), F32),
        compiler_params=_cparams(("parallel",)),
        name="merge",
    )(x2, *ys, proj, proj, proj, proj, wb, wo)


FF_CHUNK = D_FF // 2
FF_HALO = 2 * SUBLANES


def _ffn_kernel(x_ref, xp_ref, g_ref, wg_ref, wv_ref, cw_ref, cb_ref, wd_ref, o_ref, h_ref, acc_ref,
                *, tiles_per_seq):
    i = pl.program_id(0)
    f = pl.program_id(1)
    tm = x_ref.shape[0]

    @pl.when(f == 0)
    def _():
        def norm(x):
            ms = jnp.mean(x * x, axis=-1, keepdims=True)
            return (x * lax.rsqrt(ms + EPS) * g_ref[...]).astype(BF16)
        h_ref[FF_HALO:, :] = norm(x_ref[...])
        h_ref[:FF_HALO, :] = norm(xp_ref[...])
        acc_ref[...] = jnp.zeros(acc_ref.shape, F32)

    h = h_ref[...]
    row = lax.broadcasted_iota(jnp.int32, (tm, 1), 0)
    first_tile = (i % tiles_per_seq) == 0

    def conv(w_ref, half):
        u = jnp.dot(h, w_ref[...], preferred_element_type=F32)
        cw = cw_ref[half, 0]
        out = u[FF_HALO:] * cw[2:3] + cb_ref[half, 0]
        for d in (1, 2):
            prev = pltpu.roll(u, d, 0)[FF_HALO:]
            prev = jnp.where(jnp.logical_and(first_tile, row < d), 0.0, prev)
            out = out + prev * cw[2 - d:3 - d]
        return out

    gate = conv(wg_ref, 0)
    val = conv(wv_ref, 1)
    act = gate / (1.0 + jnp.exp(-gate)) * val
    acc_ref[...] += jnp.dot(act.astype(BF16), wd_ref[...], preferred_element_type=F32)

    @pl.when(f == pl.num_programs(1) - 1)
    def _():
        o_ref[...] = x_ref[...] + acc_ref[...]


def _ffn(x2, g, wup, cw, cb, wd, seq):
    n = x2.shape[0]
    tm = 512
    nf = D_FF // FF_CHUNK
    hb = tm // FF_HALO
    return pl.pallas_call(
        functools.partial(_ffn_kernel, tiles_per_seq=seq // tm),
        grid=(n // tm, nf),
        in_specs=[pl.BlockSpec((tm, D_MODEL), lambda i, f: (i, 0)),
                  pl.BlockSpec((FF_HALO, D_MODEL), lambda i, f: (jnp.maximum(i * hb - 1, 0), 0)),
                  pl.BlockSpec((1, D_MODEL), lambda i, f: (0, 0)),
                  pl.BlockSpec((D_MODEL, FF_CHUNK), lambda i, f: (0, f)),
                  pl.BlockSpec((D_MODEL, FF_CHUNK), lambda i, f: (0, nf + f)),
                  pl.BlockSpec((2, 1, CONV_W, FF_CHUNK), lambda i, f: (0, f, 0, 0)),
                  pl.BlockSpec((2, 1, 1, FF_CHUNK), lambda i, f: (0, f, 0, 0)),
                  pl.BlockSpec((FF_CHUNK, D_MODEL), lambda i, f: (f, 0))],
        out_specs=pl.BlockSpec((tm, D_MODEL), lambda i, f: (i, 0)),
        out_shape=jax.ShapeDtypeStruct((n, D_MODEL), F32),
        scratch_shapes=[pltpu.VMEM((tm + FF_HALO, D_MODEL), BF16), pltpu.VMEM((tm, D_MODEL), F32)],
        compiler_params=_cparams(("parallel", "arbitrary")),
        name="convffn",
    )(x2, x2, g, wup, wup, cw, cb, wd)


def _chunk(a, c):
    lead = a.shape[:-2]
    s = a.shape[-2]
    return a.reshape(lead + (s // c, c, LANES))


def _rope_tables(seq):
    def tab(dim, reps):
        inv = ROPE_THETA ** (-jnp.arange(0, dim, 2, dtype=F32) / dim)
        ang = jnp.arange(seq, dtype=F32)[:, None] * inv[None, :]
        cos, sin = jnp.cos(ang), jnp.sin(ang)
        return (jnp.tile(jnp.concatenate([cos, cos], axis=1), (1, reps)),
                jnp.tile(jnp.concatenate([-sin, sin], axis=1), (1, reps)))
    c64, s64 = tab(HEAD_DIM, 256 // HEAD_DIM)
    c32, s32 = tab(IDX_DIM, 256 // IDX_DIM)
    a64 = np.kron(np.eye(256 // HEAD_DIM), np.full((HEAD_DIM, HEAD_DIM), 1.0 / HEAD_DIM)).astype(np.float32)
    a32 = np.kron(np.eye(LANES // IDX_DIM), np.full((IDX_DIM, IDX_DIM), 1.0 / IDX_DIM)).astype(np.float32)
    return c64, s64, c32, s32, jnp.asarray(a64), jnp.asarray(a32)


def _cmp_to_slc(seq, ncp):
    c_start = np.arange(ncp) * CMP_STRIDE
    s_start = np.arange(LANES) * SLC_BLOCK
    m = ((c_start[:, None] < s_start[None, :] + SLC_BLOCK) & (c_start[:, None] + CMP_LEN > s_start[None, :]))
    real = (np.arange(ncp) < (seq - CMP_LEN) // CMP_STRIDE + 1)[:, None] & (np.arange(LANES) < seq // SLC_BLOCK)[None, :]
    return jnp.asarray((m & real).astype(np.float32))


def _pad_w_in(w):
    z = lambda k: jnp.zeros((w.shape[0], k), w.dtype)
    return jnp.concatenate([w[:, :_O_AIW_END], z(C_BQ - _O_AIW_END), w[:, _O_AIW_END:_O_CG_END],
                            z(C_DQ - C_CG - 3 * BRANCH_HEADS), w[:, _O_CG_END:_O_GBR], z(C_GBR - W_MIX),
                            w[:, _O_GBR:]], axis=1)


def _pad_w_in_t(w):
    wt = w.T
    z = lambda k: jnp.zeros((k, w.shape[0]), w.dtype)
    return jnp.concatenate([wt[:_O_AIW_END], z(C_BQ - _O_AIW_END), wt[_O_AIW_END:_O_CG_END],
                            z(C_DQ - C_CG - 3 * BRANCH_HEADS), wt[_O_CG_END:_O_GBR], z(C_GBR - W_MIX),
                            wt[_O_GBR:]], axis=0)


def _gain_row(a_qk_g, a_lat_g, a_idx_k_g, b_qk_g, c_qk_g, d_qk_g):
    one = lambda k: jnp.ones((k,), F32)
    zero = lambda k: jnp.zeros((k,), F32)
    t4 = lambda g: jnp.tile(g, BRANCH_HEADS)
    row = jnp.concatenate([
        t4(a_qk_g[0]) * ATTN_SCALE, a_lat_g, one(256), a_idx_k_g, zero(LANES - IDX_DIM),
        t4(b_qk_g[0]) * ATTN_SCALE, t4(b_qk_g[1]), one(256),
        t4(c_qk_g[0]) * ATTN_SCALE, one(LANES), c_qk_g[2], one(HEAD_DIM), c_qk_g[3], one(HEAD_DIM), one(LANES),
        t4(d_qk_g[0]) * ATTN_SCALE, jnp.tile(d_qk_g[1], D_KV_HEADS), one(LANES)])
    return row[None, :]


def _mixers(proj, b, s, l, tabs, c2s, a_qk_g, a_lat_g, a_kv_up, a_idx_k_g, b_qk_g, c_qk_g, c_cmp_pe,
            c_cmp_w1, c_cmp_w2, d_qk_g, d_sink):
    gains = _gain_row(a_qk_g[l], a_lat_g[l], a_idx_k_g[l], b_qk_g[l], c_qk_g[l], d_qk_g[l])
    g2 = jnp.concatenate([a_qk_g[l][1], jnp.ones((HEAD_DIM,), F32)])[None, :]
    (qa, iq, ikw, ik, kxa, ova, qb, kxb, ovb, kmean, qc, kvc, kxs, ovs, kvw, gc, qd, kvd) = _prep(
        proj, gains, g2, a_kv_up[l], tabs, b, s)
    flat = lambda y: y.reshape(b * s, BRANCH_WIDTH)
    ya = flat(_dsa(qa, iq, ikw, _chunk(ik, KC), _chunk(kxa, KC), _chunk(ova, KC), s))
    nb = s // MOBA_BLOCK
    km = kmean.reshape(b, nb, BRANCH_HEADS, HEAD_DIM).transpose(0, 2, 1, 3)
    km = jnp.pad(km, ((0, 0), (0, 0), (0, LANES - nb), (0, LANES - HEAD_DIM)))
    kcm = min(MOBA_KC, s)
    yb = flat(_moba(qb, km, _chunk(kxb, kcm), _chunk(ovb, kcm), s))
    nrow = s // CMP_STRIDE
    w1 = c_cmp_w1[l].reshape(2, CMP_LEN, HEAD_DIM, CMP_HIDDEN)
    zw = jnp.zeros_like(w1[0])
    w1 = jnp.concatenate([jnp.concatenate([w1[0], zw], axis=2), jnp.concatenate([zw, w1[1]], axis=2)], axis=1)
    w2 = c_cmp_w2[l]
    w2 = jnp.concatenate([jnp.pad(w2[0], ((0, 0), (0, HEAD_DIM))), jnp.pad(w2[1], ((0, 0), (HEAD_DIM, 0)))], axis=0)
    pe = jnp.concatenate([c_cmp_pe[l][0], c_cmp_pe[l][1]], axis=1)
    gk = jnp.concatenate([c_qk_g[l][1], jnp.ones((HEAD_DIM,), F32)])[None, :]
    cmp = _compress(kvc, w1, pe, w2, gk)
    ncp = c2s.shape[0]
    if ncp > nrow:
        cmp = jnp.pad(cmp, ((0, 0), (0, ncp - nrow), (0, 0)))
    yc = flat(_nsa(qc, gc, cmp, c2s, _chunk(kxs, KC), _chunk(ovs, KC), _chunk(kvw, QBLK), s))
    yd = flat(_swa(d_sink[l], qd, _chunk(kvd, QBLK), s))
    return ya, yb, yc, yd


def kernel(x, norm1_g, w_in, a_qk_g, a_lat_g, a_kv_up, a_idx_k_g, b_qk_g, c_qk_g, c_cmp_pe, c_cmp_w1,
           c_cmp_w2, d_qk_g, d_sink, w_branch, w_out, norm2_g, w_up, conv_w, conv_b, w_down):
    b, s, _ = x.shape
    depth = w_in.shape[0]
    assert s % KC == 0 and s >= NSA_WINDOW + NSA_QB and KC >= min(DSA_TOPK, s // 4)
    tabs = _rope_tables(s)
    ncp = max(LANES, s // CMP_STRIDE)
    c2s = _cmp_to_slc(s, ncp)
    x2 = x.reshape(b * s, D_MODEL)
    nf = D_FF // FF_CHUNK
    for l in range(depth):
        proj = _inproj(x2, norm1_g[l][None, :], _pad_w_in_t(w_in[l]).astype(BF16))
        ys = _mixers(proj, b, s, l, tabs, c2s, a_qk_g, a_lat_g, a_kv_up, a_idx_k_g, b_qk_g, c_qk_g,
                     c_cmp_pe, c_cmp_w1, c_cmp_w2, d_qk_g, d_sink)
        x2 = _merge(x2, ys, proj, w_branch[l].astype(BF16), w_out[l].astype(BF16))
        cw = conv_w[l].reshape(CONV_W, 2, nf, FF_CHUNK).transpose(1, 2, 0, 3)
        cb = conv_b[l].reshape(2, nf, 1, FF_CHUNK)
        x2 = _ffn(x2, norm2_g[l][None, :], w_up[l].astype(BF16), cw, cb, w_down[l].astype(BF16), s)
    return x2.reshape(b, s, D_MODEL)
```

```python
import functools
import math

import numpy as np
import jax
import jax.numpy as jnp
from jax import lax
from jax.experimental import pallas as pl
from jax.experimental.pallas import tpu as pltpu

F32 = jnp.float32
BF16 = jnp.bfloat16
HIGHEST = lax.Precision.HIGHEST

LANES = 128
SUBLANES = 8
VMEM_LIMIT = 48 * 1024 * 1024

D_MODEL = 1024
HEAD_DIM = 64
N_BRANCH = 4
BRANCH_HEADS = 4
BRANCH_WIDTH = BRANCH_HEADS * HEAD_DIM
ROPE_THETA = 10000.0
QBLK = 128
NEG = -1e30
TINY = 1e-30
EPS = 1e-6
ATTN_SCALE = HEAD_DIM ** -0.5
A_LATENT = 128
IDX_HEADS = 8
IDX_DIM = 32
DSA_TOPK = 256
MOBA_BLOCK = 256
MOBA_TOPK = 3
CMP_LEN = 32
CMP_STRIDE = 16
CMP_HIDDEN = 128
SLC_BLOCK = 64
SLC_TOPK = 16
NSA_WINDOW = 512
FORCE_SCORE = 1e9
NSA_KV = 6
SWA_WINDOW = 128
D_KV_HEADS = 2
D_FF = 256 * ((8 * D_MODEL // 3 + 255) // 256)
CONV_W = 3
IW_SCALE = IDX_HEADS ** -0.5 * IDX_DIM ** -0.5
LOG2E = math.log2(math.e)
INT_MIN = -2 ** 31

_O_AIW_END = 680
_O_CG_END = 2100
_O_GBR = 2612
C_AQ, C_ALAT, C_AIQ, C_AIKW = 0, 256, 384, 640
C_BQ, C_BK, C_BV = 768, 1024, 1280
C_CQ, C_CKV, C_CG = 1536, 1792, 2176
C_DQ, C_DK, C_DV = 2304, 2560, 2688
W_MIX = 2816
C_GBR = 3072
W_TOT = C_GBR + N_BRANCH * D_MODEL

KC = 512


def _cparams(sem):
    return pltpu.CompilerParams(dimension_semantics=sem, vmem_limit_bytes=VMEM_LIMIT)


def _inproj_kernel(x_ref, g_ref, w_ref, o_ref, h_ref):
    @pl.when(pl.program_id(1) == 0)
    def _():
        x = x_ref[...]
        ms = jnp.mean(x * x, axis=-1, keepdims=True)
        h_ref[...] = (x * lax.rsqrt(ms + EPS) * g_ref[...]).astype(h_ref.dtype)

    o_ref[...] = lax.dot_general(h_ref[...], w_ref[...], (((1,), (1,)), ((), ())),
                                 preferred_element_type=F32)


def _inproj(x2, g, wt):
    n = x2.shape[0]
    tm, tn = 512, 1024
    return pl.pallas_call(
        _inproj_kernel,
        grid=(n // tm, W_TOT // tn),
        in_specs=[pl.BlockSpec((tm, D_MODEL), lambda i, j: (i, 0)),
                  pl.BlockSpec((1, D_MODEL), lambda i, j: (0, 0)),
                  pl.BlockSpec((tn, D_MODEL), lambda i, j: (j, 0))],
        out_specs=pl.BlockSpec((tm, tn), lambda i, j: (i, j)),
        out_shape=jax.ShapeDtypeStruct((n, W_TOT), F32),
        scratch_shapes=[pltpu.VMEM((tm, D_MODEL), BF16)],
        compiler_params=_cparams(("parallel", "arbitrary")),
        name="inproj",
    )(x2, g, wt)


def _rope(x, cosw, sinw, half):
    w = x.shape[1]
    lane = lax.broadcasted_iota(jnp.int32, x.shape, 1)
    first = (lane % (2 * half)) < half
    rot = jnp.where(first, pltpu.roll(x, w - half, 1), pltpu.roll(x, half, 1))
    return x * cosw + rot * sinw


def _gnorm(x, avg, gain):
    ms = jnp.dot(x * x, avg, precision=HIGHEST, preferred_element_type=F32)
    return x * lax.rsqrt(ms + EPS) * gain


def _prep_kernel(p_ref, g_ref, g2_ref, kvup_ref, c64_ref, s64_ref, c32_ref, s32_ref, a64_ref, a32_ref,
                 qa_ref, iq_ref, ikw_ref, ik_ref, kxa_ref, ova_ref,
                 qb_ref, kxb_ref, ovb_ref, km_ref,
                 qc_ref, kvc_ref, kxs_ref, ovs_ref, kvw_ref, gc_ref,
                 qd_ref, kvd_ref):
    c64, s64 = c64_ref[...], s64_ref[...]
    c32, s32 = c32_ref[...], s32_ref[...]
    a64, a32 = a64_ref[...], a32_ref[...]
    c64h, s64h, a64h = c64[:, :LANES], s64[:, :LANES], a64[:LANES, :LANES]
    tm = p_ref.shape[0]
    lane1 = lax.broadcasted_iota(jnp.int32, (tm, LANES), 1)
    lo64 = lane1 < HEAD_DIM

    def seg(c, w):
        return p_ref[:, c:c + w], g_ref[:, c:c + w]

    def normrope256(c):
        x, g = seg(c, 256)
        return _rope(_gnorm(x, a64, g), c64, s64, HEAD_DIM // 2)

    def swap(x):
        return pltpu.roll(x, HEAD_DIM, 1)

    def put_heads(q, ref):
        for p in range(BRANCH_HEADS // 2):
            pair = q[:, p * LANES:(p + 1) * LANES]
            ref[0, 2 * p] = jnp.where(lo64, pair, 0.0).astype(ref.dtype)
            ref[0, 2 * p + 1] = jnp.where(lo64, swap(pair), 0.0).astype(ref.dtype)

    def put_kv(kv, kx_ref, ov_ref, idx):
        kx_ref[idx] = jnp.where(lo64, kv * LOG2E, kv).astype(BF16)
        ov_ref[idx] = jnp.where(lo64, 1.0, kv).astype(BF16)

    put_heads(normrope256(C_AQ), qa_ref)
    x, g = seg(C_ALAT, A_LATENT)
    ms = jnp.mean(x * x, axis=-1, keepdims=True)
    latn = x * lax.rsqrt(ms + EPS) * g
    kv = jnp.dot(latn, kvup_ref[...], precision=HIGHEST, preferred_element_type=F32)
    kr = _rope(_gnorm(kv, a64h, g2_ref[...]), c64h, s64h, HEAD_DIM // 2)
    put_kv(jnp.where(lo64, kr, kv), kxa_ref, ova_ref, 0)
    x, _ = seg(C_AIQ, 256)
    iq = _rope(x, c32, s32, IDX_DIM // 2)
    per = LANES // IDX_DIM
    for h in range(IDX_HEADS):
        tile = iq[:, (h // per) * LANES:(h // per + 1) * LANES]
        if h % per:
            tile = pltpu.roll(tile, LANES - (h % per) * IDX_DIM, 1)
        iq_ref[0, h] = jnp.where(lane1 < IDX_DIM, tile, 0.0).astype(BF16)
    x, g = seg(C_AIKW, LANES)
    xr = _rope(_gnorm(x, a32, g), c32[:, :LANES], s32[:, :LANES], IDX_DIM // 2)
    ikw_ref[0] = jnp.where(lane1 < IDX_DIM, xr, x * IW_SCALE)
    ik_ref[0] = jnp.where(lane1 < IDX_DIM, xr, 0.0).astype(BF16)
    put_heads(normrope256(C_BQ), qb_ref)
    kb = normrope256(C_BK)
    km_ref[0] = jnp.mean(kb, axis=0, keepdims=True)
    for p in range(BRANCH_HEADS // 2):
        kp = kb[:, p * LANES:(p + 1) * LANES]
        vp = p_ref[:, C_BV + p * LANES:C_BV + (p + 1) * LANES]
        put_kv(jnp.where(lo64, kp, swap(vp)), kxb_ref, ovb_ref, (0, 2 * p))
        put_kv(jnp.where(lo64, swap(kp), vp), kxb_ref, ovb_ref, (0, 2 * p + 1))
    put_heads(normrope256(C_CQ), qc_ref)
    for r in range(3):
        c = C_CKV + r * LANES
        x, g = seg(c, LANES)
        xn = x if r == 0 else _gnorm(x, a64h, g)
        kv = jnp.where(lo64, _rope(xn, c64h, s64h, HEAD_DIM // 2), x)
        if r == 0:
            kvc_ref[0] = kv
        elif r == 1:
            put_kv(kv, kxs_ref, ovs_ref, 0)
        else:
            kvw_ref[0] = kv.astype(BF16)
    x, _ = seg(C_CG, LANES)
    gc_ref[0] = 1.0 / (1.0 + jnp.exp(-x))
    put_heads(normrope256(C_DQ), qd_ref)
    x, g = seg(C_DK, LANES)
    kd = _rope(_gnorm(x, a64h, g), c64h, s64h, HEAD_DIM // 2)
    vd = p_ref[:, C_DV:C_DV + LANES]
    kvd_ref[0, 0] = jnp.where(lo64, kd, swap(vd)).astype(BF16)
    kvd_ref[0, 1] = jnp.where(lo64, swap(kd), vd).astype(BF16)


def _prep(proj, gains, g2, kvup, tabs, b, seq):
    n = proj.shape[0]
    tm = MOBA_BLOCK
    nt = seq // tm
    c64, s64, c32, s32, a64, a32 = tabs
    tab_spec = pl.BlockSpec((tm, 256), lambda i: (i % nt, 0))
    full = lambda a: pl.BlockSpec(a.shape, lambda i: (0,) * a.ndim)
    tok = lambda dt: (jax.ShapeDtypeStruct((b, seq, LANES), dt),
                      pl.BlockSpec((1, tm, LANES), lambda i: (i // nt, i % nt, 0)))
    hd = lambda nh, dt: (jax.ShapeDtypeStruct((b, nh, seq, LANES), dt),
                         pl.BlockSpec((1, nh, tm, LANES), lambda i: (i // nt, 0, i % nt, 0)))
    outs = [hd(BRANCH_HEADS, BF16), hd(IDX_HEADS, BF16), tok(F32), tok(BF16), tok(BF16), tok(BF16),
            hd(BRANCH_HEADS, F32), hd(BRANCH_HEADS, BF16), hd(BRANCH_HEADS, BF16),
            (jax.ShapeDtypeStruct((n // tm, 1, 256), F32), pl.BlockSpec((1, 1, 256), lambda i: (i, 0, 0))),
            hd(BRANCH_HEADS, F32), tok(F32), tok(BF16), tok(BF16), tok(BF16), tok(F32),
            hd(BRANCH_HEADS, BF16), hd(D_KV_HEADS, BF16)]
    return pl.pallas_call(
        _prep_kernel,
        grid=(n // tm,),
        in_specs=[pl.BlockSpec((tm, W_MIX), lambda i: (i, 0)), full(gains), full(g2), full(kvup),
                  tab_spec, tab_spec, tab_spec, tab_spec, full(a64), full(a32)],
        out_specs=[o[1] for o in outs],
        out_shape=[o[0] for o in outs],
        compiler_params=_cparams(("parallel",)),
        name="prep",
    )(proj, gains, g2, kvup, c64, s64, c32, s32, a64, a32)


def _gelu_tanh(x):
    return 0.5 * x * (1.0 + jnp.tanh(math.sqrt(2.0 / math.pi) * (x + 0.044715 * (x * x * x))))


def _cmp_kernel(x_ref, w1_ref, pe_ref, w2_ref, g_ref, o_ref):
    nrow = o_ref.shape[1]
    u = jnp.zeros((nrow, 2 * CMP_HIDDEN), F32)
    v = jnp.zeros((nrow, 2 * CMP_HIDDEN), F32)
    for l in range(CMP_STRIDE):
        xl = x_ref[0, pl.ds(l, nrow, stride=CMP_STRIDE), :]
        u = u + jnp.dot(xl + pe_ref[l:l + 1, :], w1_ref[l], precision=HIGHEST, preferred_element_type=F32)
        v = v + jnp.dot(xl + pe_ref[CMP_STRIDE + l:CMP_STRIDE + l + 1, :], w1_ref[CMP_STRIDE + l],
                        precision=HIGHEST, preferred_element_type=F32)
    pre = u + pltpu.roll(v, nrow - 1, 0)
    out = jnp.dot(_gelu_tanh(pre), w2_ref[...], precision=HIGHEST, preferred_element_type=F32)
    lane = lax.broadcasted_iota(jnp.int32, out.shape, 1)
    lo = lane < HEAD_DIM
    ms = jnp.sum(jnp.where(lo, out * out, 0.0), axis=-1, keepdims=True) * (1.0 / HEAD_DIM)
    o_ref[0] = jnp.where(lo, out * lax.rsqrt(ms + EPS) * g_ref[...], out)


def _compress(kvc, w1, pe, w2, gk):
    b, seq, _ = kvc.shape
    nrow = seq // CMP_STRIDE
    full = lambda a: pl.BlockSpec(a.shape, lambda i: (0,) * a.ndim)
    return pl.pallas_call(
        _cmp_kernel,
        grid=(b,),
        in_specs=[pl.BlockSpec((1, seq, LANES), lambda i: (i, 0, 0)), full(w1), full(pe), full(w2), full(gk)],
        out_specs=pl.BlockSpec((1, nrow, LANES), lambda i: (i, 0, 0)),
        out_shape=jax.ShapeDtypeStruct((b, nrow, LANES), F32),
        compiler_params=_cparams(("parallel",)),
        name="nsa_compress",
    )(kvc, w1, pe, w2, gk)


def _tile_rows(x, n):
    return jnp.concatenate([x] * n, axis=0)


def _online_step(carry, s, ov):
    m, acc = carry
    m_new = jnp.maximum(m, jnp.max(s, axis=-1, keepdims=True))
    e = jnp.exp2(s - m_new)
    acc = jnp.exp2(m - m_new) * acc + jnp.dot(e.astype(BF16), ov, preferred_element_type=F32)
    return m_new, acc


def _online_init(rows):
    return (jnp.full((rows, 1), NEG, F32), jnp.zeros((rows, LANES), F32))


def _online_finish(carry):
    _, acc = carry
    den = jnp.maximum(pltpu.roll(acc, HEAD_DIM, 1), TINY)
    lane = lax.broadcasted_iota(jnp.int32, acc.shape, 1)
    return jnp.where(lane >= HEAD_DIM, acc / den, 0.0)


def _top_lowest_index(work, jj, k):
    picks = jnp.zeros(work.shape, F32)
    jf = jj.astype(F32)
    big = float(work.shape[1])
    for _ in range(k):
        mx = jnp.max(work, axis=-1, keepdims=True)
        idx = jnp.min(jnp.where(work == mx, jf, big), axis=-1, keepdims=True)
        pick = jf == idx
        picks = jnp.where(pick, 1.0, picks)
        work = jnp.where(pick, -jnp.inf, work)
    return picks


def _split_bf16(a):
    hi = a.astype(BF16)
    return hi, (a - hi.astype(F32)).astype(BF16)


_NT = (((1,), (1,)), ((), ()))


def _pair_heads(o_even, o_odd):
    lane = lax.broadcasted_iota(jnp.int32, o_even.shape, 1)
    return jnp.where(lane < HEAD_DIM, pltpu.roll(o_even, HEAD_DIM, 1), o_odd)


def _dsa_kernel(q_ref, iq_ref, iw_ref, ik_ref, kx_ref, ov_ref, o_ref, sc_ref, *, seq, topk):
    i = pl.program_id(1)
    rows = QBLK
    t0 = i * rows
    nck = (t0 + rows + KC - 1) // KC
    tq = t0 + lax.broadcasted_iota(jnp.int32, (1, LANES), 1)
    krow = lax.broadcasted_iota(jnp.int32, (KC, LANES), 0)
    iq = iq_ref[0].reshape(IDX_HEADS * rows, LANES)
    pick = jnp.where(lax.broadcasted_iota(jnp.int32, (SUBLANES, LANES), 1)
                     == lax.broadcasted_iota(jnp.int32, (SUBLANES, LANES), 0) + IDX_DIM, 1.0, 0.0)
    iw = lax.dot_general(pick, iw_ref[0], _NT, precision=HIGHEST, preferred_element_type=F32)
    ftop = float(topk)

    def score_chunk(c, carry):
        d = lax.dot_general(ik_ref[0, c], iq, _NT, preferred_element_type=F32)
        acc = jnp.zeros((KC, LANES), F32)
        for h in range(IDX_HEADS):
            acc = acc + jnp.maximum(d[:, h * LANES:(h + 1) * LANES], 0.0) * iw[h:h + 1, :]
        acc = jnp.where(c * KC + krow <= tq, acc, NEG)
        bits = lax.bitcast_convert_type(acc, jnp.int32)
        sc_ref[c] = bits ^ ((bits >> 31) & 0x7FFFFFFF)
        return carry

    lax.fori_loop(0, nck, score_chunk, 0)

    def count(pred):
        def body(c, acc):
            hit = jnp.where(pred(sc_ref[c], c * KC + krow), 1.0, 0.0)
            while hit.shape[0] > SUBLANES:
                half = hit.shape[0] // 2
                hit = hit[:half] + hit[half:]
            return acc + hit
        acc = lax.fori_loop(0, nck, body, jnp.zeros((SUBLANES, LANES), F32))
        return jnp.sum(acc, axis=0, keepdims=True)

    zero = jnp.zeros((1, LANES), jnp.int32)
    thr = jnp.where(count(lambda k, idx: k >= zero) >= ftop, zero, jnp.full_like(zero, INT_MIN))

    def bit_step(b, thr):
        cand = thr | lax.shift_left(jnp.int32(1), 30 - b)
        return jnp.where(count(lambda k, idx: k >= cand) >= ftop, cand, thr)

    thr = lax.fori_loop(0, 31, bit_step, thr)
    need = ftop - count(lambda k, idx: k > thr)
    n_eq = count(lambda k, idx: k == thr)
    nbits = max(1, int(math.ceil(math.log2(seq))))

    def tie_search(_):
        def jstep(b, jb):
            cand = jb | lax.shift_left(jnp.int32(1), nbits - 1 - b)
            c = count(lambda k, idx: jnp.where(k == thr, idx, seq + cand) < cand)
            return jnp.where(c < need, cand, jb)
        return lax.fori_loop(0, nbits, jstep, zero)

    jbound = lax.cond(jnp.max(n_eq - need) > 0.0, tie_search, lambda _: jnp.full_like(zero, seq), 0)

    q = q_ref[0].reshape(BRANCH_HEADS * rows, LANES)
    eye = jnp.where(lax.broadcasted_iota(jnp.int32, (LANES, LANES), 0)
                    == lax.broadcasted_iota(jnp.int32, (LANES, LANES), 1), 1.0, 0.0).astype(BF16)

    def attend(c, carry):
        k = sc_ref[c]
        idx = c * KC + krow
        tie_ok = jnp.where(k == thr, idx, seq + jbound) <= jbound
        sel = jnp.where(k > thr, idx, jnp.where(tie_ok, idx, seq + tq)) <= tq
        sel_t = jnp.where(sel, 1.0, 0.0).astype(BF16)
        hit = lax.dot_general(eye, sel_t, _NT, preferred_element_type=F32)
        bias = jnp.where(hit > 0.5, 0.0, -jnp.inf)
        s = lax.dot_general(q, kx_ref[0, c], _NT, preferred_element_type=F32)
        s = (s.reshape(BRANCH_HEADS, rows, KC) + bias[None]).reshape(BRANCH_HEADS * rows, KC)
        return _online_step(carry, s, ov_ref[0, c])

    out = _online_finish(lax.fori_loop(0, nck, attend, _online_init(BRANCH_HEADS * rows)))
    for p in range(BRANCH_HEADS // 2):
        o_ref[0, :, p * LANES:(p + 1) * LANES] = _pair_heads(
            out[2 * p * rows:(2 * p + 1) * rows], out[(2 * p + 1) * rows:(2 * p + 2) * rows]).astype(o_ref.dtype)


def _dsa(q, iq, ikw, ik, kx, ov, seq):
    b = q.shape[0]
    nq = seq // QBLK
    nc = seq // KC
    topk = min(DSA_TOPK, seq // 4)
    return pl.pallas_call(
        functools.partial(_dsa_kernel, seq=seq, topk=topk),
        grid=(b, nq),
        in_specs=[pl.BlockSpec((1, BRANCH_HEADS, QBLK, LANES), lambda bi, i: (bi, 0, i, 0)),
                  pl.BlockSpec((1, IDX_HEADS, QBLK, LANES), lambda bi, i: (bi, 0, i, 0)),
                  pl.BlockSpec((1, QBLK, LANES), lambda bi, i: (bi, i, 0)),
                  pl.BlockSpec((1, nc, KC, LANES), lambda bi, i: (bi, 0, 0, 0)),
                  pl.BlockSpec((1, nc, KC, LANES), lambda bi, i: (bi, 0, 0, 0)),
                  pl.BlockSpec((1, nc, KC, LANES), lambda bi, i: (bi, 0, 0, 0))],
        out_specs=pl.BlockSpec((1, QBLK, BRANCH_WIDTH), lambda bi, i: (bi, i, 0)),
        out_shape=jax.ShapeDtypeStruct((b, seq, BRANCH_WIDTH), BF16),
        scratch_shapes=[pltpu.VMEM((nc, KC, QBLK), jnp.int32)],
        compiler_params=_cparams(("parallel", "arbitrary")),
        name="dsa",
    )(q, iq, ikw, ik, kx, ov)


def _moba_kernel(q_ref, km_ref, kx_ref, ov_ref, o_ref, *, seq):
    own = pl.program_id(2)
    rows = MOBA_BLOCK
    jj = lax.broadcasted_iota(jnp.int32, (rows, LANES), 1)
    kcm = kx_ref.shape[3]
    per = kcm // MOBA_BLOCK
    blk_row = lax.broadcasted_iota(jnp.int32, (LANES, kcm), 0)
    blk_col = lax.broadcasted_iota(jnp.int32, (LANES, kcm), 1) // MOBA_BLOCK
    col = lax.broadcasted_iota(jnp.int32, (rows, kcm), 1)
    own0 = own * MOBA_BLOCK
    tq = own0 + lax.broadcasted_iota(jnp.int32, (rows, 1), 0)
    picks, qbs = [], []
    for hh in range(2):
        qf = q_ref[0, hh]
        gate = lax.dot_general(qf, km_ref[0, hh], _NT, precision=HIGHEST, preferred_element_type=F32)
        gate = jnp.where(jj < own, gate, NEG)
        pk = _top_lowest_index(gate, jj, MOBA_TOPK)
        picks.append(jnp.where(jj < own, pk, 0.0).astype(BF16))
        qbs.append(qf.astype(BF16))

    def group(g, carry):
        expand = jnp.where(blk_row == blk_col + g * per, 1.0, 0.0).astype(BF16)
        kpos = g * kcm + col
        new = []
        for hh in range(2):
            hit = jnp.dot(picks[hh], expand, preferred_element_type=F32) > 0.5
            ok = jnp.where(hit, kpos, jnp.where(kpos >= own0, kpos, seq + tq)) <= tq
            s = lax.dot_general(qbs[hh], kx_ref[0, hh, g], _NT, preferred_element_type=F32)
            new.append(_online_step(carry[hh], jnp.where(ok, s, -jnp.inf), ov_ref[0, hh, g]))
        return tuple(new)

    done = lax.fori_loop(0, own // per + 1, group, (_online_init(rows), _online_init(rows)))
    o_ref[0] = _pair_heads(_online_finish(done[0]), _online_finish(done[1])).astype(o_ref.dtype)


MOBA_KC = 4 * MOBA_BLOCK


def _moba(q, km, kx, ov, seq):
    b = q.shape[0]
    nb = seq // MOBA_BLOCK
    kcm = min(MOBA_KC, seq)
    ng = seq // kcm
    return pl.pallas_call(
        functools.partial(_moba_kernel, seq=seq),
        grid=(b, BRANCH_HEADS // 2, nb),
        in_specs=[pl.BlockSpec((1, 2, MOBA_BLOCK, LANES), lambda bi, p, i: (bi, p, i, 0)),
                  pl.BlockSpec((1, 2, LANES, LANES), lambda bi, p, i: (bi, p, 0, 0)),
                  pl.BlockSpec((1, 2, ng, kcm, LANES), lambda bi, p, i: (bi, p, 0, 0, 0)),
                  pl.BlockSpec((1, 2, ng, kcm, LANES), lambda bi, p, i: (bi, p, 0, 0, 0))],
        out_specs=pl.BlockSpec((1, MOBA_BLOCK, LANES), lambda bi, p, i: (bi, i, p)),
        out_shape=jax.ShapeDtypeStruct((b, seq, BRANCH_WIDTH), BF16),
        compiler_params=_cparams(("parallel", "parallel", "arbitrary")),
        name="moba",
    )(q, km, kx, ov)


def _nsa_kernel(q_ref, g_ref, cmp_ref, c2s_ref, kxs_ref, ovs_ref, kvw_ref,
                o_ref, *, seq):
    i = pl.program_id(1)
    qr = q_ref.shape[2]
    t0 = i * qr
    rows = BRANCH_HEADS * qr
    qf = q_ref[0].reshape(rows, LANES)
    qb = qf.astype(BF16)
    tpos = t0 + lax.broadcasted_iota(jnp.int32, (qr, 1), 0)
    tpos4 = _tile_rows(tpos, BRANCH_HEADS)

    ncp = cmp_ref.shape[1]
    q_hi, q_lo = _split_bf16(qf)
    c_hi, c_lo = _split_bf16(cmp_ref[0])
    s = (lax.dot_general(q_hi, c_hi, _NT, preferred_element_type=F32)
         + lax.dot_general(q_hi, c_lo, _NT, preferred_element_type=F32)
         + lax.dot_general(q_lo, c_hi, _NT, preferred_element_type=F32))
    cend = lax.broadcasted_iota(jnp.int32, (rows, ncp), 1) * CMP_STRIDE + (CMP_LEN - 1)
    ok = cend <= tpos4
    s = jnp.where(ok, s, NEG)
    e = jnp.where(ok, jnp.exp(s - jnp.max(s, axis=-1, keepdims=True)), 0.0)
    p = e / jnp.maximum(jnp.sum(e, axis=-1, keepdims=True), TINY)
    o_c = jnp.dot(p.astype(BF16), c_hi, preferred_element_type=F32)
    psum = p[0:qr]
    for h in range(1, BRANCH_HEADS):
        psum = psum + p[h * qr:(h + 1) * qr]
    p_hi, p_lo = _split_bf16(psum)
    c2s = c2s_ref[...].astype(BF16)
    imp = (jnp.dot(p_hi, c2s, preferred_element_type=F32)
           + jnp.dot(p_lo, c2s, preferred_element_type=F32))

    jj = lax.broadcasted_iota(jnp.int32, (qr, LANES), 1)
    cur = tpos // SLC_BLOCK
    forced = (jj == 0) | (jj == cur) | (jj == cur - 1)
    imp = jnp.where(forced, FORCE_SCORE, imp)
    imp = jnp.where(jj <= cur, imp, NEG)
    imp = jnp.where(jj < seq // SLC_BLOCK, imp, -jnp.inf)
    ntop = min(SLC_TOPK, seq // SLC_BLOCK)
    picks = _top_lowest_index(imp, jj, ntop)
    picks = jnp.where(jj <= cur, picks, 0.0).astype(BF16)

    nck = (t0 + qr + KC - 1) // KC
    blk_row = lax.broadcasted_iota(jnp.int32, (LANES, KC), 0)
    blk_col = lax.broadcasted_iota(jnp.int32, (LANES, KC), 1) // SLC_BLOCK
    lane = lax.broadcasted_iota(jnp.int32, (qr, KC), 1)

    def slc(c, carry):
        expand = jnp.where(blk_row == blk_col + c * (KC // SLC_BLOCK), 1.0, 0.0).astype(BF16)
        hit = jnp.dot(picks, expand, preferred_element_type=F32) > 0.5
        ok = jnp.where(hit, c * KC + lane, seq + tpos) <= tpos
        bias = jnp.where(ok, 0.0, -jnp.inf)
        sc = lax.dot_general(qb, kxs_ref[0, c], _NT, preferred_element_type=F32)
        sc = (sc.reshape(BRANCH_HEADS, qr, KC) + bias[None]).reshape(rows, KC)
        return _online_step(carry, sc, ovs_ref[0, c])

    o_s = _online_finish(lax.fori_loop(0, nck, slc, _online_init(rows)))

    nwin = (NSA_WINDOW + qr) // QBLK
    cb = jnp.maximum(i * (qr // QBLK) - NSA_WINDOW // QBLK, 0)
    sw = jnp.concatenate([lax.dot_general(qb, kvw_ref[0, cb + r], _NT, preferred_element_type=F32)
                          for r in range(nwin)], axis=1)
    kpos = cb * QBLK + lax.broadcasted_iota(jnp.int32, (rows, nwin * QBLK), 1)
    diff = tpos4 - kpos
    ok = jnp.where(diff >= 0, diff, NSA_WINDOW) < NSA_WINDOW
    sw = jnp.where(ok, sw, NEG)
    e = jnp.where(ok, jnp.exp(sw - jnp.max(sw, axis=-1, keepdims=True)), 0.0)
    pw = (e / jnp.maximum(jnp.sum(e, axis=-1, keepdims=True), TINY)).astype(BF16)
    o_w = jnp.zeros((rows, LANES), F32)
    for r in range(nwin):
        o_w = o_w + jnp.dot(pw[:, r * QBLK:(r + 1) * QBLK], kvw_ref[0, cb + r],
                            preferred_element_type=F32)

    g = g_ref[0]
    heads = []
    for h in range(BRANCH_HEADS):
        sl = slice(h * qr, (h + 1) * qr)
        heads.append(g[:, 3 * h:3 * h + 1] * o_c[sl] + g[:, 3 * h + 1:3 * h + 2] * o_s[sl]
                     + g[:, 3 * h + 2:3 * h + 3] * o_w[sl])
    for p in range(BRANCH_HEADS // 2):
        o_ref[0, :, p * LANES:(p + 1) * LANES] = _pair_heads(heads[2 * p], heads[2 * p + 1]).astype(o_ref.dtype)


NSA_QB = 256


def _nsa(q, g, cmp, c2s, kxs, ovs, kvw, seq):
    b = q.shape[0]
    qb = NSA_QB
    nq = seq // QBLK
    nc = seq // KC
    ncp = cmp.shape[1]
    return pl.pallas_call(
        functools.partial(_nsa_kernel, seq=seq),
        grid=(b, seq // qb),
        in_specs=[pl.BlockSpec((1, BRANCH_HEADS, qb, LANES), lambda bi, i: (bi, 0, i, 0)),
                  pl.BlockSpec((1, qb, LANES), lambda bi, i: (bi, i, 0)),
                  pl.BlockSpec((1, ncp, LANES), lambda bi, i: (bi, 0, 0)),
                  pl.BlockSpec((ncp, LANES), lambda bi, i: (0, 0)),
                  pl.BlockSpec((1, nc, KC, LANES), lambda bi, i: (bi, 0, 0, 0)),
                  pl.BlockSpec((1, nc, KC, LANES), lambda bi, i: (bi, 0, 0, 0)),
                  pl.BlockSpec((1, nq, QBLK, LANES), lambda bi, i: (bi, 0, 0, 0))],
        out_specs=pl.BlockSpec((1, qb, BRANCH_WIDTH), lambda bi, i: (bi, i, 0)),
        out_shape=jax.ShapeDtypeStruct((b, seq, BRANCH_WIDTH), BF16),
        compiler_params=_cparams(("parallel", "arbitrary")),
        name="nsa",
    )(q, g, cmp, c2s, kxs, ovs, kvw)


def _swa_kernel(sink_ref, q_ref, kv_ref, o_ref):
    i = pl.program_id(1)
    t0 = i * QBLK
    grp = BRANCH_HEADS // D_KV_HEADS
    rows = grp * QBLK
    cb = jnp.maximum(i - 1, 0)
    tpos = _tile_rows(t0 + lax.broadcasted_iota(jnp.int32, (QBLK, 1), 0), grp)
    kpos = cb * QBLK + lax.broadcasted_iota(jnp.int32, (rows, 2 * QBLK), 1)
    diff = tpos - kpos
    ok = jnp.where(diff >= 0, diff, SWA_WINDOW) < SWA_WINDOW
    for c in range(D_KV_HEADS):
        qb = q_ref[0, c * grp:(c + 1) * grp].reshape(rows, LANES)
        s = jnp.concatenate([lax.dot_general(qb, kv_ref[0, c, cb + r], _NT, preferred_element_type=F32)
                             for r in range(2)], axis=1)
        s = jnp.where(ok, s, NEG)
        sink = jnp.concatenate([jnp.full((QBLK, 1), sink_ref[c * grp + gi], F32) for gi in range(grp)],
                               axis=0)
        m = jnp.maximum(jnp.max(s, axis=-1, keepdims=True), sink)
        e = jnp.where(ok, jnp.exp(s - m), 0.0)
        den = jnp.sum(e, axis=-1, keepdims=True) + jnp.exp(sink - m)
        p = (e / jnp.maximum(den, TINY)).astype(BF16)
        o = jnp.zeros((rows, LANES), F32)
        for r in range(2):
            o = o + jnp.dot(p[:, r * QBLK:(r + 1) * QBLK], kv_ref[0, c, cb + r],
                            preferred_element_type=F32)
        o_ref[0, :, c * LANES:(c + 1) * LANES] = _pair_heads(o[:QBLK], o[QBLK:]).astype(o_ref.dtype)


def _swa(sink, q, kv, seq):
    b = q.shape[0]
    nq = seq // QBLK
    assert BRANCH_HEADS // D_KV_HEADS == 2
    return pl.pallas_call(
        _swa_kernel,
        grid=(b, nq),
        in_specs=[pl.BlockSpec(memory_space=pltpu.SMEM),
                  pl.BlockSpec((1, BRANCH_HEADS, QBLK, LANES), lambda bi, i: (bi, 0, i, 0)),
                  pl.BlockSpec((1, D_KV_HEADS, nq, QBLK, LANES), lambda bi, i: (bi, 0, 0, 0, 0))],
        out_specs=pl.BlockSpec((1, QBLK, BRANCH_WIDTH), lambda bi, i: (bi, i, 0)),
        out_shape=jax.ShapeDtypeStruct((b, seq, BRANCH_WIDTH), BF16),
        compiler_params=_cparams(("parallel", "arbitrary")),
        name="swa",
    )(sink, q, kv)


def _merge_kernel(x_ref, y0_ref, y1_ref, y2_ref, y3_ref, g0_ref, g1_ref, g2_ref, g3_ref, wb_ref, wo_ref,
                  o_ref):
    merged = jnp.zeros(x_ref.shape, F32)
    ys = (y0_ref, y1_ref, y2_ref, y3_ref)
    for n, g_ref in enumerate((g0_ref, g1_ref, g2_ref, g3_ref)):
        br = jnp.dot(ys[n][...], wb_ref[n], preferred_element_type=F32)
        gate = 1.0 / (1.0 + jnp.exp(-g_ref[...]))
        merged = merged + gate * br
    o_ref[...] = x_ref[...] + jnp.dot(merged.astype(BF16), wo_ref[...], preferred_element_type=F32)


def _merge(x2, ys, proj, wb, wo):
    n = x2.shape[0]
    tm = 256
    gb = C_GBR // D_MODEL
    gate_specs = [pl.BlockSpec((tm, D_MODEL), functools.partial(lambda i, k: (i, gb + k), k=k))
                  for k in range(N_BRANCH)]
    return pl.pallas_call(
        _merge_kernel,
        grid=(n // tm,),
        in_specs=[pl.BlockSpec((tm, D_MODEL), lambda i: (i, 0))]
                 + [pl.BlockSpec((tm, BRANCH_WIDTH), lambda i: (i, 0))] * N_BRANCH + gate_specs +
                 [pl.BlockSpec(wb.shape, lambda i: (0, 0, 0)),
                  pl.BlockSpec(wo.shape, lambda i: (0, 0))],
        out_specs=pl.BlockSpec((tm, D_MODEL), lambda i: (i, 0)),
        out_shape=jax.ShapeDtypeStruct((n, D_MODEL), F32),
        compiler_params=_cparams(("parallel",)),
        name="merge",
    )(x2, *ys, proj, proj, proj, proj, wb, wo)


FF_CHUNK = D_FF // 2
FF_HALO = 2 * SUBLANES


def _ffn_kernel(x_ref, xp_ref, g_ref, wg_ref, wv_ref, cw_ref, cb_ref, wd_ref, o_ref, h_ref, acc_ref,
                *, tiles_per_seq):
    i = pl.program_id(0)
    f = pl.program_id(1)
    tm = x_ref.shape[0]

    @pl.when(f == 0)
    def _():
        def norm(x):
            ms = jnp.mean(x * x, axis=-1, keepdims=True)
            return (x * lax.rsqrt(ms + EPS) * g_ref[...]).astype(BF16)
        h_ref[FF_HALO:, :] = norm(x_ref[...])
        h_ref[:FF_HALO, :] = norm(xp_ref[...])
        acc_ref[...] = jnp.zeros(acc_ref.shape, F32)

    h = h_ref[...]
    row = lax.broadcasted_iota(jnp.int32, (tm, 1), 0)
    first_tile = (i % tiles_per_seq) == 0

    def conv(w_ref, half):
        u = jnp.dot(h, w_ref[...], preferred_element_type=F32)
        cw = cw_ref[half, 0]
        out = u[FF_HALO:] * cw[2:3] + cb_ref[half, 0]
        for d in (1, 2):
            prev = pltpu.roll(u, d, 0)[FF_HALO:]
            prev = jnp.where(jnp.logical_and(first_tile, row < d), 0.0, prev)
            out = out + prev * cw[2 - d:3 - d]
        return out

    gate = conv(wg_ref, 0)
    val = conv(wv_ref, 1)
    act = gate / (1.0 + jnp.exp(-gate)) * val
    acc_ref[...] += jnp.dot(act.astype(BF16), wd_ref[...], preferred_element_type=F32)

    @pl.when(f == pl.num_programs(1) - 1)
    def _():
        o_ref[...] = x_ref[...] + acc_ref[...]


def _ffn(x2, g, wup, cw, cb, wd, seq):
    n = x2.shape[0]
    tm = 512
    nf = D_FF // FF_CHUNK
    hb = tm // FF_HALO
    return pl.pallas_call(
        functools.partial(_ffn_kernel, tiles_per_seq=seq // tm),
        grid=(n // tm, nf),
        in_specs=[pl.BlockSpec((tm, D_MODEL), lambda i, f: (i, 0)),
                  pl.BlockSpec((FF_HALO, D_MODEL), lambda i, f: (jnp.maximum(i * hb - 1, 0), 0)),
                  pl.BlockSpec((1, D_MODEL), lambda i, f: (0, 0)),
                  pl.BlockSpec((D_MODEL, FF_CHUNK), lambda i, f: (0, f)),
                  pl.BlockSpec((D_MODEL, FF_CHUNK), lambda i, f: (0, nf + f)),
                  pl.BlockSpec((2, 1, CONV_W, FF_CHUNK), lambda i, f: (0, f, 0, 0)),
                  pl.BlockSpec((2, 1, 1, FF_CHUNK), lambda i, f: (0, f, 0, 0)),
                  pl.BlockSpec((FF_CHUNK, D_MODEL), lambda i, f: (f, 0))],
        out_specs=pl.BlockSpec((tm, D_MODEL), lambda i, f: (i, 0)),
        out_shape=jax.ShapeDtypeStruct((n, D_MODEL), F32),
        scratch_shapes=[pltpu.VMEM((tm + FF_HALO, D_MODEL), BF16), pltpu.VMEM((tm, D_MODEL), F32)],
        compiler_params=_cparams(("parallel", "arbitrary")),
        name="convffn",
    )(x2, x2, g, wup, wup, cw, cb, wd)


def _chunk(a, c):
    lead = a.shape[:-2]
    s = a.shape[-2]
    return a.reshape(lead + (s // c, c, LANES))


def _rope_tables(seq):
    def tab(dim, reps):
        inv = ROPE_THETA ** (-jnp.arange(0, dim, 2, dtype=F32) / dim)
        ang = jnp.arange(seq, dtype=F32)[:, None] * inv[None, :]
        cos, sin = jnp.cos(ang), jnp.sin(ang)
        return (jnp.tile(jnp.concatenate([cos, cos], axis=1), (1, reps)),
                jnp.tile(jnp.concatenate([-sin, sin], axis=1), (1, reps)))
    c64, s64 = tab(HEAD_DIM, 256 // HEAD_DIM)
    c32, s32 = tab(IDX_DIM, 256 // IDX_DIM)
    a64 = np.kron(np.eye(256 // HEAD_DIM), np.full((HEAD_DIM, HEAD_DIM), 1.0 / HEAD_DIM)).astype(np.float32)
    a32 = np.kron(np.eye(LANES // IDX_DIM), np.full((IDX_DIM, IDX_DIM), 1.0 / IDX_DIM)).astype(np.float32)
    return c64, s64, c32, s32, jnp.asarray(a64), jnp.asarray(a32)


def _cmp_to_slc(seq, ncp):
    c_start = np.arange(ncp) * CMP_STRIDE
    s_start = np.arange(LANES) * SLC_BLOCK
    m = ((c_start[:, None] < s_start[None, :] + SLC_BLOCK) & (c_start[:, None] + CMP_LEN > s_start[None, :]))
    real = (np.arange(ncp) < (seq - CMP_LEN) // CMP_STRIDE + 1)[:, None] & (np.arange(LANES) < seq // SLC_BLOCK)[None, :]
    return jnp.asarray((m & real).astype(np.float32))


def _pad_w_in(w):
    z = lambda k: jnp.zeros((w.shape[0], k), w.dtype)
    return jnp.concatenate([w[:, :_O_AIW_END], z(C_BQ - _O_AIW_END), w[:, _O_AIW_END:_O_CG_END],
                            z(C_DQ - C_CG - 3 * BRANCH_HEADS), w[:, _O_CG_END:_O_GBR], z(C_GBR - W_MIX),
                            w[:, _O_GBR:]], axis=1)


def _pad_w_in_t(w):
    wt = w.T
    z = lambda k: jnp.zeros((k, w.shape[0]), w.dtype)
    return jnp.concatenate([wt[:_O_AIW_END], z(C_BQ - _O_AIW_END), wt[_O_AIW_END:_O_CG_END],
                            z(C_DQ - C_CG - 3 * BRANCH_HEADS), wt[_O_CG_END:_O_GBR], z(C_GBR - W_MIX),
                            wt[_O_GBR:]], axis=0)


def _gain_row(a_qk_g, a_lat_g, a_idx_k_g, b_qk_g, c_qk_g, d_qk_g):
    one = lambda k: jnp.ones((k,), F32)
    zero = lambda k: jnp.zeros((k,), F32)
    t4 = lambda g: jnp.tile(g, BRANCH_HEADS)
    row = jnp.concatenate([
        t4(a_qk_g[0]) * ATTN_SCALE, a_lat_g, one(256), a_idx_k_g, zero(LANES - IDX_DIM),
        t4(b_qk_g[0]) * ATTN_SCALE, t4(b_qk_g[1]), one(256),
        t4(c_qk_g[0]) * ATTN_SCALE, one(LANES), c_qk_g[2], one(HEAD_DIM), c_qk_g[3], one(HEAD_DIM), one(LANES),
        t4(d_qk_g[0]) * ATTN_SCALE, jnp.tile(d_qk_g[1], D_KV_HEADS), one(LANES)])
    return row[None, :]


def _mixers(proj, b, s, l, tabs, c2s, a_qk_g, a_lat_g, a_kv_up, a_idx_k_g, b_qk_g, c_qk_g, c_cmp_pe,
            c_cmp_w1, c_cmp_w2, d_qk_g, d_sink):
    gains = _gain_row(a_qk_g[l], a_lat_g[l], a_idx_k_g[l], b_qk_g[l], c_qk_g[l], d_qk_g[l])
    g2 = jnp.concatenate([a_qk_g[l][1], jnp.ones((HEAD_DIM,), F32)])[None, :]
    (qa, iq, ikw, ik, kxa, ova, qb, kxb, ovb, kmean, qc, kvc, kxs, ovs, kvw, gc, qd, kvd) = _prep(
        proj, gains, g2, a_kv_up[l], tabs, b, s)
    flat = lambda y: y.reshape(b * s, BRANCH_WIDTH)
    ya = flat(_dsa(qa, iq, ikw, _chunk(ik, KC), _chunk(kxa, KC), _chunk(ova, KC), s))
    nb = s // MOBA_BLOCK
    km = kmean.reshape(b, nb, BRANCH_HEADS, HEAD_DIM).transpose(0, 2, 1, 3)
    km = jnp.pad(km, ((0, 0), (0, 0), (0, LANES - nb), (0, LANES - HEAD_DIM)))
    kcm = min(MOBA_KC, s)
    yb = flat(_moba(qb, km, _chunk(kxb, kcm), _chunk(ovb, kcm), s))
    nrow = s // CMP_STRIDE
    w1 = c_cmp_w1[l].reshape(2, CMP_LEN, HEAD_DIM, CMP_HIDDEN)
    zw = jnp.zeros_like(w1[0])
    w1 = jnp.concatenate([jnp.concatenate([w1[0], zw], axis=2), jnp.concatenate([zw, w1[1]], axis=2)], axis=1)
    w2 = c_cmp_w2[l]
    w2 = jnp.concatenate([jnp.pad(w2[0], ((0, 0), (0, HEAD_DIM))), jnp.pad(w2[1], ((0, 0), (HEAD_DIM, 0)))], axis=0)
    pe = jnp.concatenate([c_cmp_pe[l][0], c_cmp_pe[l][1]], axis=1)
    gk = jnp.concatenate([c_qk_g[l][1], jnp.ones((HEAD_DIM,), F32)])[None, :]
    cmp = _compress(kvc, w1, pe, w2, gk)
    ncp = c2s.shape[0]
    if ncp > nrow:
        cmp = jnp.pad(cmp, ((0, 0), (0, ncp - nrow), (0, 0)))
    yc = flat(_nsa(qc, gc, cmp, c2s, _chunk(kxs, KC), _chunk(ovs, KC), _chunk(kvw, QBLK), s))
    yd = flat(_swa(d_sink[l], qd, _chunk(kvd, QBLK), s))
    return ya, yb, yc, yd


def kernel(x, norm1_g, w_in, a_qk_g, a_lat_g, a_kv_up, a_idx_k_g, b_qk_g, c_qk_g, c_cmp_pe, c_cmp_w1,
           c_cmp_w2, d_qk_g, d_sink, w_branch, w_out, norm2_g, w_up, conv_w, conv_b, w_down):
    b, s, _ = x.shape
    depth = w_in.shape[0]
    assert s % KC == 0 and s >= NSA_WINDOW + NSA_QB and KC >= min(DSA_TOPK, s // 4)
    tabs = _rope_tables(s)
    ncp = max(LANES, s // CMP_STRIDE)
    c2s = _cmp_to_slc(s, ncp)
    x2 = x.reshape(b * s, D_MODEL)
    nf = D_FF // FF_CHUNK
    for l in range(depth):
        proj = _inproj(x2, norm1_g[l][None, :], _pad_w_in_t(w_in[l]).astype(BF16))
        ys = _mixers(proj, b, s, l, tabs, c2s, a_qk_g, a_lat_g, a_kv_up, a_idx_k_g, b_qk_g, c_qk_g,
                     c_cmp_pe, c_cmp_w1, c_cmp_w2, d_qk_g, d_sink)
        x2 = _merge(x2, ys, proj, w_branch[l].astype(BF16), w_out[l].astype(BF16))
        cw = conv_w[l].reshape(CONV_W, 2, nf, FF_CHUNK).transpose(1, 2, 0, 3)
        cb = conv_b[l].reshape(2, nf, 1, FF_CHUNK)
        x2 = _ffn(x2, norm2_g[l][None, :], w_up[l].astype(BF16), cw, cb, w_down[l].astype(BF16), s)
    return x2.reshape(b, s, D_MODEL)
```
